```python
import math
import jax, jax.numpy as jnp
from jax import lax
import numpy as np

D_MODEL = 2048
BATCH = 2
SEQ = 8192
DEPTH = 2

GRID_W = 64
CTX_LEN = 256
Q_BLOCK = 128
ROPE_BASE = 10000.0
EPS = 1e-6

MLA_HEADS = 4
MLA_Q_RANK = 384
MLA_KV_RANK = 256
MLA_NOPE = 128
MLA_ROPE = 64
MLA_V = 128
DIFF_HEADS = 4
DIFF_QK = 64
DIFF_V = 128
FNET_GROUPS = 4
FNET_GROUP_DIM = 128
FNET_WIDTH = FNET_GROUPS * FNET_GROUP_DIM
CONV_WIDTH = 512
CONV_KERNEL = 31
CONV_GROUPS = 4
N_BRANCH = 4
BRANCH_W = 512
IN_MLA = MLA_Q_RANK + MLA_KV_RANK + MLA_ROPE
IN_DIFF = DIFF_HEADS * (4 * DIFF_QK + DIFF_V)
IN_FNET = FNET_WIDTH
IN_CONV = 2 * CONV_WIDTH
D_IN = IN_MLA + IN_DIFF + IN_FNET + IN_CONV
PEER_HEADS = 8
PEER_NKEYS = 128
PEER_EXPERTS = PEER_NKEYS * PEER_NKEYS
PEER_DKEY = 256
PEER_TOPK = 16
PEER_CHUNK = 128

kernel_name = "hybrid_gated_mla_diff_fnet_conformer_peer"


def rms_norm(x, g):
    xf = x.astype(jnp.float32)
    y = xf * lax.rsqrt(jnp.mean(xf * xf, axis=-1, keepdims=True) + EPS)
    return (y * g.astype(jnp.float32)).astype(x.dtype)


def modulate(h, shift, scale):
    return h * (1 + scale) + shift


def axial_rope_tables(L, dim):
    rows = L // GRID_W
    row = jnp.repeat(jnp.arange(rows, dtype=jnp.float32), GRID_W)
    col = jnp.tile(jnp.arange(GRID_W, dtype=jnp.float32), rows)
    nf = dim // 4
    inv = ROPE_BASE ** (-jnp.arange(nf, dtype=jnp.float32) / nf)
    ang = jnp.concatenate([row[:, None] * inv, col[:, None] * inv], axis=-1)
    return (jnp.cos(ang), jnp.sin(ang))


def apply_rope(x, cs):
    if cs is None:
        return x
    cos, sin = cs
    half = x.shape[-1] // 2
    shape = (1, x.shape[1]) + (1,) * (x.ndim - 3) + (half,)
    cos = cos.reshape(shape)
    sin = sin.reshape(shape)
    xf = x.astype(jnp.float32)
    x1, x2 = xf[..., :half], xf[..., half:]
    return jnp.concatenate([x1 * cos - x2 * sin, x2 * cos + x1 * sin], axis=-1).astype(x.dtype)


def mixer_projections(hn, p, rope_mla, rope_diff):
    B, L, _ = hn.shape
    z = hn @ p["w_in"]
    offs = list(np.cumsum([MLA_Q_RANK, MLA_KV_RANK, MLA_ROPE, IN_DIFF, IN_FNET]))
    q_lat, kv_lat, k_rope, zd, zf, zc = jnp.split(z, offs, axis=-1)
    q = (rms_norm(q_lat, p["g_q_mla"]) @ p["w_uq"]).reshape(B, L, MLA_HEADS, MLA_NOPE + MLA_ROPE)
    q_nope = q[..., :MLA_NOPE]
    q_rope = apply_rope(q[..., MLA_NOPE:], rope_mla)
    kv = (rms_norm(kv_lat, p["g_kv_mla"]) @ p["w_ukv"]).reshape(B, L, MLA_HEADS, MLA_NOPE + MLA_V)
    k_nope = kv[..., :MLA_NOPE]
    v_mla = kv[..., MLA_NOPE:]
    k_rope = apply_rope(k_rope, rope_mla)
    dq, dk, dv = jnp.split(zd, 3, axis=-1)
    dq = apply_rope(dq.reshape(B, L, DIFF_HEADS, 2, DIFF_QK), rope_diff)
    dk = apply_rope(dk.reshape(B, L, DIFF_HEADS, 2, DIFF_QK), rope_diff)
    dv = dv.reshape(B, L, DIFF_HEADS, DIFF_V)
    return dict(q_nope=q_nope, q_rope=q_rope, k_nope=k_nope, k_rope=k_rope, v_mla=v_mla,
                dq=dq, dk=dk, dv=dv, zf=zf, zc=zc)


def mla_attend(q_nope, q_rope, k_nope, k_rope, v):
    scale = (MLA_NOPE + MLA_ROPE) ** -0.5
    s = (jnp.einsum('bqhd,bkhd->bhqk', q_nope, k_nope)
         + jnp.einsum('bqhd,bkd->bhqk', q_rope, k_rope)) * scale
    pr = jax.nn.softmax(s.astype(jnp.float32), axis=-1).astype(v.dtype)
    return jnp.einsum('bhqk,bkhd->bqhd', pr, v)


def diff_attend(q, k, v, lam):
    s = jnp.einsum('bqhcd,bkhcd->bchqk', q, k) * (DIFF_QK ** -0.5)
    pr = jax.nn.softmax(s.astype(jnp.float32), axis=-1)
    a = pr[:, 0] - lam * pr[:, 1]
    return jnp.einsum('bhqk,bkhd->bqhd', a.astype(v.dtype), v)


def sweep_blocks(fn, qs, kv):
    B, L = qs[0].shape[:2]
    nb = L // Q_BLOCK
    qb = tuple(jnp.swapaxes(q.reshape((B, nb, Q_BLOCK) + q.shape[2:]), 0, 1) for q in qs)
    out = lax.map(lambda blk: fn(*blk, *kv), qb)
    return jnp.swapaxes(out, 0, 1).reshape((B, L) + out.shape[3:])


def fourier_mix(zf):
    B, L, _ = zf.shape
    f = zf.astype(jnp.float32).reshape(B, L, FNET_GROUPS, FNET_GROUP_DIM)
    y = jnp.fft.fft2(f, axes=(1, 3), norm="ortho").real
    return y.reshape(B, L, FNET_WIDTH).astype(zf.dtype)


def conformer_conv(zc, w_dw, b_dw, g, b):
    B, L, _ = zc.shape
    a, gt = jnp.split(zc, 2, axis=-1)
    u = a * jax.nn.sigmoid(gt)
    pad = CONV_KERNEL // 2
    y = lax.conv_general_dilated(u, w_dw[:, None, :], window_strides=(1,), padding=[(pad, pad)],
                                 dimension_numbers=('NWC', 'WIO', 'NWC'),
                                 feature_group_count=CONV_WIDTH) + b_dw
    yf = y.astype(jnp.float32).reshape(B, L, CONV_GROUPS, CONV_WIDTH // CONV_GROUPS)
    mu = jnp.mean(yf, axis=-1, keepdims=True)
    var = jnp.mean(jnp.square(yf - mu), axis=-1, keepdims=True)
    yn = ((yf - mu) * lax.rsqrt(var + 1e-5)).reshape(B, L, CONV_WIDTH)
    yn = yn * g.astype(jnp.float32) + b.astype(jnp.float32)
    return jax.nn.silu(yn).astype(zc.dtype)


def token_mix(hn, P, kv_mla, kv_diff, p, lam, lam_init, blocked):
    B, L, _ = hn.shape
    if blocked:
        o_mla = sweep_blocks(mla_attend, (P["q_nope"], P["q_rope"]), kv_mla)
        o_diff = sweep_blocks(diff_attend, (P["dq"],), kv_diff + (lam,))
    else:
        o_mla = mla_attend(P["q_nope"], P["q_rope"], *kv_mla)
        o_diff = diff_attend(P["dq"], *kv_diff, lam)
    o_diff = rms_norm(o_diff, p["g_subln"]) * (1.0 - lam_init)
    branches = (o_mla.reshape(B, L, BRANCH_W),
                fourier_mix(P["zf"]),
                conformer_conv(P["zc"], p["w_dw"], p["b_dw"], p["g_conv_norm"], p["b_conv_norm"]),
                o_diff.reshape(B, L, BRANCH_W))
    y = None
    for n, br in enumerate(branches):
        term = jax.nn.sigmoid(hn @ p["w_gate"][n]) * (br @ p["w_branch"][n])
        y = term if y is None else y + term
    return y @ p["w_out"]


def peer(hn, w_q, sub_keys, U, V):
    B, L, D = hn.shape
    nb = (B * L) // PEER_CHUNK

    def chunk(xc):
        C = xc.shape[0]
        q = (xc @ w_q).reshape(C, PEER_HEADS, 2, PEER_DKEY // 2)
        s = jnp.einsum('chpd,pkd->chpk', q, sub_keys)
        s1, i1 = lax.top_k(s[:, :, 0], PEER_TOPK)
        s2, i2 = lax.top_k(s[:, :, 1], PEER_TOPK)
        cand = (s1[..., :, None] + s2[..., None, :]).reshape(C, PEER_HEADS, PEER_TOPK * PEER_TOPK)
        cidx = (i1[..., :, None] * PEER_NKEYS + i2[..., None, :]).reshape(C, PEER_HEADS, PEER_TOPK * PEER_TOPK)
        sc, j = lax.top_k(cand, PEER_TOPK)
        idx = jnp.take_along_axis(cidx, j, axis=-1)
        g = jax.nn.softmax(sc.astype(jnp.float32), axis=-1).astype(xc.dtype)
        u = jnp.take(U, idx, axis=0)
        act = jax.nn.gelu(jnp.einsum('chkd,cd->chk', u, xc), approximate=False)
        v = jnp.take(V, idx, axis=0)
        return jnp.einsum('chk,chkd->cd', g * act, v)

    out = lax.map(chunk, hn.reshape(nb, PEER_CHUNK, D))
    return out.reshape(B, L, D)


def layer(xl, xc, p, mod_l, mod_c, rope_mla, rope_diff, lam_init, need_ctx):
    sh1l, sc1l, g1l, sh2l, sc2l, g2l = jnp.split(mod_l[:, None, :], 6, axis=-1)
    sh1c, sc1c, g1c, sh2c, sc2c, g2c = jnp.split(mod_c[None, None, :], 6, axis=-1)
    hl = modulate(rms_norm(xl, p["g_norm1"]), sh1l, sc1l)
    hc = modulate(rms_norm(xc, p["g_norm1"]), sh1c, sc1c)
    Pl = mixer_projections(hl, p, rope_mla, rope_diff)
    Pc = mixer_projections(hc, p, None, None)
    lq1, lk1, lq2, lk2 = (t.astype(jnp.float32) for t in (p["lam_q1"], p["lam_k1"], p["lam_q2"], p["lam_k2"]))
    lam = jnp.exp(jnp.sum(lq1 * lk1)) - jnp.exp(jnp.sum(lq2 * lk2)) + lam_init
    kv_mla_c = (Pc["k_nope"], Pc["k_rope"], Pc["v_mla"])
    kv_diff_c = (Pc["dk"], Pc["dv"])
    kv_mla_l = tuple(jnp.concatenate([a, b], axis=1) for a, b in
                     zip(kv_mla_c, (Pl["k_nope"], Pl["k_rope"], Pl["v_mla"])))
    kv_diff_l = tuple(jnp.concatenate([a, b], axis=1) for a, b in
                      zip(kv_diff_c, (Pl["dk"], Pl["dv"])))
    xl = xl + g1l * token_mix(hl, Pl, kv_mla_l, kv_diff_l, p, lam, lam_init, True)
    xl = xl + g2l * peer(modulate(rms_norm(xl, p["g_norm2"]), sh2l, sc2l),
                         p["w_peer_q"], p["peer_keys"], p["peer_u"], p["peer_v"])
    if need_ctx:
        xc = xc + g1c * token_mix(hc, Pc, kv_mla_c, kv_diff_c, p, lam, lam_init, False)
        xc = xc + g2c * peer(modulate(rms_norm(xc, p["g_norm2"]), sh2c, sc2c),
                             p["w_peer_q"], p["peer_keys"], p["peer_u"], p["peer_v"])
    return xl, xc


def setup_inputs(seed: int = 0) -> dict:
    key = jax.random.key(seed)
    ks = jax.random.split(key, 32)
    D = D_MODEL

    def nrm(k, shape, s):
        return jax.random.normal(k, shape, jnp.float32) * s

    return dict(
        x=nrm(ks[0], (BATCH, SEQ, D), 1.0),
        c=nrm(ks[1], (BATCH, D), 1.0),
        ctx=nrm(ks[2], (BATCH, CTX_LEN, D), 1.0),
        c_ctx=nrm(ks[3], (D,), 1.0),
        w_ada=nrm(ks[4], (DEPTH, D, 6 * D), 0.5 * D ** -0.5),
        b_ada=nrm(ks[5], (DEPTH, 6 * D), 0.02),
        g_norm1=1.0 + nrm(ks[6], (DEPTH, D), 0.02),
        g_norm2=1.0 + nrm(ks[7], (DEPTH, D), 0.02),
        w_in=nrm(ks[8], (DEPTH, D, D_IN), D ** -0.5),
        g_q_mla=1.0 + nrm(ks[9], (DEPTH, MLA_Q_RANK), 0.02),
        w_uq=nrm(ks[10], (DEPTH, MLA_Q_RANK, MLA_HEADS * (MLA_NOPE + MLA_ROPE)), MLA_Q_RANK ** -0.5),
        g_kv_mla=1.0 + nrm(ks[11], (DEPTH, MLA_KV_RANK), 0.02),
        w_ukv=nrm(ks[12], (DEPTH, MLA_KV_RANK, MLA_HEADS * (MLA_NOPE + MLA_V)), MLA_KV_RANK ** -0.5),
        lam_q1=nrm(ks[13], (DEPTH, DIFF_QK), 0.1),
        lam_k1=nrm(ks[14], (DEPTH, DIFF_QK), 0.1),
        lam_q2=nrm(ks[15], (DEPTH, DIFF_QK), 0.1),
        lam_k2=nrm(ks[16], (DEPTH, DIFF_QK), 0.1),
        g_subln=1.0 + nrm(ks[17], (DEPTH, DIFF_V), 0.02),
        w_dw=nrm(ks[18], (DEPTH, CONV_KERNEL, CONV_WIDTH), CONV_KERNEL ** -0.5),
        b_dw=nrm(ks[19], (DEPTH, CONV_WIDTH), 0.02),
        g_conv_norm=1.0 + nrm(ks[20], (DEPTH, CONV_WIDTH), 0.02),
        b_conv_norm=nrm(ks[21], (DEPTH, CONV_WIDTH), 0.02),
        w_gate=nrm(ks[22], (DEPTH, N_BRANCH, D, D), D ** -0.5),
        w_branch=nrm(ks[23], (DEPTH, N_BRANCH, BRANCH_W, D), BRANCH_W ** -0.5),
        w_out=nrm(ks[24], (DEPTH, D, D), D ** -0.5),
        w_peer_q=nrm(ks[25], (DEPTH, D, PEER_HEADS * PEER_DKEY), D ** -0.5),
        peer_keys=nrm(ks[26], (DEPTH, 2, PEER_NKEYS, PEER_DKEY // 2), (PEER_DKEY // 2) ** -0.5),
        peer_u=nrm(ks[27], (DEPTH, PEER_EXPERTS, D), D ** -0.5),
        peer_v=nrm(ks[28], (DEPTH, PEER_EXPERTS, D), 1.0),
        g_final=1.0 + nrm(ks[29], (D,), 0.02),
    )


def reference(x, c, ctx, c_ctx, w_ada, b_ada, g_norm1, g_norm2, w_in, g_q_mla, w_uq, g_kv_mla,
              w_ukv, lam_q1, lam_k1, lam_q2, lam_k2, g_subln, w_dw, b_dw, g_conv_norm, b_conv_norm,
              w_gate, w_branch, w_out, w_peer_q, peer_keys, peer_u, peer_v, g_final):
    L = x.shape[1]
    rope_mla = axial_rope_tables(L, MLA_ROPE)
    rope_diff = axial_rope_tables(L, DIFF_QK)
    xl, xc = x, ctx
    for i in range(DEPTH):
        p = dict(g_norm1=g_norm1[i], g_norm2=g_norm2[i], w_in=w_in[i], g_q_mla=g_q_mla[i],
                 w_uq=w_uq[i], g_kv_mla=g_kv_mla[i], w_ukv=w_ukv[i], lam_q1=lam_q1[i],
                 lam_k1=lam_k1[i], lam_q2=lam_q2[i], lam_k2=lam_k2[i], g_subln=g_subln[i],
                 w_dw=w_dw[i], b_dw=b_dw[i], g_conv_norm=g_conv_norm[i], b_conv_norm=b_conv_norm[i],
                 w_gate=w_gate[i], w_branch=w_branch[i], w_out=w_out[i], w_peer_q=w_peer_q[i],
                 peer_keys=peer_keys[i], peer_u=peer_u[i], peer_v=peer_v[i])
        mod_l = jax.nn.silu(c) @ w_ada[i] + b_ada[i]
        mod_c = jax.nn.silu(c_ctx) @ w_ada[i] + b_ada[i]
        lam_init = 0.8 - 0.6 * math.exp(-0.3 * i)
        xl, xc = layer(xl, xc, p, mod_l, mod_c, rope_mla, rope_diff, lam_init, i < DEPTH - 1)
    return rms_norm(xl, g_final)
```

```python
import functools
import math

import numpy as np
import jax
import jax.numpy as jnp
from jax import lax
from jax.experimental import pallas as pl
from jax.experimental.pallas import tpu as pltpu

F32 = jnp.float32
BF16 = jnp.bfloat16

GRID_W = 64
ROPE_BASE = 10000.0
EPS = 1e-6
MLA_HEADS = 4
MLA_Q_RANK = 384
MLA_KV_RANK = 256
MLA_NOPE = 128
MLA_ROPE = 64
MLA_V = 128
DIFF_HEADS = 4
DIFF_QK = 64
DIFF_V = 128
FNET_GROUPS = 4
FNET_GROUP_DIM = 128
FNET_WIDTH = 512
CONV_WIDTH = 512
CONV_KERNEL = 31
CONV_GROUPS = 4
N_BRANCH = 4
BRANCH_W = 512
IN_MLA = MLA_Q_RANK + MLA_KV_RANK + MLA_ROPE
IN_DIFF = DIFF_HEADS * (4 * DIFF_QK + DIFF_V)
PEER_HEADS = 8
PEER_NKEYS = 128
PEER_EXPERTS = PEER_NKEYS * PEER_NKEYS
PEER_DKEY = 256
PEER_TOPK = 16

LANES = 128
ROW_TILE = 256
FLAT_TILE = 512
VMEM_LIMIT = 56 * 1024 * 1024
NEG_INF = float("-inf")


def _cparams(sem):
    return pltpu.CompilerParams(dimension_semantics=sem, vmem_limit_bytes=VMEM_LIMIT)


def _pick(n, cands):
    for c in cands:
        if n % c == 0:
            return c
    raise ValueError(f"no tile for {n} in {cands}")


def _ada_kernel(cb_ref, w_ref, b_ref, o_ref, *, rows, tn):
    outs = []
    for r in range(rows):
        a = cb_ref[r]
        a = a * jax.nn.sigmoid(a)
        cols = [jnp.sum(w_ref[:, j * LANES:(j + 1) * LANES] * a, axis=0, keepdims=True)
                for j in range(tn // LANES)]
        outs.append(jnp.concatenate(cols, axis=1) + b_ref[...])
    outs.append(jnp.zeros((8 - rows, tn), F32))
    o_ref[...] = jnp.concatenate(outs, axis=0)


def ada_modulation(cond, w_ada, b_ada):
    rows, d = cond.shape
    depth, _, n = w_ada.shape
    tn = 512
    cb = jnp.broadcast_to(cond[:, :, None], (rows, d, LANES))
    return pl.pallas_call(
        functools.partial(_ada_kernel, rows=rows, tn=tn),
        grid=(depth, n // tn),
        in_specs=[pl.BlockSpec((rows, d, LANES), lambda l, j: (0, 0, 0)),
                  pl.BlockSpec((None, d, tn), lambda l, j: (l, 0, j)),
                  pl.BlockSpec((None, 1, tn), lambda l, j: (l, 0, j))],
        out_specs=pl.BlockSpec((None, 8, tn), lambda l, j: (l, 0, j)),
        out_shape=jax.ShapeDtypeStruct((depth, 8, n), F32),
        compiler_params=_cparams(("arbitrary", "arbitrary")),
        name="ada_modulation",
    )(cb, w_ada, b_ada.reshape(depth, 1, n))


def _norm_mod_kernel(*refs, has_delta, gate_row, shift_row, scale_row):
    if has_delta:
        x_ref, d_ref, mg_ref, ms_ref, g_ref, xo_ref, h_ref = refs
        x = x_ref[...] + mg_ref[gate_row:gate_row + 1, :] * d_ref[...].astype(F32)
        xo_ref[...] = x
    else:
        x_ref, ms_ref, g_ref, h_ref = refs
        x = x_ref[...]
    y = x * lax.rsqrt(jnp.mean(x * x, axis=-1, keepdims=True) + EPS) * g_ref[...]
    h = y * (1.0 + ms_ref[scale_row:scale_row + 1, :]) + ms_ref[shift_row:shift_row + 1, :]
    h_ref[...] = h.astype(BF16)


def norm_modulate(x, g, mod_ss, shift_row, scale_row, nctx, delta=None, mod_gate=None, gate_row=None):
    B, T, D = x.shape
    tm = ROW_TILE
    row_spec = pl.BlockSpec((None, tm, D), lambda b, i: (b, i, 0))
    mod_spec = pl.BlockSpec((None, 8, D), lambda b, i: (jnp.where(i < nctx, B, b), 0, 0))
    g_spec = pl.BlockSpec((1, D), lambda b, i: (0, 0))
    h_shape = jax.ShapeDtypeStruct((B, T, D), BF16)
    kern = functools.partial(_norm_mod_kernel, has_delta=delta is not None, gate_row=gate_row,
                             shift_row=shift_row, scale_row=scale_row)
    if delta is None:
        return pl.pallas_call(
            kern, grid=(B, T // tm), in_specs=[row_spec, mod_spec, g_spec], out_specs=row_spec,
            out_shape=h_shape, compiler_params=_cparams(("parallel", "parallel")), name="norm_modulate",
        )(x, mod_ss, g.reshape(1, D))
    return pl.pallas_call(
        kern, grid=(B, T // tm), in_specs=[row_spec, row_spec, mod_spec, mod_spec, g_spec],
        out_specs=[row_spec, row_spec],
        out_shape=[jax.ShapeDtypeStruct((B, T, D), F32), h_shape],
        compiler_params=_cparams(("parallel", "parallel")), name="residual_norm_modulate",
    )(x, delta, mod_gate, mod_ss, g.reshape(1, D))


def _mm_kernel(x_ref, w_ref, o_ref):
    o_ref[...] = jnp.dot(x_ref[...], w_ref[...], preferred_element_type=F32).astype(o_ref.dtype)


def matmul(x, w, out_dtype, name):
    M, K = x.shape
    N = w.shape[1]
    tm = _pick(M, (FLAT_TILE, 256, 128))
    tn = _pick(N, (1024, 768, 512, 256, 128))
    return pl.pallas_call(
        _mm_kernel, grid=(N // tn, M // tm),
        in_specs=[pl.BlockSpec((tm, K), lambda j, i: (i, 0)),
                  pl.BlockSpec((K, tn), lambda j, i: (0, j))],
        out_specs=pl.BlockSpec((tm, tn), lambda j, i: (i, j)),
        out_shape=jax.ShapeDtypeStruct((M, N), out_dtype),
        compiler_params=_cparams(("parallel", "parallel")), name=name,
    )(x, w)


def _rope(v, cos, sin, lane):
    r = jnp.where((lane % 64) < 32, pltpu.roll(v, 96, 1), pltpu.roll(v, 32, 1))
    return v * cos + r * sin


def _proj_kernel(zm_ref, zd_ref, cos_ref, sin_ref, gq_ref, gkv_ref, wuq_ref, wukv_ref,
                 qm_ref, km_ref, vm_ref, qd_ref, kd_ref, *, mla_scale, diff_scale):
    tm = zm_ref.shape[0]
    cos = cos_ref[...]
    sin = sin_ref[...]
    lane = lax.broadcasted_iota(jnp.int32, (tm, LANES), 1)

    def rms(v, g):
        return v * lax.rsqrt(jnp.mean(v * v, axis=-1, keepdims=True) + EPS) * g

    zm = zm_ref[...].astype(F32)
    qn = rms(zm[:, :MLA_Q_RANK], gq_ref[...]).astype(BF16)
    kvn = rms(zm[:, MLA_Q_RANK:MLA_Q_RANK + MLA_KV_RANK], gkv_ref[...]).astype(BF16)
    k_rope = _rope(zm[:, MLA_Q_RANK + MLA_KV_RANK:], cos, sin, lane).astype(BF16)
    q = jnp.dot(qn, wuq_ref[...], preferred_element_type=F32)
    kv = jnp.dot(kvn, wukv_ref[...], preferred_element_type=F32)
    for h in range(MLA_HEADS):
        qm_ref[h, :, 0:LANES] = (q[:, 256 * h:256 * h + LANES] * mla_scale).astype(BF16)
        qm_ref[h, :, LANES:2 * LANES] = (
            _rope(q[:, 256 * h + LANES:256 * h + 2 * LANES], cos, sin, lane) * mla_scale).astype(BF16)
        km_ref[h, :, 0:LANES] = kv[:, LANES * h:LANES * (h + 1)].astype(BF16)
        km_ref[h, :, LANES:2 * LANES] = k_rope
        vm_ref[h] = kv[:, 512 + LANES * h:512 + LANES * (h + 1)].astype(BF16)
    zd = zd_ref[...].astype(F32)
    for h in range(DIFF_HEADS):
        qd_ref[h] = (_rope(zd[:, LANES * h:LANES * (h + 1)], cos, sin, lane) * diff_scale).astype(BF16)
        kd_ref[h] = _rope(zd[:, 512 + LANES * h:512 + LANES * (h + 1)], cos, sin, lane).astype(BF16)


def attention_projections(z_mla, z_diff, cos_t, sin_t, g_q, g_kv, w_uq_p, w_ukv_p):
    B, T, _ = z_mla.shape
    tm = ROW_TILE
    H = MLA_HEADS

    def head_spec(w):
        return pl.BlockSpec((None, H, tm, w), lambda b, i: (b, 0, i, 0))

    def head_shape(w):
        return jax.ShapeDtypeStruct((B, H, T, w), BF16)

    return pl.pallas_call(
        functools.partial(_proj_kernel, mla_scale=float((MLA_NOPE + MLA_ROPE) ** -0.5),
                          diff_scale=float(DIFF_QK ** -0.5)),
        grid=(B, T // tm),
        in_specs=[pl.BlockSpec((None, tm, z_mla.shape[2]), lambda b, i: (b, i, 0)),
                  pl.BlockSpec((None, tm, z_diff.shape[2]), lambda b, i: (b, i, 0)),
                  pl.BlockSpec((tm, LANES), lambda b, i: (i, 0)),
                  pl.BlockSpec((tm, LANES), lambda b, i: (i, 0)),
                  pl.BlockSpec((1, MLA_Q_RANK), lambda b, i: (0, 0)),
                  pl.BlockSpec((1, MLA_KV_RANK), lambda b, i: (0, 0)),
                  pl.BlockSpec(w_uq_p.shape, lambda b, i: (0, 0)),
                  pl.BlockSpec(w_ukv_p.shape, lambda b, i: (0, 0))],
        out_specs=[head_spec(256), head_spec(256), head_spec(128), head_spec(128), head_spec(128)],
        out_shape=[head_shape(256), head_shape(256), head_shape(128), head_shape(128), head_shape(128)],
        compiler_params=_cparams(("parallel", "parallel")), name="attention_projections",
    )(z_mla, z_diff, cos_t, sin_t, g_q.reshape(1, -1), g_kv.reshape(1, -1), w_uq_p, w_ukv_p)


def _ctx_mask(s, qi, ki, tq, tk, ctx_len):
    row = qi * tq + lax.broadcasted_iota(jnp.int32, s.shape, 0)
    col = ki * tk + lax.broadcasted_iota(jnp.int32, s.shape, 1)
    return jnp.where((row < ctx_len) & (col >= ctx_len), NEG_INF, s)


def _online_softmax_step(s, v, m_ref, l_ref, acc_ref):
    m_prev = m_ref[...]
    m_new = jnp.maximum(m_prev, jnp.max(s, axis=-1, keepdims=True))
    alpha = jnp.exp(m_prev - m_new)
    p = jnp.exp(s - m_new)
    l_ref[...] = alpha * l_ref[...] + jnp.sum(p, axis=-1, keepdims=True)
    acc_ref[...] = alpha * acc_ref[...] + jnp.dot(p.astype(BF16), v, preferred_element_type=F32)
    m_ref[...] = m_new


def _mla_att_kernel(q_ref, k_ref, v_ref, o_ref, m_ref, l_ref, acc_ref, *, tq, tk, ctx_len):
    qi = pl.program_id(2)
    ki = pl.program_id(3)

    @pl.when(ki == 0)
    def _():
        m_ref[...] = jnp.full(m_ref.shape, NEG_INF, F32)
        l_ref[...] = jnp.zeros(l_ref.shape, F32)
        acc_ref[...] = jnp.zeros(acc_ref.shape, F32)

    s = lax.dot_general(q_ref[...], k_ref[...], (((1,), (1,)), ((), ())), preferred_element_type=F32)
    s = _ctx_mask(s, qi, ki, tq, tk, ctx_len)
    _online_softmax_step(s, v_ref[...], m_ref, l_ref, acc_ref)

    @pl.when(ki == pl.num_programs(3) - 1)
    def _():
        o_ref[...] = (acc_ref[...] / l_ref[...]).astype(o_ref.dtype)


def mla_attention(qm, km, vm, ctx_len):
    B, H, T, dk = qm.shape
    tq = tk = _pick(T, (768, 256))
    return pl.pallas_call(
        functools.partial(_mla_att_kernel, tq=tq, tk=tk, ctx_len=ctx_len),
        grid=(B, H, T // tq, T // tk),
        in_specs=[pl.BlockSpec((None, None, tq, dk), lambda b, h, i, j: (b, h, i, 0)),
                  pl.BlockSpec((None, None, tk, dk), lambda b, h, i, j: (b, h, j, 0)),
                  pl.BlockSpec((None, None, tk, MLA_V), lambda b, h, i, j: (b, h, j, 0))],
        out_specs=pl.BlockSpec((None, tq, MLA_V), lambda b, h, i, j: (b, i, h)),
        out_shape=jax.ShapeDtypeStruct((B, T, H * MLA_V), BF16),
        scratch_shapes=[pltpu.VMEM((tq, 1), F32), pltpu.VMEM((tq, 1), F32), pltpu.VMEM((tq, MLA_V), F32)],
        compiler_params=_cparams(("parallel", "parallel", "parallel", "arbitrary")), name="mla_attention",
    )(qm, km, vm)


def _diff_att_kernel(q_ref, k_ref, v_ref, lam_ref, g_ref, o_ref,
                     m0_ref, l0_ref, a0_ref, m1_ref, l1_ref, a1_ref, *, tq, tk, ctx_len, lam_init):
    qi = pl.program_id(2)
    ki = pl.program_id(3)

    @pl.when(ki == 0)
    def _():
        for m_ref, l_ref, a_ref in ((m0_ref, l0_ref, a0_ref), (m1_ref, l1_ref, a1_ref)):
            m_ref[...] = jnp.full(m_ref.shape, NEG_INF, F32)
            l_ref[...] = jnp.zeros(l_ref.shape, F32)
            a_ref[...] = jnp.zeros(a_ref.shape, F32)

    q = q_ref[...]
    k = k_ref[...]
    v = v_ref[...]
    first = lax.broadcasted_iota(jnp.int32, q.shape, 1) < DIFF_QK
    zero = jnp.zeros_like(q)
    for qc, m_ref, l_ref, a_ref in ((jnp.where(first, q, zero), m0_ref, l0_ref, a0_ref),
                                    (jnp.where(first, zero, q), m1_ref, l1_ref, a1_ref)):
        s = lax.dot_general(qc, k, (((1,), (1,)), ((), ())), preferred_element_type=F32)
        s = _ctx_mask(s, qi, ki, tq, tk, ctx_len)
        _online_softmax_step(s, v, m_ref, l_ref, a_ref)

    @pl.when(ki == pl.num_programs(3) - 1)
    def _():
        lv = lam_ref[...]
        lam = (jnp.exp(jnp.sum(lv[0:1] * lv[1:2], axis=-1, keepdims=True))
               - jnp.exp(jnp.sum(lv[2:3] * lv[3:4], axis=-1, keepdims=True)) + lam_init)
        o = a0_ref[...] / l0_ref[...] - lam * (a1_ref[...] / l1_ref[...])
        o = o * lax.rsqrt(jnp.mean(o * o, axis=-1, keepdims=True) + EPS) * g_ref[...]
        o_ref[...] = (o * (1.0 - lam_init)).astype(o_ref.dtype)


def diff_attention(qd, kd, z_diff, lam_vecs, g_subln, lam_init, ctx_len):
    B, H, T, dk = qd.shape
    tq = tk = _pick(T, (768, 256))
    v_block0 = (2 * DIFF_HEADS * 2 * DIFF_QK) // DIFF_V
    small = [pltpu.VMEM((tq, 1), F32), pltpu.VMEM((tq, 1), F32), pltpu.VMEM((tq, DIFF_V), F32)]
    return pl.pallas_call(
        functools.partial(_diff_att_kernel, tq=tq, tk=tk, ctx_len=ctx_len, lam_init=lam_init),
        grid=(B, H, T // tq, T // tk),
        in_specs=[pl.BlockSpec((None, None, tq, dk), lambda b, h, i, j: (b, h, i, 0)),
                  pl.BlockSpec((None, None, tk, dk), lambda b, h, i, j: (b, h, j, 0)),
                  pl.BlockSpec((None, tk, DIFF_V), lambda b, h, i, j: (b, j, v_block0 + h)),
                  pl.BlockSpec(lam_vecs.shape, lambda b, h, i, j: (0, 0)),
                  pl.BlockSpec((1, DIFF_V), lambda b, h, i, j: (0, 0))],
        out_specs=pl.BlockSpec((None, tq, DIFF_V), lambda b, h, i, j: (b, i, h)),
        out_shape=jax.ShapeDtypeStruct((B, T, H * DIFF_V), BF16),
        scratch_shapes=small + small,
        compiler_params=_cparams(("parallel", "parallel", "parallel", "arbitrary")), name="diff_attention",
    )(qd, kd, z_diff, lam_vecs, g_subln.reshape(1, -1))


def _dft_mats(n):
    k = np.arange(n)
    ang = 2.0 * np.pi * ((k[:, None] * k[None, :]) % n) / n
    return jnp.asarray(np.cos(ang), BF16), jnp.asarray(np.sin(ang), BF16)


def _channel_dft(x, cc, sc):
    tr, ti = [], []
    for s in range(x.shape[1] // LANES):
        xs = x[:, s * LANES:(s + 1) * LANES]
        tr.append(jnp.dot(xs, cc, preferred_element_type=F32))
        ti.append(-jnp.dot(xs, sc, preferred_element_type=F32))
    return jnp.concatenate(tr, axis=1), jnp.concatenate(ti, axis=1)


def _fnet_stage1_kernel(x_ref, cc_ref, sc_ref, c_ref, s_ref, twc_ref, tws_ref, yr_ref, yi_ref, *, nb):
    tr, ti = _channel_dft(x_ref[...], cc_ref[...], sc_ref[...])
    trb, tib = tr.astype(BF16), ti.astype(BF16)
    c, s = c_ref[...], s_ref[...]
    yr = jnp.dot(c, trb, preferred_element_type=F32) + jnp.dot(s, tib, preferred_element_type=F32)
    yi = jnp.dot(c, tib, preferred_element_type=F32) - jnp.dot(s, trb, preferred_element_type=F32)
    for i in range(nb):
        twc = jnp.concatenate([twc_ref[i]] * FNET_GROUPS, axis=1)
        tws = jnp.concatenate([tws_ref[i]] * FNET_GROUPS, axis=1)
        a = yr[:, i * FNET_WIDTH:(i + 1) * FNET_WIDTH]
        b = yi[:, i * FNET_WIDTH:(i + 1) * FNET_WIDTH]
        yr_ref[i] = a * twc + b * tws
        yi_ref[i] = b * twc - a * tws


def _fnet_stage2_kernel(yr_ref, yi_ref, c_ref, s_ref, o_ref, *, scale):
    xr = (jnp.dot(c_ref[...], yr_ref[...].astype(BF16), preferred_element_type=F32)
          + jnp.dot(s_ref[...], yi_ref[...].astype(BF16), preferred_element_type=F32))
    o_ref[...] = (xr * scale).astype(o_ref.dtype)


def _fnet_direct_kernel(x_ref, cc_ref, sc_ref, c_ref, s_ref, o_ref, *, scale):
    tr, ti = _channel_dft(x_ref[...], cc_ref[...], sc_ref[...])
    xr = (jnp.dot(c_ref[...], tr.astype(BF16), preferred_element_type=F32)
          + jnp.dot(s_ref[...], ti.astype(BF16), preferred_element_type=F32))
    o_ref[...] = (xr * scale).astype(o_ref.dtype)


def _full(a):
    return pl.BlockSpec(a.shape, lambda *_: (0,) * a.ndim)


def fourier_mix_long(zf):
    B, L, W = zf.shape
    n2 = LANES
    n1 = L // n2
    nb = _pick(n1, (8,))
    cc, sc = _dft_mats(FNET_GROUP_DIM)
    c2, s2 = _dft_mats(n2)
    c1, s1 = _dft_mats(n1)
    ang = 2.0 * np.pi * (np.arange(n1)[:, None] * np.arange(n2)[None, :]) / L
    twc = jnp.asarray(np.broadcast_to(np.cos(ang)[:, :, None], (n1, n2, LANES)), F32)
    tws = jnp.asarray(np.broadcast_to(np.sin(ang)[:, :, None], (n1, n2, LANES)), F32)
    x = zf.reshape(B, n2, n1 * W)
    y_shape = jax.ShapeDtypeStruct((B, n1, n2, W), F32)
    y_spec = pl.BlockSpec((None, nb, n2, W), lambda b, j: (b, j, 0, 0))
    tw_spec = pl.BlockSpec((nb, n2, LANES), lambda b, j: (j, 0, 0))
    yr, yi = pl.pallas_call(
        functools.partial(_fnet_stage1_kernel, nb=nb), grid=(B, n1 // nb),
        in_specs=[pl.BlockSpec((None, n2, nb * W), lambda b, j: (b, 0, j)),
                  _full(cc), _full(sc), _full(c2), _full(s2), tw_spec, tw_spec],
        out_specs=[y_spec, y_spec], out_shape=[y_shape, y_shape],
        compiler_params=_cparams(("parallel", "parallel")), name="fnet_stage1",
    )(x, cc, sc, c2, s2, twc, tws)
    cols = n2 * W
    tn = 4096
    y2_spec = pl.BlockSpec((None, n1, tn), lambda b, j: (b, 0, j))
    out = pl.pallas_call(
        functools.partial(_fnet_stage2_kernel, scale=float((L * FNET_GROUP_DIM) ** -0.5)),
        grid=(B, cols // tn),
        in_specs=[y2_spec, y2_spec, _full(c1), _full(s1)],
        out_specs=y2_spec, out_shape=jax.ShapeDtypeStruct((B, n1, cols), BF16),
        compiler_params=_cparams(("parallel", "parallel")), name="fnet_stage2",
    )(yr.reshape(B, n1, cols), yi.reshape(B, n1, cols), c1, s1)
    return out.reshape(B, L, W)


def fourier_mix_short(zf):
    B, L, W = zf.shape
    cc, sc = _dft_mats(FNET_GROUP_DIM)
    c, s = _dft_mats(L)
    spec = pl.BlockSpec((None, L, W), lambda b: (b, 0, 0))
    return pl.pallas_call(
        functools.partial(_fnet_direct_kernel, scale=float((L * FNET_GROUP_DIM) ** -0.5)), grid=(B,),
        in_specs=[spec, _full(cc), _full(sc), _full(c), _full(s)],
        out_specs=spec, out_shape=jax.ShapeDtypeStruct((B, L, W), BF16),
        compiler_params=_cparams(("parallel",)), name="fnet_direct",
    )(zf, cc, sc, c, s)


HALO = 16


def _conv_kernel(prev_ref, cur_ref, next_ref, w_ref, bdw_ref, g_ref, b_ref, o_ref, ext_ref, *, nctx):
    i = pl.program_id(1)
    nt = pl.num_programs(1)
    tm = cur_ref.shape[0]

    def glu(z):
        z = z.astype(F32)
        return z[:, :CONV_WIDTH] * jax.nn.sigmoid(z[:, CONV_WIDTH:])

    has_prev = jnp.logical_and(i != 0, i != nctx)
    has_next = jnp.logical_and(i != nctx - 1, i != nt - 1)
    ext_ref[0:HALO, :] = jnp.where(has_prev, glu(prev_ref[tm - HALO:tm, :]), 0.0)
    ext_ref[HALO:HALO + tm, :] = glu(cur_ref[...])
    ext_ref[HALO + tm:2 * HALO + tm, :] = jnp.where(has_next, glu(next_ref[0:HALO, :]), 0.0)
    pad = CONV_KERNEL // 2
    acc = jnp.zeros((tm, CONV_WIDTH), F32)
    for k in range(CONV_KERNEL):
        acc = acc + w_ref[k:k + 1, :] * ext_ref[pl.ds(HALO - pad + k, tm), :]
    y = acc + bdw_ref[...]
    gw = CONV_WIDTH // CONV_GROUPS
    outs = []
    for gi in range(CONV_GROUPS):
        yg = y[:, gi * gw:(gi + 1) * gw]
        mu = jnp.mean(yg, axis=-1, keepdims=True)
        var = jnp.mean(jnp.square(yg - mu), axis=-1, keepdims=True)
        outs.append((yg - mu) * lax.rsqrt(var + 1e-5))
    yn = jnp.concatenate(outs, axis=1) * g_ref[...] + b_ref[...]
    o_ref[...] = (yn * jax.nn.sigmoid(yn)).astype(o_ref.dtype)


def conformer_conv(zc, w_dw, b_dw, g, b, nctx):
    B, T, W2 = zc.shape
    tm = ROW_TILE
    nt = T // tm
    w_pad = jnp.concatenate([w_dw, jnp.zeros((32 - CONV_KERNEL, CONV_WIDTH), F32)], axis=0)
    vec = pl.BlockSpec((1, CONV_WIDTH), lambda bb, i: (0, 0))
    return pl.pallas_call(
        functools.partial(_conv_kernel, nctx=nctx), grid=(B, nt),
        in_specs=[pl.BlockSpec((None, tm, W2), lambda bb, i: (bb, jnp.maximum(i - 1, 0), 0)),
                  pl.BlockSpec((None, tm, W2), lambda bb, i: (bb, i, 0)),
                  pl.BlockSpec((None, tm, W2), lambda bb, i: (bb, jnp.minimum(i + 1, nt - 1), 0)),
                  pl.BlockSpec((32, CONV_WIDTH), lambda bb, i: (0, 0)), vec, vec, vec],
        out_specs=pl.BlockSpec((None, tm, CONV_WIDTH), lambda bb, i: (bb, i, 0)),
        out_shape=jax.ShapeDtypeStruct((B, T, CONV_WIDTH), BF16),
        scratch_shapes=[pltpu.VMEM((tm + 2 * HALO, CONV_WIDTH), F32)],
        compiler_params=_cparams(("parallel", "parallel")), name="conformer_conv",
    )(zc, zc, zc, w_pad, b_dw.reshape(1, -1), g.reshape(1, -1), b.reshape(1, -1))


def _merge_kernel(h_ref, b0_ref, b1_ref, b2_ref, b3_ref, wg_ref, wb_ref, o_ref):
    h = h_ref[...]
    acc = None
    for n, br in enumerate((b0_ref, b1_ref, b2_ref, b3_ref)):
        gate = jax.nn.sigmoid(jnp.dot(h, wg_ref[n], preferred_element_type=F32))
        term = gate * jnp.dot(br[...], wb_ref[n], preferred_element_type=F32)
        acc = term if acc is None else acc + term
    o_ref[...] = acc.astype(o_ref.dtype)


def gated_merge(h, branches, w_gate, w_branch):
    M, D = h.shape
    N = w_gate.shape[2]
    tm = _pick(M, (FLAT_TILE, 256))
    tn = _pick(N, (512, 256))
    br_spec = pl.BlockSpec((tm, BRANCH_W), lambda j, i: (i, 0))
    return pl.pallas_call(
        _merge_kernel, grid=(N // tn, M // tm),
        in_specs=[pl.BlockSpec((tm, D), lambda j, i: (i, 0)), br_spec, br_spec, br_spec, br_spec,
                  pl.BlockSpec((N_BRANCH, D, tn), lambda j, i: (0, 0, j)),
                  pl.BlockSpec((N_BRANCH, BRANCH_W, tn), lambda j, i: (0, 0, j))],
        out_specs=pl.BlockSpec((tm, tn), lambda j, i: (i, j)),
        out_shape=jax.ShapeDtypeStruct((M, N), BF16),
        compiler_params=_cparams(("parallel", "parallel")), name="gated_merge",
    )(h, *branches, w_gate, w_branch)


def _peer_scores_kernel(q_ref, keys_ref, o_ref):
    for hp in range(2 * PEER_HEADS):
        o_ref[hp] = lax.dot_general(keys_ref[hp % 2], q_ref[:, hp * LANES:(hp + 1) * LANES],
                                    (((1,), (1,)), ((), ())), preferred_element_type=F32)


def peer_scores(q, keys):
    M = q.shape[0]
    tm = _pick(M, (FLAT_TILE, 256))
    nhp = 2 * PEER_HEADS
    return pl.pallas_call(
        _peer_scores_kernel, grid=(M // tm,),
        in_specs=[pl.BlockSpec((tm, q.shape[1]), lambda i: (i, 0)), _full(keys)],
        out_specs=pl.BlockSpec((nhp, PEER_NKEYS, tm), lambda i: (0, 0, i)),
        out_shape=jax.ShapeDtypeStruct((nhp, PEER_NKEYS, M), F32),
        compiler_params=_cparams(("parallel",)), name="peer_scores",
    )(q, keys)


_STAIRCASE = [(a, b) for a in range(PEER_TOPK) for b in range(PEER_TOPK) if (a + 1) * (b + 1) <= PEER_TOPK]
_NCAND = -(-len(_STAIRCASE) // 8) * 8


def _peer_threshold_kernel(s_ref, thr_ref, top_ref, zinv_ref, t1_ref, t2_ref, c_ref):
    tl = s_ref.shape[2]

    def extract(vals, dst_ref):
        v = vals
        m = None
        for r in range(PEER_TOPK):
            m = jnp.max(v, axis=0, keepdims=True)
            if dst_ref is not None:
                dst_ref[r:r + 1, :] = m
            v = jnp.where(v == m, NEG_INF, v)
        return m

    for h in range(PEER_HEADS):
        extract(s_ref[2 * h], t1_ref)
        extract(s_ref[2 * h + 1], t2_ref)
        c_ref[...] = jnp.full(c_ref.shape, NEG_INF, F32)
        for r, (a, b) in enumerate(_STAIRCASE):
            c_ref[r:r + 1, :] = t1_ref[a:a + 1, :] + t2_ref[b:b + 1, :]
        cand = c_ref[...]
        thr = extract(cand, None)
        top = t1_ref[0:1, :] + t2_ref[0:1, :]
        z = jnp.sum(jnp.where(cand >= thr, jnp.exp(cand - top), 0.0), axis=0, keepdims=True)
        thr_ref[h:h + 1, :] = thr
        top_ref[h:h + 1, :] = top
        zinv_ref[h:h + 1, :] = 1.0 / z


def peer_thresholds(scores):
    nhp, nk, M = scores.shape
    tl = _pick(M, (256, 128))
    out_spec = pl.BlockSpec((PEER_HEADS, tl), lambda i: (0, i))
    out_shape = jax.ShapeDtypeStruct((PEER_HEADS, M), F32)
    return pl.pallas_call(
        _peer_threshold_kernel, grid=(M // tl,),
        in_specs=[pl.BlockSpec((nhp, nk, tl), lambda i: (0, 0, i))],
        out_specs=[out_spec, out_spec, out_spec], out_shape=[out_shape, out_shape, out_shape],
        scratch_shapes=[pltpu.VMEM((PEER_TOPK, tl), F32), pltpu.VMEM((PEER_TOPK, tl), F32),
                        pltpu.VMEM((_NCAND, tl), F32)],
        compiler_params=_cparams(("parallel",)), name="peer_thresholds",
    )(scores)


def _peer_experts_kernel(x_ref, u_ref, vt_ref, s_ref, thr_ref, top_ref, zinv_ref, o_ref,
                         acc_ref, st_ref, a_ref, *, ek):
    j = pl.program_id(1)

    @pl.when(j == 0)
    def _():
        acc_ref[...] = jnp.zeros(acc_ref.shape, F32)

    st_ref[...] = lax.dot_general(u_ref[...], x_ref[...], (((1,), (1,)), ((), ())), preferred_element_type=F32)
    for r in range(ek // PEER_NKEYS):
        i1 = j * (ek // PEER_NKEYS) + r
        w = None
        for h in range(PEER_HEADS):
            sm = s_ref[2 * h, pl.ds(i1, 1), :] + s_ref[2 * h + 1]
            val = jnp.exp(sm - top_ref[h:h + 1, :]) * zinv_ref[h:h + 1, :]
            sel = jnp.where(sm >= thr_ref[h:h + 1, :], val, 0.0)
            w = sel if w is None else w + sel
        pre = st_ref[r * PEER_NKEYS:(r + 1) * PEER_NKEYS, :]
        act = 0.5 * pre * (1.0 + lax.erf(pre * (2.0 ** -0.5)))
        a_ref[r * PEER_NKEYS:(r + 1) * PEER_NKEYS, :] = (w * act).astype(BF16)
    acc_ref[...] += jnp.dot(vt_ref[...], a_ref[...], preferred_element_type=F32)

    @pl.when(j == pl.num_programs(1) - 1)
    def _():
        o_ref[...] = acc_ref[...].T


def peer_experts(x, u, vt, scores, thr, top, zinv):
    M, D = x.shape
    E = u.shape[0]
    tm = _pick(M, (FLAT_TILE, 256))
    ek = 512
    head_spec = pl.BlockSpec((PEER_HEADS, tm), lambda i, j: (0, i))
    return pl.pallas_call(
        functools.partial(_peer_experts_kernel, ek=ek), grid=(M // tm, E // ek),
        in_specs=[pl.BlockSpec((tm, D), lambda i, j: (i, 0)),
                  pl.BlockSpec((ek, D), lambda i, j: (j, 0)),
                  pl.BlockSpec((D, ek), lambda i, j: (0, j)),
                  pl.BlockSpec((2 * PEER_HEADS, PEER_NKEYS, tm), lambda i, j: (0, 0, i)),
                  head_spec, head_spec, head_spec],
        out_specs=pl.BlockSpec((tm, D), lambda i, j: (i, 0)),
        out_shape=jax.ShapeDtypeStruct((M, D), F32),
        scratch_shapes=[pltpu.VMEM((D, tm), F32), pltpu.VMEM((ek, tm), F32), pltpu.VMEM((ek, tm), BF16)],
        compiler_params=_cparams(("parallel", "arbitrary")), name="peer_experts",
    )(x, u, vt, scores, thr, top, zinv)


def _final_kernel(x_ref, d_ref, mg_ref, g_ref, o_ref, *, gate_row):
    x = x_ref[...] + mg_ref[gate_row:gate_row + 1, :] * d_ref[...]
    o_ref[...] = x * lax.rsqrt(jnp.mean(x * x, axis=-1, keepdims=True) + EPS) * g_ref[...]


def final_norm(x, delta, mod_gate, g, nctx, gate_row):
    B, T, D = x.shape
    tm = ROW_TILE
    L = T - nctx * tm
    row_spec = pl.BlockSpec((None, tm, D), lambda b, i: (b, i + nctx, 0))
    return pl.pallas_call(
        functools.partial(_final_kernel, gate_row=gate_row), grid=(B, L // tm),
        in_specs=[row_spec, row_spec, pl.BlockSpec((None, 8, D), lambda b, i: (b, 0, 0)),
                  pl.BlockSpec((1, D), lambda b, i: (0, 0))],
        out_specs=pl.BlockSpec((None, tm, D), lambda b, i: (b, i, 0)),
        out_shape=jax.ShapeDtypeStruct((B, L, D), F32),
        compiler_params=_cparams(("parallel", "parallel")), name="final_norm",
    )(x, delta, mod_gate, g.reshape(1, D))


def _rope_tables(L, ctx_len):
    rows = L // GRID_W
    row = jnp.repeat(jnp.arange(rows, dtype=F32), GRID_W)
    col = jnp.tile(jnp.arange(GRID_W, dtype=F32), rows)
    nf = MLA_ROPE // 4
    inv = ROPE_BASE ** (-jnp.arange(nf, dtype=F32) / nf)
    ang = jnp.concatenate([row[:, None] * inv, col[:, None] * inv], axis=-1)
    cos, sin = jnp.cos(ang), jnp.sin(ang)
    cos_t = jnp.concatenate([cos, cos, cos, cos], axis=-1)
    sin_t = jnp.concatenate([-sin, sin, -sin, sin], axis=-1)
    cos_t = jnp.concatenate([jnp.ones((ctx_len, LANES), F32), cos_t], axis=0)
    sin_t = jnp.concatenate([jnp.zeros((ctx_len, LANES), F32), sin_t], axis=0)
    return cos_t, sin_t


def kernel(x, c, ctx, c_ctx, w_ada, b_ada, g_norm1, g_norm2, w_in, g_q_mla, w_uq, g_kv_mla, w_ukv, lam_q1, lam_k1, lam_q2, lam_k2, g_subln, w_dw, b_dw, g_conv_norm, b_conv_norm, w_gate, w_branch, w_out, w_peer_q, peer_keys, peer_u, peer_v, g_final):
    B, L, D = x.shape
    ctx_len = ctx.shape[1]
    depth = w_ada.shape[0]
    T = ctx_len + L
    M = B * T
    assert MLA_ROPE == DIFF_QK and ctx_len % ROW_TILE == 0 and L % ROW_TILE == 0
    nctx = ctx_len // ROW_TILE

    cos_t, sin_t = _rope_tables(L, ctx_len)
    xs = jnp.concatenate([ctx, x], axis=1)

    mods = ada_modulation(jnp.concatenate([c, c_ctx[None, :]], axis=0), w_ada, b_ada)
    mods = mods.reshape(depth, 8, 6, D)[:, :B + 1]
    mods = jnp.concatenate([mods, jnp.zeros((depth, B + 1, 2, D), F32)], axis=2)

    o_c, o_f, o_d = 0, 2 * CONV_WIDTH, 2 * CONV_WIDTH + FNET_WIDTH
    pending = None
    for l in range(depth):
        lam_init = 0.8 - 0.6 * math.exp(-0.3 * l)
        s0, s1, s2 = MLA_Q_RANK, MLA_Q_RANK + MLA_KV_RANK, IN_MLA
        wi = w_in[l]
        w_mla = jnp.concatenate([wi[:, :s2], jnp.zeros((D, LANES - MLA_ROPE), F32)], axis=1).astype(BF16)
        w_diff = wi[:, s2:s2 + IN_DIFF].astype(BF16)
        w_fnet = wi[:, s2 + IN_DIFF:s2 + IN_DIFF + FNET_WIDTH].astype(BF16)
        w_conv = wi[:, s2 + IN_DIFF + FNET_WIDTH:].astype(BF16)
        hd = MLA_NOPE + MLA_ROPE
        w_uq_p = jnp.pad(w_uq[l].reshape(MLA_Q_RANK, MLA_HEADS, hd),
                         ((0, 0), (0, 0), (0, 256 - hd))).reshape(MLA_Q_RANK, MLA_HEADS * 256).astype(BF16)
        wkv = w_ukv[l].reshape(MLA_KV_RANK, MLA_HEADS, MLA_NOPE + MLA_V)
        w_ukv_p = jnp.concatenate([wkv[:, :, :MLA_NOPE].reshape(MLA_KV_RANK, -1),
                                   wkv[:, :, MLA_NOPE:].reshape(MLA_KV_RANK, -1)], axis=1).astype(BF16)
        lam_vecs = jnp.stack([lam_q1[l], lam_k1[l], lam_q2[l], lam_k2[l]], axis=0)

        if pending is None:
            h = norm_modulate(xs, g_norm1[l], mods[l], 0, 1, nctx)
        else:
            xs, h = norm_modulate(xs, g_norm1[l], mods[l], 0, 1, nctx,
                                  delta=pending[0], mod_gate=pending[1], gate_row=5)
        hf = h.reshape(M, D)
        z_mla = matmul(hf, w_mla, BF16, "in_proj_mla").reshape(B, T, -1)
        z_diff = matmul(hf, w_diff, BF16, "in_proj_diff").reshape(B, T, -1)
        z_fnet = matmul(hf, w_fnet, BF16, "in_proj_fnet").reshape(B, T, -1)
        z_conv = matmul(hf, w_conv, BF16, "in_proj_conv").reshape(B, T, -1)

        qm, km, vm, qd, kd = attention_projections(z_mla, z_diff, cos_t, sin_t, g_q_mla[l], g_kv_mla[l],
                                                   w_uq_p, w_ukv_p)
        o_mla = mla_attention(qm, km, vm, ctx_len)
        o_diff = diff_attention(qd, kd, z_diff, lam_vecs, g_subln[l], lam_init, ctx_len)
        o_fnet = jnp.concatenate([fourier_mix_short(z_fnet[:, :ctx_len]), fourier_mix_long(z_fnet[:, ctx_len:])],
                                 axis=1)
        o_conv = conformer_conv(z_conv, w_dw[l], b_dw[l], g_conv_norm[l], b_conv_norm[l], nctx)

        y = gated_merge(hf, [o.reshape(M, BRANCH_W) for o in (o_mla, o_fnet, o_conv, o_diff)],
                        w_gate[l].astype(BF16), w_branch[l].astype(BF16))
        mix = matmul(y, w_out[l].astype(BF16), F32, "out_proj").reshape(B, T, D)
        xs, h2 = norm_modulate(xs, g_norm2[l], mods[l], 3, 4, nctx, delta=mix, mod_gate=mods[l], gate_row=2)

        h2f = h2.reshape(M, D)
        q = matmul(h2f, w_peer_q[l].astype(BF16), BF16, "peer_query")
        scores = peer_scores(q, peer_keys[l].astype(BF16))
        thr, top, zinv = peer_thresholds(scores)
        peer_out = peer_experts(h2f, peer_u[l].astype(BF16), peer_v[l].T.astype(BF16), scores, thr, top, zinv)
        pending = (peer_out.reshape(B, T, D), mods[l])

    return final_norm(xs, pending[0], pending[1], g_final, nctx, 5)
```

```python
import functools
import math

import numpy as np
import jax
import jax.numpy as jnp
from jax import lax
from jax.experimental import pallas as pl
from jax.experimental.pallas import tpu as pltpu

F32 = jnp.float32
BF16 = jnp.bfloat16

GRID_W = 64
ROPE_BASE = 10000.0
EPS = 1e-6
MLA_HEADS = 4
MLA_Q_RANK = 384
MLA_KV_RANK = 256
MLA_NOPE = 128
MLA_ROPE = 64
MLA_V = 128
DIFF_HEADS = 4
DIFF_QK = 64
DIFF_V = 128
FNET_GROUPS = 4
FNET_GROUP_DIM = 128
FNET_WIDTH = 512
CONV_WIDTH = 512
CONV_KERNEL = 31
CONV_GROUPS = 4
N_BRANCH = 4
BRANCH_W = 512
IN_MLA = MLA_Q_RANK + MLA_KV_RANK + MLA_ROPE
IN_DIFF = DIFF_HEADS * (4 * DIFF_QK + DIFF_V)
PEER_HEADS = 8
PEER_NKEYS = 128
PEER_EXPERTS = PEER_NKEYS * PEER_NKEYS
PEER_DKEY = 256
PEER_TOPK = 16

LANES = 128
ROW_TILE = 256
FLAT_TILE = 512
VMEM_LIMIT = 56 * 1024 * 1024
NEG_INF = float("-inf")
LOG2E = math.log2(math.e)


def _cparams(sem):
    return pltpu.CompilerParams(dimension_semantics=sem, vmem_limit_bytes=VMEM_LIMIT)


def _pick(n, cands):
    for c in cands:
        if n % c == 0:
            return c
    raise ValueError(f"no tile for {n} in {cands}")


def _ada_kernel(cb_ref, w_ref, b_ref, o_ref, *, rows, tn):
    outs = []
    for r in range(rows):
        a = cb_ref[r]
        a = a * jax.nn.sigmoid(a)
        cols = [jnp.sum(w_ref[:, j * LANES:(j + 1) * LANES] * a, axis=0, keepdims=True)
                for j in range(tn // LANES)]
        outs.append(jnp.concatenate(cols, axis=1) + b_ref[...])
    outs.append(jnp.zeros((8 - rows, tn), F32))
    o_ref[...] = jnp.concatenate(outs, axis=0)


def ada_modulation(cond, w_ada, b_ada):
    rows, d = cond.shape
    depth, _, n = w_ada.shape
    tn = 512
    cb = jnp.broadcast_to(cond[:, :, None], (rows, d, LANES))
    return pl.pallas_call(
        functools.partial(_ada_kernel, rows=rows, tn=tn),
        grid=(depth, n // tn),
        in_specs=[pl.BlockSpec((rows, d, LANES), lambda l, j: (0, 0, 0)),
                  pl.BlockSpec((None, d, tn), lambda l, j: (l, 0, j)),
                  pl.BlockSpec((None, 1, tn), lambda l, j: (l, 0, j))],
        out_specs=pl.BlockSpec((None, 8, tn), lambda l, j: (l, 0, j)),
        out_shape=jax.ShapeDtypeStruct((depth, 8, n), F32),
        compiler_params=_cparams(("arbitrary", "arbitrary")),
        name="ada_modulation",
    )(cb, w_ada, b_ada.reshape(depth, 1, n))


def _norm_mod_kernel(*refs, has_delta, gate_row, shift_row, scale_row):
    if has_delta:
        x_ref, d_ref, mg_ref, ms_ref, g_ref, xo_ref, h_ref = refs
        x = x_ref[...] + mg_ref[gate_row:gate_row + 1, :] * d_ref[...].astype(F32)
        xo_ref[...] = x
    else:
        x_ref, ms_ref, g_ref, h_ref = refs
        x = x_ref[...]
    y = x * lax.rsqrt(jnp.mean(x * x, axis=-1, keepdims=True) + EPS) * g_ref[...]
    h = y * (1.0 + ms_ref[scale_row:scale_row + 1, :]) + ms_ref[shift_row:shift_row + 1, :]
    h_ref[...] = h.astype(BF16)


def norm_modulate(x, g, mod_ss, shift_row, scale_row, nctx, delta=None, mod_gate=None, gate_row=None):
    B, T, D = x.shape
    tm = ROW_TILE
    row_spec = pl.BlockSpec((None, tm, D), lambda b, i: (b, i, 0))
    mod_spec = pl.BlockSpec((None, 8, D), lambda b, i: (jnp.where(i < nctx, B, b), 0, 0))
    g_spec = pl.BlockSpec((1, D), lambda b, i: (0, 0))
    h_shape = jax.ShapeDtypeStruct((B, T, D), BF16)
    kern = functools.partial(_norm_mod_kernel, has_delta=delta is not None, gate_row=gate_row,
                             shift_row=shift_row, scale_row=scale_row)
    if delta is None:
        return pl.pallas_call(
            kern, grid=(B, T // tm), in_specs=[row_spec, mod_spec, g_spec], out_specs=row_spec,
            out_shape=h_shape, compiler_params=_cparams(("parallel", "parallel")), name="norm_modulate",
        )(x, mod_ss, g.reshape(1, D))
    return pl.pallas_call(
        kern, grid=(B, T // tm), in_specs=[row_spec, row_spec, mod_spec, mod_spec, g_spec],
        out_specs=[row_spec, row_spec],
        out_shape=[jax.ShapeDtypeStruct((B, T, D), F32), h_shape],
        compiler_params=_cparams(("parallel", "parallel")), name="residual_norm_modulate",
    )(x, delta, mod_gate, mod_ss, g.reshape(1, D))


def _mm_kernel(x_ref, w_ref, o_ref):
    o_ref[...] = jnp.dot(x_ref[...], w_ref[...], preferred_element_type=F32).astype(o_ref.dtype)


def matmul(x, w, out_dtype, name):
    M, K = x.shape
    N = w.shape[1]
    tm = _pick(M, (FLAT_TILE, 256, 128))
    tn = _pick(N, (1024, 768, 512, 256, 128))
    return pl.pallas_call(
        _mm_kernel, grid=(N // tn, M // tm),
        in_specs=[pl.BlockSpec((tm, K), lambda j, i: (i, 0)),
                  pl.BlockSpec((K, tn), lambda j, i: (0, j))],
        out_specs=pl.BlockSpec((tm, tn), lambda j, i: (i, j)),
        out_shape=jax.ShapeDtypeStruct((M, N), out_dtype),
        compiler_params=_cparams(("parallel", "parallel")), name=name,
    )(x, w)


def _rope(v, cos, sin, lane):
    r = jnp.where((lane % 64) < 32, pltpu.roll(v, 96, 1), pltpu.roll(v, 32, 1))
    return v * cos + r * sin


def _proj_kernel(zm_ref, zd_ref, cos_ref, sin_ref, gq_ref, gkv_ref, wuq_ref, wukv_ref,
                 qm_ref, km_ref, vm_ref, qd_ref, kd_ref, vd_ref, *, mla_scale, diff_scale):
    tm = zm_ref.shape[0]
    cos = cos_ref[...]
    sin = sin_ref[...]
    lane = lax.broadcasted_iota(jnp.int32, (tm, LANES), 1)
    ones_col = jnp.where(lane == 0, 1.0, 0.0).astype(BF16)

    def rms(v, g):
        return v * lax.rsqrt(jnp.mean(v * v, axis=-1, keepdims=True) + EPS) * g

    zm = zm_ref[...].astype(F32)
    qn = rms(zm[:, :MLA_Q_RANK], gq_ref[...]).astype(BF16)
    kvn = rms(zm[:, MLA_Q_RANK:MLA_Q_RANK + MLA_KV_RANK], gkv_ref[...]).astype(BF16)
    k_rope = _rope(zm[:, MLA_Q_RANK + MLA_KV_RANK:], cos, sin, lane).astype(BF16)
    q = jnp.dot(qn, wuq_ref[...], preferred_element_type=F32)
    kv = jnp.dot(kvn, wukv_ref[...], preferred_element_type=F32)
    for h in range(MLA_HEADS):
        qm_ref[h, :, 0:LANES] = (q[:, 256 * h:256 * h + LANES] * mla_scale).astype(BF16)
        qm_ref[h, :, LANES:2 * LANES] = (
            _rope(q[:, 256 * h + LANES:256 * h + 2 * LANES], cos, sin, lane) * mla_scale).astype(BF16)
        km_ref[h, :, 0:LANES] = kv[:, LANES * h:LANES * (h + 1)].astype(BF16)
        km_ref[h, :, LANES:2 * LANES] = k_rope
        vm_ref[h, :, 0:LANES] = kv[:, 512 + LANES * h:512 + LANES * (h + 1)].astype(BF16)
        vm_ref[h, :, LANES:2 * LANES] = ones_col
    zd = zd_ref[...].astype(F32)
    for h in range(DIFF_HEADS):
        qd_ref[h] = (_rope(zd[:, LANES * h:LANES * (h + 1)], cos, sin, lane) * diff_scale).astype(BF16)
        kd_ref[h] = _rope(zd[:, 512 + LANES * h:512 + LANES * (h + 1)], cos, sin, lane).astype(BF16)
        vd_ref[h, :, 0:LANES] = zd_ref[:, 1024 + LANES * h:1024 + LANES * (h + 1)]
        vd_ref[h, :, LANES:2 * LANES] = ones_col


def attention_projections(z_mla, z_diff, cos_t, sin_t, g_q, g_kv, w_uq_p, w_ukv_p):
    B, T, _ = z_mla.shape
    tm = ROW_TILE
    H = MLA_HEADS

    def head_spec(w):
        return pl.BlockSpec((None, H, tm, w), lambda b, i: (b, 0, i, 0))

    def head_shape(w):
        return jax.ShapeDtypeStruct((B, H, T, w), BF16)

    return pl.pallas_call(
        functools.partial(_proj_kernel, mla_scale=float((MLA_NOPE + MLA_ROPE) ** -0.5 * LOG2E),
                          diff_scale=float(DIFF_QK ** -0.5 * LOG2E)),
        grid=(B, T // tm),
        in_specs=[pl.BlockSpec((None, tm, z_mla.shape[2]), lambda b, i: (b, i, 0)),
                  pl.BlockSpec((None, tm, z_diff.shape[2]), lambda b, i: (b, i, 0)),
                  pl.BlockSpec((tm, LANES), lambda b, i: (i, 0)),
                  pl.BlockSpec((tm, LANES), lambda b, i: (i, 0)),
                  pl.BlockSpec((1, MLA_Q_RANK), lambda b, i: (0, 0)),
                  pl.BlockSpec((1, MLA_KV_RANK), lambda b, i: (0, 0)),
                  pl.BlockSpec(w_uq_p.shape, lambda b, i: (0, 0)),
                  pl.BlockSpec(w_ukv_p.shape, lambda b, i: (0, 0))],
        out_specs=[head_spec(256), head_spec(256), head_spec(256), head_spec(128), head_spec(128), head_spec(256)],
        out_shape=[head_shape(256), head_shape(256), head_shape(256), head_shape(128), head_shape(128),
                   head_shape(256)],
        compiler_params=_cparams(("parallel", "parallel")), name="attention_projections",
    )(z_mla, z_diff, cos_t, sin_t, g_q.reshape(1, -1), g_kv.reshape(1, -1), w_uq_p, w_ukv_p)


V_EXT = 2 * LANES


def _att_update(q, k, v_ext, m_ref, acc_ref, idx, mask):
    s = lax.dot_general(q, k, (((1,), (1,)), ((), ())), preferred_element_type=F32)
    if mask is not None:
        s = jnp.where(mask, NEG_INF, s)
    m_prev = m_ref[idx]
    m_new = jnp.maximum(m_prev, jnp.max(s, axis=-1, keepdims=True))
    alpha = jnp.exp2(m_prev - m_new)
    p = jnp.exp2(s - m_new).astype(BF16)
    acc_ref[idx] = alpha * acc_ref[idx] + jnp.dot(p, v_ext, preferred_element_type=F32)
    m_ref[idx] = m_new


def _att_init(m_ref, acc_ref):
    m_ref[...] = jnp.full(m_ref.shape, NEG_INF, F32)
    acc_ref[...] = jnp.zeros(acc_ref.shape, F32)


def _att_sweep(step, qi, ki, tq, tk, ctx_len):
    @pl.when(qi == 0)
    def _():
        row = lax.broadcasted_iota(jnp.int32, (tq, tk), 0)
        col = ki * tk + lax.broadcasted_iota(jnp.int32, (tq, tk), 1)
        step((row < ctx_len) & (col >= ctx_len))

    @pl.when(qi != 0)
    def _():
        step(None)


def _att_result(acc):
    return acc[:, :LANES] / acc[:, LANES:LANES + 1]


def _mla_att_kernel(q_ref, k_ref, v_ref, o_ref, m_ref, acc_ref, *, tq, tk, ctx_len):
    qi = pl.program_id(1)
    ki = pl.program_id(2)

    @pl.when(ki == 0)
    def _():
        _att_init(m_ref, acc_ref)

    def step(mask):
        for h in range(MLA_HEADS):
            _att_update(q_ref[h], k_ref[h], v_ref[h], m_ref, acc_ref, h, mask)

    _att_sweep(step, qi, ki, tq, tk, ctx_len)

    @pl.when(ki == pl.num_programs(2) - 1)
    def _():
        for h in range(MLA_HEADS):
            o_ref[:, h * MLA_V:(h + 1) * MLA_V] = _att_result(acc_ref[h]).astype(o_ref.dtype)


def _att_specs(H, tq, tk, dk):
    return [pl.BlockSpec((None, H, tq, dk), lambda b, i, j: (b, 0, i, 0)),
            pl.BlockSpec((None, H, tk, dk), lambda b, i, j: (b, 0, j, 0)),
            pl.BlockSpec((None, H, tk, V_EXT), lambda b, i, j: (b, 0, j, 0))]


def mla_attention(qm, km, vm, ctx_len):
    B, H, T, dk = qm.shape
    tq = tk = _pick(T, (768, 256))
    assert ctx_len <= tq
    return pl.pallas_call(
        functools.partial(_mla_att_kernel, tq=tq, tk=tk, ctx_len=ctx_len),
        grid=(B, T // tq, T // tk),
        in_specs=_att_specs(H, tq, tk, dk),
        out_specs=pl.BlockSpec((None, tq, H * MLA_V), lambda b, i, j: (b, i, 0)),
        out_shape=jax.ShapeDtypeStruct((B, T, H * MLA_V), BF16),
        scratch_shapes=[pltpu.VMEM((H, tq, 1), F32), pltpu.VMEM((H, tq, V_EXT), F32)],
        compiler_params=_cparams(("parallel", "parallel", "arbitrary")), name="mla_attention",
    )(qm, km, vm)


def _diff_att_kernel(q_ref, k_ref, v_ref, lam_ref, g_ref, o_ref, m_ref, acc_ref, *, tq, tk, ctx_len, lam_init):
    qi = pl.program_id(1)
    ki = pl.program_id(2)

    @pl.when(ki == 0)
    def _():
        _att_init(m_ref, acc_ref)

    def step(mask):
        first = lax.broadcasted_iota(jnp.int32, (tq, 2 * DIFF_QK), 1) < DIFF_QK
        for h in range(DIFF_HEADS):
            q = q_ref[h]
            zero = jnp.zeros_like(q)
            _att_update(jnp.where(first, q, zero), k_ref[h], v_ref[h], m_ref, acc_ref, 2 * h, mask)
            _att_update(jnp.where(first, zero, q), k_ref[h], v_ref[h], m_ref, acc_ref, 2 * h + 1, mask)

    _att_sweep(step, qi, ki, tq, tk, ctx_len)

    @pl.when(ki == pl.num_programs(2) - 1)
    def _():
        lv = lam_ref[...]
        lam = (jnp.exp(jnp.sum(lv[0:1] * lv[1:2], axis=-1, keepdims=True))
               - jnp.exp(jnp.sum(lv[2:3] * lv[3:4], axis=-1, keepdims=True)) + lam_init)
        for h in range(DIFF_HEADS):
            o = _att_result(acc_ref[2 * h]) - lam * _att_result(acc_ref[2 * h + 1])
            o = o * lax.rsqrt(jnp.mean(o * o, axis=-1, keepdims=True) + EPS) * g_ref[...]
            o_ref[:, h * DIFF_V:(h + 1) * DIFF_V] = (o * (1.0 - lam_init)).astype(o_ref.dtype)


def diff_attention(qd, kd, vd, lam_vecs, g_subln, lam_init, ctx_len):
    B, H, T, dk = qd.shape
    tq = tk = _pick(T, (768, 256))
    assert ctx_len <= tq
    return pl.pallas_call(
        functools.partial(_diff_att_kernel, tq=tq, tk=tk, ctx_len=ctx_len, lam_init=lam_init),
        grid=(B, T // tq, T // tk),
        in_specs=_att_specs(H, tq, tk, dk) + [pl.BlockSpec(lam_vecs.shape, lambda b, i, j: (0, 0)),
                                               pl.BlockSpec((1, DIFF_V), lambda b, i, j: (0, 0))],
        out_specs=pl.BlockSpec((None, tq, H * DIFF_V), lambda b, i, j: (b, i, 0)),
        out_shape=jax.ShapeDtypeStruct((B, T, H * DIFF_V), BF16),
        scratch_shapes=[pltpu.VMEM((2 * H, tq, 1), F32), pltpu.VMEM((2 * H, tq, V_EXT), F32)],
        compiler_params=_cparams(("parallel", "parallel", "arbitrary")), name="diff_attention",
    )(qd, kd, vd, lam_vecs, g_subln.reshape(1, -1))


def _dft_mats(n):
    k = np.arange(n)
    ang = 2.0 * np.pi * ((k[:, None] * k[None, :]) % n) / n
    return jnp.asarray(np.cos(ang), BF16), jnp.asarray(np.sin(ang), BF16)


def _channel_dft(x, cc, sc):
    tr, ti = [], []
    for s in range(x.shape[1] // LANES):
        xs = x[:, s * LANES:(s + 1) * LANES]
        tr.append(jnp.dot(xs, cc, preferred_element_type=F32))
        ti.append(-jnp.dot(xs, sc, preferred_element_type=F32))
    return jnp.concatenate(tr, axis=1), jnp.concatenate(ti, axis=1)


def _fnet_stage1_kernel(x_ref, cc_ref, sc_ref, c_ref, s_ref, twc_ref, tws_ref, yr_ref, yi_ref, *, nb):
    tr, ti = _channel_dft(x_ref[...], cc_ref[...], sc_ref[...])
    trb, tib = tr.astype(BF16), ti.astype(BF16)
    c, s = c_ref[...], s_ref[...]
    yr = jnp.dot(c, trb, preferred_element_type=F32) + jnp.dot(s, tib, preferred_element_type=F32)
    yi = jnp.dot(c, tib, preferred_element_type=F32) - jnp.dot(s, trb, preferred_element_type=F32)
    for i in range(nb):
        twc = jnp.concatenate([twc_ref[i]] * FNET_GROUPS, axis=1)
        tws = jnp.concatenate([tws_ref[i]] * FNET_GROUPS, axis=1)
        a = yr[:, i * FNET_WIDTH:(i + 1) * FNET_WIDTH]
        b = yi[:, i * FNET_WIDTH:(i + 1) * FNET_WIDTH]
        yr_ref[i] = a * twc + b * tws
        yi_ref[i] = b * twc - a * tws


def _fnet_stage2_kernel(yr_ref, yi_ref, c_ref, s_ref, o_ref, *, scale):
    xr = (jnp.dot(c_ref[...], yr_ref[...].astype(BF16), preferred_element_type=F32)
          + jnp.dot(s_ref[...], yi_ref[...].astype(BF16), preferred_element_type=F32))
    o_ref[...] = (xr * scale).astype(o_ref.dtype)


def _fnet_direct_kernel(x_ref, cc_ref, sc_ref, c_ref, s_ref, o_ref, *, scale):
    tr, ti = _channel_dft(x_ref[...], cc_ref[...], sc_ref[...])
    xr = (jnp.dot(c_ref[...], tr.astype(BF16), preferred_element_type=F32)
          + jnp.dot(s_ref[...], ti.astype(BF16), preferred_element_type=F32))
    o_ref[...] = (xr * scale).astype(o_ref.dtype)


def _full(a):
    return pl.BlockSpec(a.shape, lambda *_: (0,) * a.ndim)


def fourier_mix_long(zf):
    B, L, W = zf.shape
    n2 = LANES
    n1 = L // n2
    nb = _pick(n1, (8,))
    cc, sc = _dft_mats(FNET_GROUP_DIM)
    c2, s2 = _dft_mats(n2)
    c1, s1 = _dft_mats(n1)
    ang = 2.0 * np.pi * (np.arange(n1)[:, None] * np.arange(n2)[None, :]) / L
    twc = jnp.asarray(np.broadcast_to(np.cos(ang)[:, :, None], (n1, n2, LANES)), F32)
    tws = jnp.asarray(np.broadcast_to(np.sin(ang)[:, :, None], (n1, n2, LANES)), F32)
    x = zf.reshape(B, n2, n1 * W)
    y_shape = jax.ShapeDtypeStruct((B, n1, n2, W), F32)
    y_spec = pl.BlockSpec((None, nb, n2, W), lambda b, j: (b, j, 0, 0))
    tw_spec = pl.BlockSpec((nb, n2, LANES), lambda b, j: (j, 0, 0))
    yr, yi = pl.pallas_call(
        functools.partial(_fnet_stage1_kernel, nb=nb), grid=(B, n1 // nb),
        in_specs=[pl.BlockSpec((None, n2, nb * W), lambda b, j: (b, 0, j)),
                  _full(cc), _full(sc), _full(c2), _full(s2), tw_spec, tw_spec],
        out_specs=[y_spec, y_spec], out_shape=[y_shape, y_shape],
        compiler_params=_cparams(("parallel", "parallel")), name="fnet_stage1",
    )(x, cc, sc, c2, s2, twc, tws)
    cols = n2 * W
    tn = 4096
    y2_spec = pl.BlockSpec((None, n1, tn), lambda b, j: (b, 0, j))
    out = pl.pallas_call(
        functools.partial(_fnet_stage2_kernel, scale=float((L * FNET_GROUP_DIM) ** -0.5)),
        grid=(B, cols // tn),
        in_specs=[y2_spec, y2_spec, _full(c1), _full(s1)],
        out_specs=y2_spec, out_shape=jax.ShapeDtypeStruct((B, n1, cols), BF16),
        compiler_params=_cparams(("parallel", "parallel")), name="fnet_stage2",
    )(yr.reshape(B, n1, cols), yi.reshape(B, n1, cols), c1, s1)
    return out.reshape(B, L, W)


def fourier_mix_short(zf):
    B, L, W = zf.shape
    cc, sc = _dft_mats(FNET_GROUP_DIM)
    c, s = _dft_mats(L)
    spec = pl.BlockSpec((None, L, W), lambda b: (b, 0, 0))
    return pl.pallas_call(
        functools.partial(_fnet_direct_kernel, scale=float((L * FNET_GROUP_DIM) ** -0.5)), grid=(B,),
        in_specs=[spec, _full(cc), _full(sc), _full(c), _full(s)],
        out_specs=spec, out_shape=jax.ShapeDtypeStruct((B, L, W), BF16),
        compiler_params=_cparams(("parallel",)), name="fnet_direct",
    )(zf, cc, sc, c, s)


HALO = 16


def _conv_kernel(prev_ref, cur_ref, next_ref, w_ref, bdw_ref, g_ref, b_ref, o_ref, ext_ref, *, nctx):
    i = pl.program_id(1)
    nt = pl.num_programs(1)
    tm = cur_ref.shape[0]

    def glu(z):
        z = z.astype(F32)
        return z[:, :CONV_WIDTH] * jax.nn.sigmoid(z[:, CONV_WIDTH:])

    has_prev = jnp.logical_and(i != 0, i != nctx)
    has_next = jnp.logical_and(i != nctx - 1, i != nt - 1)
    ext_ref[0:HALO, :] = jnp.where(has_prev, glu(prev_ref[tm - HALO:tm, :]), 0.0)
    ext_ref[HALO:HALO + tm, :] = glu(cur_ref[...])
    ext_ref[HALO + tm:2 * HALO + tm, :] = jnp.where(has_next, glu(next_ref[0:HALO, :]), 0.0)
    pad = CONV_KERNEL // 2
    acc = jnp.zeros((tm, CONV_WIDTH), F32)
    for k in range(CONV_KERNEL):
        acc = acc + w_ref[k:k + 1, :] * ext_ref[pl.ds(HALO - pad + k, tm), :]
    y = acc + bdw_ref[...]
    gw = CONV_WIDTH // CONV_GROUPS
    outs = []
    for gi in range(CONV_GROUPS):
        yg = y[:, gi * gw:(gi + 1) * gw]
        mu = jnp.mean(yg, axis=-1, keepdims=True)
        var = jnp.mean(jnp.square(yg - mu), axis=-1, keepdims=True)
        outs.append((yg - mu) * lax.rsqrt(var + 1e-5))
    yn = jnp.concatenate(outs, axis=1) * g_ref[...] + b_ref[...]
    o_ref[...] = (yn * jax.nn.sigmoid(yn)).astype(o_ref.dtype)


def conformer_conv(zc, w_dw, b_dw, g, b, nctx):
    B, T, W2 = zc.shape
    tm = ROW_TILE
    nt = T // tm
    w_pad = jnp.concatenate([w_dw, jnp.zeros((32 - CONV_KERNEL, CONV_WIDTH), F32)], axis=0)
    vec = pl.BlockSpec((1, CONV_WIDTH), lambda bb, i: (0, 0))
    return pl.pallas_call(
        functools.partial(_conv_kernel, nctx=nctx), grid=(B, nt),
        in_specs=[pl.BlockSpec((None, tm, W2), lambda bb, i: (bb, jnp.maximum(i - 1, 0), 0)),
                  pl.BlockSpec((None, tm, W2), lambda bb, i: (bb, i, 0)),
                  pl.BlockSpec((None, tm, W2), lambda bb, i: (bb, jnp.minimum(i + 1, nt - 1), 0)),
                  pl.BlockSpec((32, CONV_WIDTH), lambda bb, i: (0, 0)), vec, vec, vec],
        out_specs=pl.BlockSpec((None, tm, CONV_WIDTH), lambda bb, i: (bb, i, 0)),
        out_shape=jax.ShapeDtypeStruct((B, T, CONV_WIDTH), BF16),
        scratch_shapes=[pltpu.VMEM((tm + 2 * HALO, CONV_WIDTH), F32)],
        compiler_params=_cparams(("parallel", "parallel")), name="conformer_conv",
    )(zc, zc, zc, w_pad, b_dw.reshape(1, -1), g.reshape(1, -1), b.reshape(1, -1))


def _merge_kernel(h_ref, b0_ref, b1_ref, b2_ref, b3_ref, wg_ref, wb_ref, o_ref):
    h = h_ref[...]
    acc = None
    for n, br in enumerate((b0_ref, b1_ref, b2_ref, b3_ref)):
        gate = jax.nn.sigmoid(jnp.dot(h, wg_ref[n], preferred_element_type=F32))
        term = gate * jnp.dot(br[...], wb_ref[n], preferred_element_type=F32)
        acc = term if acc is None else acc + term
    o_ref[...] = acc.astype(o_ref.dtype)


def gated_merge(h, branches, w_gate, w_branch):
    M, D = h.shape
    N = w_gate.shape[2]
    tm = _pick(M, (FLAT_TILE, 256))
    tn = _pick(N, (512, 256))
    br_spec = pl.BlockSpec((tm, BRANCH_W), lambda j, i: (i, 0))
    return pl.pallas_call(
        _merge_kernel, grid=(N // tn, M // tm),
        in_specs=[pl.BlockSpec((tm, D), lambda j, i: (i, 0)), br_spec, br_spec, br_spec, br_spec,
                  pl.BlockSpec((N_BRANCH, D, tn), lambda j, i: (0, 0, j)),
                  pl.BlockSpec((N_BRANCH, BRANCH_W, tn), lambda j, i: (0, 0, j))],
        out_specs=pl.BlockSpec((tm, tn), lambda j, i: (i, j)),
        out_shape=jax.ShapeDtypeStruct((M, N), BF16),
        compiler_params=_cparams(("parallel", "parallel")), name="gated_merge",
    )(h, *branches, w_gate, w_branch)


def _peer_scores_kernel(q_ref, keys_ref, o_ref):
    for hp in range(2 * PEER_HEADS):
        o_ref[hp] = lax.dot_general(keys_ref[hp % 2], q_ref[:, hp * LANES:(hp + 1) * LANES],
                                    (((1,), (1,)), ((), ())), preferred_element_type=F32)


def peer_scores(q, keys):
    M = q.shape[0]
    tm = _pick(M, (FLAT_TILE, 256))
    nhp = 2 * PEER_HEADS
    return pl.pallas_call(
        _peer_scores_kernel, grid=(M // tm,),
        in_specs=[pl.BlockSpec((tm, q.shape[1]), lambda i: (i, 0)), _full(keys)],
        out_specs=pl.BlockSpec((nhp, PEER_NKEYS, tm), lambda i: (0, 0, i)),
        out_shape=jax.ShapeDtypeStruct((nhp, PEER_NKEYS, M), F32),
        compiler_params=_cparams(("parallel",)), name="peer_scores",
    )(q, keys)


_NRANK = PEER_TOPK + 1
_STAIRCASE = [(a, b) for a in range(_NRANK) for b in range(_NRANK) if (a + 1) * (b + 1) <= _NRANK]
_NCAND = -(-len(_STAIRCASE) // 8) * 8
_NRANK_PAD = -(-_NRANK // 8) * 8


def _peer_threshold_kernel(s_ref, s1f_ref, s2f_ref, thr_ref, t1_ref, t2_ref, c_ref):
    def extract(vals, dst_ref):
        v = vals
        m = prev = None
        for r in range(_NRANK):
            prev = m
            m = jnp.max(v, axis=0, keepdims=True)
            if dst_ref is not None:
                dst_ref[r:r + 1, :] = m
            v = jnp.where(v == m, NEG_INF, v)
        return prev, m

    for h in range(PEER_HEADS):
        u1 = s_ref[2 * h] * LOG2E
        u2 = s_ref[2 * h + 1] * LOG2E
        extract(u1, t1_ref)
        extract(u2, t2_ref)
        m1 = t1_ref[0:1, :]
        m2 = t2_ref[0:1, :]
        c_ref[...] = jnp.full(c_ref.shape, NEG_INF, F32)
        for r, (a, b) in enumerate(_STAIRCASE):
            c_ref[r:r + 1, :] = (t1_ref[a:a + 1, :] - m1) + (t2_ref[b:b + 1, :] - m2)
        cand = c_ref[...]
        c16, c17 = extract(cand, None)
        thr = 0.5 * (c16 + c17)
        z = jnp.sum(jnp.where(cand >= thr, jnp.exp2(cand), 0.0), axis=0, keepdims=True)
        lz = -jnp.log2(z)
        thr_ref[h:h + 1, :] = thr + lz
        s1f_ref[h] = (u1 - m1) + lz
        s2f_ref[h] = u2 - m2


def peer_thresholds(scores):
    nhp, nk, M = scores.shape
    tl = _pick(M, (256, 128))
    sf_spec = pl.BlockSpec((PEER_HEADS, nk, tl), lambda i: (0, 0, i))
    sf_shape = jax.ShapeDtypeStruct((PEER_HEADS, nk, M), F32)
    return pl.pallas_call(
        _peer_threshold_kernel, grid=(M // tl,),
        in_specs=[pl.BlockSpec((nhp, nk, tl), lambda i: (0, 0, i))],
        out_specs=[sf_spec, sf_spec, pl.BlockSpec((PEER_HEADS, tl), lambda i: (0, i))],
        out_shape=[sf_shape, sf_shape, jax.ShapeDtypeStruct((PEER_HEADS, M), F32)],
        scratch_shapes=[pltpu.VMEM((_NRANK_PAD, tl), F32), pltpu.VMEM((_NRANK_PAD, tl), F32),
                        pltpu.VMEM((_NCAND, tl), F32)],
        compiler_params=_cparams(("parallel",)), name="peer_thresholds",
    )(scores)


PEER_TOKEN_SPLIT = 256
PEER_EXPERT_SPLIT = 512


def _peer_experts_kernel(x_ref, u_ref, vt_ref, s1_ref, s2_ref, thr_ref, o_ref,
                         acc_ref, e1_ref, e2_ref, thr2_ref, *scratch, ek):
    j = pl.program_id(1)
    tm = x_ref.shape[0]
    th, es = PEER_TOKEN_SPLIT, PEER_EXPERT_SPLIT
    pieces = [(c, t) for c in range(ek // es) for t in range(tm // th)]
    st_refs, a_refs = scratch[:len(pieces)], scratch[len(pieces):]

    @pl.when(j == 0)
    def _():
        acc_ref[...] = jnp.zeros(acc_ref.shape, F32)
        for h in range(PEER_HEADS):
            s1 = s1_ref[h]
            e1_ref[h] = jnp.exp2(s1)
            e2_ref[h] = jnp.exp2(s2_ref[h])
            thr2_ref[h] = thr_ref[h:h + 1, :] - s1

    def pre_activations(p):
        c, t = pieces[p]
        st_refs[p][...] = lax.dot_general(u_ref[c * es:(c + 1) * es, :], x_ref[t * th:(t + 1) * th, :],
                                          (((1,), (1,)), ((), ())), preferred_element_type=F32)

    def weights(p):
        c, t = pieces[p]
        for r in range(es // PEER_NKEYS):
            i1 = j * (ek // PEER_NKEYS) + c * (es // PEER_NKEYS) + r
            rows = slice(r * PEER_NKEYS, (r + 1) * PEER_NKEYS)
            half = slice(t * th, (t + 1) * th)
            thr_rows = [thr2_ref[h, pl.ds(i1, 1), half] for h in range(PEER_HEADS)]
            e1_rows = [e1_ref[h, pl.ds(i1, 1), half] for h in range(PEER_HEADS)]
            for g in range(th // LANES):
                lanes = slice(t * th + g * LANES, t * th + (g + 1) * LANES)
                sub = slice(g * LANES, (g + 1) * LANES)
                w = None
                for h in range(PEER_HEADS):
                    sel = jnp.where(s2_ref[h, :, lanes] >= thr_rows[h][:, sub], e2_ref[h, :, lanes], 0.0)
                    sel = sel * e1_rows[h][:, sub]
                    w = sel if w is None else w + sel
                pre = st_refs[p][rows, g * LANES:(g + 1) * LANES]
                act = pre * (0.5 + 0.5 * lax.erf(pre * (2.0 ** -0.5)))
                a_refs[p][rows, g * LANES:(g + 1) * LANES] = (w * act).astype(BF16)

    def accumulate(p):
        c, t = pieces[p]
        lanes = slice(t * th, (t + 1) * th)
        acc_ref[:, lanes] += jnp.dot(vt_ref[:, c * es:(c + 1) * es], a_refs[p][...], preferred_element_type=F32)

    for p in range(len(pieces)):
        pre_activations(p)
    for p in range(len(pieces)):
        weights(p)
        accumulate(p)

    @pl.when(j == pl.num_programs(1) - 1)
    def _():
        o_ref[...] = acc_ref[...].T


def peer_experts(x, u, vt, s1f, s2f, thr):
    M, D = x.shape
    E = u.shape[0]
    tm = _pick(M, (FLAT_TILE, 256))
    ek = 1024
    th, es = PEER_TOKEN_SPLIT, PEER_EXPERT_SPLIT
    npieces = (tm // th) * (ek // es)
    sf_spec = pl.BlockSpec((PEER_HEADS, PEER_NKEYS, tm), lambda i, j: (0, 0, i))
    sf_scratch = pltpu.VMEM((PEER_HEADS, PEER_NKEYS, tm), F32)
    return pl.pallas_call(
        functools.partial(_peer_experts_kernel, ek=ek), grid=(M // tm, E // ek),
        in_specs=[pl.BlockSpec((tm, D), lambda i, j: (i, 0)),
                  pl.BlockSpec((ek, D), lambda i, j: (j, 0)),
                  pl.BlockSpec((D, ek), lambda i, j: (0, j)),
                  sf_spec, sf_spec, pl.BlockSpec((PEER_HEADS, tm), lambda i, j: (0, i))],
        out_specs=pl.BlockSpec((tm, D), lambda i, j: (i, 0)),
        out_shape=jax.ShapeDtypeStruct((M, D), F32),
        scratch_shapes=([pltpu.VMEM((D, tm), F32), sf_scratch, sf_scratch, sf_scratch]
                        + [pltpu.VMEM((es, th), F32)] * npieces + [pltpu.VMEM((es, th), BF16)] * npieces),
        compiler_params=_cparams(("parallel", "arbitrary")), name="peer_experts",
    )(x, u, vt, s1f, s2f, thr)


def _final_kernel(x_ref, d_ref, mg_ref, g_ref, o_ref, *, gate_row):
    x = x_ref[...] + mg_ref[gate_row:gate_row + 1, :] * d_ref[...]
    o_ref[...] = x * lax.rsqrt(jnp.mean(x * x, axis=-1, keepdims=True) + EPS) * g_ref[...]


def final_norm(x, delta, mod_gate, g, nctx, gate_row):
    B, T, D = x.shape
    tm = ROW_TILE
    L = T - nctx * tm
    row_spec = pl.BlockSpec((None, tm, D), lambda b, i: (b, i + nctx, 0))
    return pl.pallas_call(
        functools.partial(_final_kernel, gate_row=gate_row), grid=(B, L // tm),
        in_specs=[row_spec, row_spec, pl.BlockSpec((None, 8, D), lambda b, i: (b, 0, 0)),
                  pl.BlockSpec((1, D), lambda b, i: (0, 0))],
        out_specs=pl.BlockSpec((None, tm, D), lambda b, i: (b, i, 0)),
        out_shape=jax.ShapeDtypeStruct((B, L, D), F32),
        compiler_params=_cparams(("parallel", "parallel")), name="final_norm",
    )(x, delta, mod_gate, g.reshape(1, D))


def _rope_tables(L, ctx_len):
    rows = L // GRID_W
    row = jnp.repeat(jnp.arange(rows, dtype=F32), GRID_W)
    col = jnp.tile(jnp.arange(GRID_W, dtype=F32), rows)
    nf = MLA_ROPE // 4
    inv = ROPE_BASE ** (-jnp.arange(nf, dtype=F32) / nf)
    ang = jnp.concatenate([row[:, None] * inv, col[:, None] * inv], axis=-1)
    cos, sin = jnp.cos(ang), jnp.sin(ang)
    cos_t = jnp.concatenate([cos, cos, cos, cos], axis=-1)
    sin_t = jnp.concatenate([-sin, sin, -sin, sin], axis=-1)
    cos_t = jnp.concatenate([jnp.ones((ctx_len, LANES), F32), cos_t], axis=0)
    sin_t = jnp.concatenate([jnp.zeros((ctx_len, LANES), F32), sin_t], axis=0)
    return cos_t, sin_t


def kernel(x, c, ctx, c_ctx, w_ada, b_ada, g_norm1, g_norm2, w_in, g_q_mla, w_uq, g_kv_mla, w_ukv, lam_q1, lam_k1, lam_q2, lam_k2, g_subln, w_dw, b_dw, g_conv_norm, b_conv_norm, w_gate, w_branch, w_out, w_peer_q, peer_keys, peer_u, peer_v, g_final):
    B, L, D = x.shape
    ctx_len = ctx.shape[1]
    depth = w_ada.shape[0]
    T = ctx_len + L
    M = B * T
    assert MLA_ROPE == DIFF_QK and ctx_len % ROW_TILE == 0 and L % ROW_TILE == 0
    nctx = ctx_len // ROW_TILE

    cos_t, sin_t = _rope_tables(L, ctx_len)
    xs = jnp.concatenate([ctx, x], axis=1)

    mods = ada_modulation(jnp.concatenate([c, c_ctx[None, :]], axis=0), w_ada, b_ada)
    mods = mods.reshape(depth, 8, 6, D)[:, :B + 1]
    mods = jnp.concatenate([mods, jnp.zeros((depth, B + 1, 2, D), F32)], axis=2)

    o_c, o_f, o_d = 0, 2 * CONV_WIDTH, 2 * CONV_WIDTH + FNET_WIDTH
    pending = None
    for l in range(depth):
        lam_init = 0.8 - 0.6 * math.exp(-0.3 * l)
        s0, s1, s2 = MLA_Q_RANK, MLA_Q_RANK + MLA_KV_RANK, IN_MLA
        wi = w_in[l]
        w_mla = jnp.concatenate([wi[:, :s2], jnp.zeros((D, LANES - MLA_ROPE), F32)], axis=1).astype(BF16)
        w_diff = wi[:, s2:s2 + IN_DIFF].astype(BF16)
        w_fnet = wi[:, s2 + IN_DIFF:s2 + IN_DIFF + FNET_WIDTH].astype(BF16)
        w_conv = wi[:, s2 + IN_DIFF + FNET_WIDTH:].astype(BF16)
        hd = MLA_NOPE + MLA_ROPE
        w_uq_p = jnp.pad(w_uq[l].reshape(MLA_Q_RANK, MLA_HEADS, hd),
                         ((0, 0), (0, 0), (0, 256 - hd))).reshape(MLA_Q_RANK, MLA_HEADS * 256).astype(BF16)
        wkv = w_ukv[l].reshape(MLA_KV_RANK, MLA_HEADS, MLA_NOPE + MLA_V)
        w_ukv_p = jnp.concatenate([wkv[:, :, :MLA_NOPE].reshape(MLA_KV_RANK, -1),
                                   wkv[:, :, MLA_NOPE:].reshape(MLA_KV_RANK, -1)], axis=1).astype(BF16)
        lam_vecs = jnp.stack([lam_q1[l], lam_k1[l], lam_q2[l], lam_k2[l]], axis=0)

        if pending is None:
            h = norm_modulate(xs, g_norm1[l], mods[l], 0, 1, nctx)
        else:
            xs, h = norm_modulate(xs, g_norm1[l], mods[l], 0, 1, nctx,
                                  delta=pending[0], mod_gate=pending[1], gate_row=5)
        hf = h.reshape(M, D)
        z_mla = matmul(hf, w_mla, BF16, "in_proj_mla").reshape(B, T, -1)
        z_diff = matmul(hf, w_diff, BF16, "in_proj_diff").reshape(B, T, -1)
        z_fnet = matmul(hf, w_fnet, BF16, "in_proj_fnet").reshape(B, T, -1)
        z_conv = matmul(hf, w_conv, BF16, "in_proj_conv").reshape(B, T, -1)

        qm, km, vm, qd, kd, vd = attention_projections(z_mla, z_diff, cos_t, sin_t, g_q_mla[l], g_kv_mla[l],
                                                   w_uq_p, w_ukv_p)
        o_mla = mla_attention(qm, km, vm, ctx_len)
        o_diff = diff_attention(qd, kd, vd, lam_vecs, g_subln[l], lam_init, ctx_len)
        o_fnet = jnp.concatenate([fourier_mix_short(z_fnet[:, :ctx_len]), fourier_mix_long(z_fnet[:, ctx_len:])],
                                 axis=1)
        o_conv = conformer_conv(z_conv, w_dw[l], b_dw[l], g_conv_norm[l], b_conv_norm[l], nctx)

        y = gated_merge(hf, [o.reshape(M, BRANCH_W) for o in (o_mla, o_fnet, o_conv, o_diff)],
                        w_gate[l].astype(BF16), w_branch[l].astype(BF16))
        mix = matmul(y, w_out[l].astype(BF16), F32, "out_proj").reshape(B, T, D)
        xs, h2 = norm_modulate(xs, g_norm2[l], mods[l], 3, 4, nctx, delta=mix, mod_gate=mods[l], gate_row=2)

        h2f = h2.reshape(M, D)
        q = matmul(h2f, w_peer_q[l].astype(BF16), BF16, "peer_query")
        scores = peer_scores(q, peer_keys[l].astype(BF16))
        s1f, s2f, thr = peer_thresholds(scores)
        peer_out = peer_experts(h2f, peer_u[l].astype(BF16), peer_v[l].T.astype(BF16), s1f, s2f, thr)
        pending = (peer_out.reshape(B, T, D), mods[l])

    return final_norm(xs, pending[0], pending[1], g_final, nctx, 5)
```

```python
import functools
import math

import numpy as np
import jax
import jax.numpy as jnp
from jax import lax
from jax.experimental import pallas as pl
from jax.experimental.pallas import tpu as pltpu

F32 = jnp.float32
BF16 = jnp.bfloat16

GRID_W = 64
ROPE_BASE = 10000.0
EPS = 1e-6
MLA_HEADS = 4
MLA_Q_RANK = 384
MLA_KV_RANK = 256
MLA_NOPE = 128
MLA_ROPE = 64
MLA_V = 128
DIFF_HEADS = 4
DIFF_QK = 64
DIFF_V = 128
FNET_GROUPS = 4
FNET_GROUP_DIM = 128
FNET_WIDTH = 512
CONV_WIDTH = 512
CONV_KERNEL = 31
CONV_GROUPS = 4
N_BRANCH = 4
BRANCH_W = 512
IN_MLA = MLA_Q_RANK + MLA_KV_RANK + MLA_ROPE
IN_DIFF = DIFF_HEADS * (4 * DIFF_QK + DIFF_V)
PEER_HEADS = 8
PEER_NKEYS = 128
PEER_EXPERTS = PEER_NKEYS * PEER_NKEYS
PEER_DKEY = 256
PEER_TOPK = 16

LANES = 128
ROW_TILE = 256
FLAT_TILE = 512
VMEM_LIMIT = 56 * 1024 * 1024
NEG_INF = float("-inf")
LOG2E = math.log2(math.e)


def _cparams(sem):
    return pltpu.CompilerParams(dimension_semantics=sem, vmem_limit_bytes=VMEM_LIMIT)


def _pick(n, cands):
    for c in cands:
        if n % c == 0:
            return c
    raise ValueError(f"no tile for {n} in {cands}")


def _ada_kernel(cb_ref, w_ref, b_ref, o_ref, *, rows, tn):
    outs = []
    for r in range(rows):
        a = cb_ref[r]
        a = a * jax.nn.sigmoid(a)
        cols = [jnp.sum(w_ref[:, j * LANES:(j + 1) * LANES] * a, axis=0, keepdims=True)
                for j in range(tn // LANES)]
        outs.append(jnp.concatenate(cols, axis=1) + b_ref[...])
    outs.append(jnp.zeros((8 - rows, tn), F32))
    o_ref[...] = jnp.concatenate(outs, axis=0)


def ada_modulation(cond, w_ada, b_ada):
    rows, d = cond.shape
    depth, _, n = w_ada.shape
    tn = 512
    cb = jnp.broadcast_to(cond[:, :, None], (rows, d, LANES))
    return pl.pallas_call(
        functools.partial(_ada_kernel, rows=rows, tn=tn),
        grid=(depth, n // tn),
        in_specs=[pl.BlockSpec((rows, d, LANES), lambda l, j: (0, 0, 0)),
                  pl.BlockSpec((None, d, tn), lambda l, j: (l, 0, j)),
                  pl.BlockSpec((None, 1, tn), lambda l, j: (l, 0, j))],
        out_specs=pl.BlockSpec((None, 8, tn), lambda l, j: (l, 0, j)),
        out_shape=jax.ShapeDtypeStruct((depth, 8, n), F32),
        compiler_params=_cparams(("arbitrary", "arbitrary")),
        name="ada_modulation",
    )(cb, w_ada, b_ada.reshape(depth, 1, n))


def _norm_mod_kernel(*refs, has_delta, gate_row, shift_row, scale_row):
    if has_delta:
        x_ref, d_ref, mg_ref, ms_ref, g_ref, xo_ref, h_ref = refs
        x = x_ref[...] + mg_ref[gate_row:gate_row + 1, :] * d_ref[...].astype(F32)
        xo_ref[...] = x
    else:
        x_ref, ms_ref, g_ref, h_ref = refs
        x = x_ref[...]
    y = x * lax.rsqrt(jnp.mean(x * x, axis=-1, keepdims=True) + EPS) * g_ref[...]
    h = y * (1.0 + ms_ref[scale_row:scale_row + 1, :]) + ms_ref[shift_row:shift_row + 1, :]
    h_ref[...] = h.astype(BF16)


def norm_modulate(x, g, mod_ss, shift_row, scale_row, nctx, delta=None, mod_gate=None, gate_row=None):
    B, T, D = x.shape
    tm = ROW_TILE
    row_spec = pl.BlockSpec((None, tm, D), lambda b, i: (b, i, 0))
    mod_spec = pl.BlockSpec((None, 8, D), lambda b, i: (jnp.where(i < nctx, B, b), 0, 0))
    g_spec = pl.BlockSpec((1, D), lambda b, i: (0, 0))
    h_shape = jax.ShapeDtypeStruct((B, T, D), BF16)
    kern = functools.partial(_norm_mod_kernel, has_delta=delta is not None, gate_row=gate_row,
                             shift_row=shift_row, scale_row=scale_row)
    if delta is None:
        return pl.pallas_call(
            kern, grid=(B, T // tm), in_specs=[row_spec, mod_spec, g_spec], out_specs=row_spec,
            out_shape=h_shape, compiler_params=_cparams(("parallel", "parallel")), name="norm_modulate",
        )(x, mod_ss, g.reshape(1, D))
    return pl.pallas_call(
        kern, grid=(B, T // tm), in_specs=[row_spec, row_spec, mod_spec, mod_spec, g_spec],
        out_specs=[row_spec, row_spec],
        out_shape=[jax.ShapeDtypeStruct((B, T, D), F32), h_shape],
        compiler_params=_cparams(("parallel", "parallel")), name="residual_norm_modulate",
    )(x, delta, mod_gate, mod_ss, g.reshape(1, D))


def _mm_kernel(x_ref, w_ref, o_ref):
    o_ref[...] = jnp.dot(x_ref[...], w_ref[...], preferred_element_type=F32).astype(o_ref.dtype)


def matmul(x, w, out_dtype, name):
    M, K = x.shape
    N = w.shape[1]
    tm = _pick(M, (FLAT_TILE, 256, 128))
    tn = _pick(N, (1024, 768, 512, 256, 128))
    return pl.pallas_call(
        _mm_kernel, grid=(N // tn, M // tm),
        in_specs=[pl.BlockSpec((tm, K), lambda j, i: (i, 0)),
                  pl.BlockSpec((K, tn), lambda j, i: (0, j))],
        out_specs=pl.BlockSpec((tm, tn), lambda j, i: (i, j)),
        out_shape=jax.ShapeDtypeStruct((M, N), out_dtype),
        compiler_params=_cparams(("parallel", "parallel")), name=name,
    )(x, w)


def _rope(v, cos, sin, lane):
    r = jnp.where((lane % 64) < 32, pltpu.roll(v, 96, 1), pltpu.roll(v, 32, 1))
    return v * cos + r * sin


def _proj_kernel(zm_ref, zd_ref, cos_ref, sin_ref, gq_ref, gkv_ref, wuq_ref, wukv_ref,
                 qm_ref, km_ref, vm_ref, qd_ref, kd_ref, vd_ref, *, mla_scale, diff_scale):
    tm = zm_ref.shape[0]
    cos = cos_ref[...]
    sin = sin_ref[...]
    lane = lax.broadcasted_iota(jnp.int32, (tm, LANES), 1)
    ones_col = jnp.where(lane == 0, 1.0, 0.0).astype(BF16)

    def rms(v, g):
        return v * lax.rsqrt(jnp.mean(v * v, axis=-1, keepdims=True) + EPS) * g

    zm = zm_ref[...].astype(F32)
    qn = rms(zm[:, :MLA_Q_RANK], gq_ref[...]).astype(BF16)
    kvn = rms(zm[:, MLA_Q_RANK:MLA_Q_RANK + MLA_KV_RANK], gkv_ref[...]).astype(BF16)
    k_rope = _rope(zm[:, MLA_Q_RANK + MLA_KV_RANK:], cos, sin, lane).astype(BF16)
    q = jnp.dot(qn, wuq_ref[...], preferred_element_type=F32)
    kv = jnp.dot(kvn, wukv_ref[...], preferred_element_type=F32)
    for h in range(MLA_HEADS):
        qm_ref[h, :, 0:LANES] = (q[:, 256 * h:256 * h + LANES] * mla_scale).astype(BF16)
        qm_ref[h, :, LANES:2 * LANES] = (
            _rope(q[:, 256 * h + LANES:256 * h + 2 * LANES], cos, sin, lane) * mla_scale).astype(BF16)
        km_ref[h, :, 0:LANES] = kv[:, LANES * h:LANES * (h + 1)].astype(BF16)
        km_ref[h, :, LANES:2 * LANES] = k_rope
        vm_ref[h, :, 0:LANES] = kv[:, 512 + LANES * h:512 + LANES * (h + 1)].astype(BF16)
        vm_ref[h, :, LANES:2 * LANES] = ones_col
    zd = zd_ref[...].astype(F32)
    for h in range(DIFF_HEADS):
        qd_ref[h] = (_rope(zd[:, LANES * h:LANES * (h + 1)], cos, sin, lane) * diff_scale).astype(BF16)
        kd_ref[h] = _rope(zd[:, 512 + LANES * h:512 + LANES * (h + 1)], cos, sin, lane).astype(BF16)
        vd_ref[h, :, 0:LANES] = zd_ref[:, 1024 + LANES * h:1024 + LANES * (h + 1)]
        vd_ref[h, :, LANES:2 * LANES] = ones_col


def attention_projections(z_mla, z_diff, cos_t, sin_t, g_q, g_kv, w_uq_p, w_ukv_p):
    B, T, _ = z_mla.shape
    tm = ROW_TILE
    H = MLA_HEADS

    def head_spec(w):
        return pl.BlockSpec((None, H, tm, w), lambda b, i: (b, 0, i, 0))

    def head_shape(w):
        return jax.ShapeDtypeStruct((B, H, T, w), BF16)

    return pl.pallas_call(
        functools.partial(_proj_kernel, mla_scale=float((MLA_NOPE + MLA_ROPE) ** -0.5 * LOG2E),
                          diff_scale=float(DIFF_QK ** -0.5 * LOG2E)),
        grid=(B, T // tm),
        in_specs=[pl.BlockSpec((None, tm, z_mla.shape[2]), lambda b, i: (b, i, 0)),
                  pl.BlockSpec((None, tm, z_diff.shape[2]), lambda b, i: (b, i, 0)),
                  pl.BlockSpec((tm, LANES), lambda b, i: (i, 0)),
                  pl.BlockSpec((tm, LANES), lambda b, i: (i, 0)),
                  pl.BlockSpec((1, MLA_Q_RANK), lambda b, i: (0, 0)),
                  pl.BlockSpec((1, MLA_KV_RANK), lambda b, i: (0, 0)),
                  pl.BlockSpec(w_uq_p.shape, lambda b, i: (0, 0)),
                  pl.BlockSpec(w_ukv_p.shape, lambda b, i: (0, 0))],
        out_specs=[head_spec(256), head_spec(256), head_spec(256), head_spec(128), head_spec(128), head_spec(256)],
        out_shape=[head_shape(256), head_shape(256), head_shape(256), head_shape(128), head_shape(128),
                   head_shape(256)],
        compiler_params=_cparams(("parallel", "parallel")), name="attention_projections",
    )(z_mla, z_diff, cos_t, sin_t, g_q.reshape(1, -1), g_kv.reshape(1, -1), w_uq_p, w_ukv_p)


V_EXT = 2 * LANES


def _att_scores(q, k, mask):
    s = lax.dot_general(q, k, (((1,), (1,)), ((), ())), preferred_element_type=F32)
    return s if mask is None else jnp.where(mask, NEG_INF, s)


def _lane_tile(x, width):
    return x if width == LANES else jnp.concatenate([x] * (width // LANES), axis=1)


def _att_update(s, v_ext, m_ref, acc_ref, idx):
    m_prev = m_ref[idx]
    m_new = jnp.maximum(m_prev, jnp.max(s, axis=-1, keepdims=True))
    p = jnp.exp2(s - _lane_tile(m_new, s.shape[1])).astype(BF16)
    alpha = _lane_tile(jnp.exp2(m_prev - m_new), V_EXT)
    acc_ref[idx] = alpha * acc_ref[idx] + jnp.dot(p, v_ext, preferred_element_type=F32)
    m_ref[idx] = m_new


def _att_init(m_ref, acc_ref):
    m_ref[...] = jnp.full(m_ref.shape, NEG_INF, F32)
    acc_ref[...] = jnp.zeros(acc_ref.shape, F32)


def _att_step(streams, mask, m_ref, acc_ref):
    for idx, (q, k, v) in enumerate(streams):
        _att_update(_att_scores(q(), k(), mask), v(), m_ref, acc_ref, idx)


def _att_sweep(step, qi, ki, tq, tk, ctx_len):
    @pl.when(qi == 0)
    def _():
        row = lax.broadcasted_iota(jnp.int32, (tq, tk), 0)
        col = ki * tk + lax.broadcasted_iota(jnp.int32, (tq, tk), 1)
        step((row < ctx_len) & (col >= ctx_len))

    @pl.when(qi != 0)
    def _():
        step(None)


def _att_result(acc):
    return acc[:, :LANES] / acc[:, LANES:LANES + 1]


def _mla_att_kernel(q_ref, k_ref, v_ref, o_ref, m_ref, acc_ref, *, tq, tk, ctx_len):
    qi = pl.program_id(1)
    ki = pl.program_id(2)

    @pl.when(ki == 0)
    def _():
        _att_init(m_ref, acc_ref)

    streams = [(lambda h=h: q_ref[h], lambda h=h: k_ref[h], lambda h=h: v_ref[h]) for h in range(MLA_HEADS)]
    _att_sweep(lambda mask: _att_step(streams, mask, m_ref, acc_ref), qi, ki, tq, tk, ctx_len)

    @pl.when(ki == pl.num_programs(2) - 1)
    def _():
        for h in range(MLA_HEADS):
            o_ref[:, h * MLA_V:(h + 1) * MLA_V] = _att_result(acc_ref[h]).astype(o_ref.dtype)


def _att_specs(H, tq, tk, dk):
    return [pl.BlockSpec((None, H, tq, dk), lambda b, i, j: (b, 0, i, 0)),
            pl.BlockSpec((None, H, tk, dk), lambda b, i, j: (b, 0, j, 0)),
            pl.BlockSpec((None, H, tk, V_EXT), lambda b, i, j: (b, 0, j, 0))]


def _att_scratch(nstreams, tq):
    return [pltpu.VMEM((nstreams, tq, LANES), F32), pltpu.VMEM((nstreams, tq, V_EXT), F32)]


def mla_attention(qm, km, vm, ctx_len):
    B, H, T, dk = qm.shape
    tq = tk = _pick(T, (768, 256))
    assert ctx_len <= tq
    return pl.pallas_call(
        functools.partial(_mla_att_kernel, tq=tq, tk=tk, ctx_len=ctx_len),
        grid=(B, T // tq, T // tk),
        in_specs=_att_specs(H, tq, tk, dk),
        out_specs=pl.BlockSpec((None, tq, H * MLA_V), lambda b, i, j: (b, i, 0)),
        out_shape=jax.ShapeDtypeStruct((B, T, H * MLA_V), BF16),
        scratch_shapes=_att_scratch(H, tq),
        compiler_params=_cparams(("parallel", "parallel", "arbitrary")), name="mla_attention",
    )(qm, km, vm)


def _diff_att_kernel(q_ref, k_ref, v_ref, lam_ref, g_ref, o_ref, m_ref, acc_ref, *, tq, tk, ctx_len, lam_init):
    qi = pl.program_id(1)
    ki = pl.program_id(2)

    @pl.when(ki == 0)
    def _():
        _att_init(m_ref, acc_ref)

    def component(h, c):
        q = q_ref[h]
        first = lax.broadcasted_iota(jnp.int32, q.shape, 1) < DIFF_QK
        return jnp.where(first if c == 0 else jnp.logical_not(first), q, jnp.zeros_like(q))

    streams = [(lambda h=h, c=c: component(h, c), lambda h=h: k_ref[h], lambda h=h: v_ref[h])
               for h in range(DIFF_HEADS) for c in range(2)]
    _att_sweep(lambda mask: _att_step(streams, mask, m_ref, acc_ref), qi, ki, tq, tk, ctx_len)

    @pl.when(ki == pl.num_programs(2) - 1)
    def _():
        lv = lam_ref[...]
        lam = (jnp.exp(jnp.sum(lv[0:1] * lv[1:2], axis=-1, keepdims=True))
               - jnp.exp(jnp.sum(lv[2:3] * lv[3:4], axis=-1, keepdims=True)) + lam_init)
        for h in range(DIFF_HEADS):
            o = _att_result(acc_ref[2 * h]) - lam * _att_result(acc_ref[2 * h + 1])
            o = o * lax.rsqrt(jnp.mean(o * o, axis=-1, keepdims=True) + EPS) * g_ref[...]
            o_ref[:, h * DIFF_V:(h + 1) * DIFF_V] = (o * (1.0 - lam_init)).astype(o_ref.dtype)


def diff_attention(qd, kd, vd, lam_vecs, g_subln, lam_init, ctx_len):
    B, H, T, dk = qd.shape
    tq = tk = _pick(T, (768, 256))
    assert ctx_len <= tq
    return pl.pallas_call(
        functools.partial(_diff_att_kernel, tq=tq, tk=tk, ctx_len=ctx_len, lam_init=lam_init),
        grid=(B, T // tq, T // tk),
        in_specs=_att_specs(H, tq, tk, dk) + [pl.BlockSpec(lam_vecs.shape, lambda b, i, j: (0, 0)),
                                               pl.BlockSpec((1, DIFF_V), lambda b, i, j: (0, 0))],
        out_specs=pl.BlockSpec((None, tq, H * DIFF_V), lambda b, i, j: (b, i, 0)),
        out_shape=jax.ShapeDtypeStruct((B, T, H * DIFF_V), BF16),
        scratch_shapes=_att_scratch(2 * H, tq),
        compiler_params=_cparams(("parallel", "parallel", "arbitrary")), name="diff_attention",
    )(qd, kd, vd, lam_vecs, g_subln.reshape(1, -1))


def _dft_mats(n):
    k = np.arange(n)
    ang = 2.0 * np.pi * ((k[:, None] * k[None, :]) % n) / n
    return jnp.asarray(np.cos(ang), BF16), jnp.asarray(np.sin(ang), BF16)


def _channel_dft(x, cc, sc):
    tr, ti = [], []
    for s in range(x.shape[1] // LANES):
        xs = x[:, s * LANES:(s + 1) * LANES]
        tr.append(jnp.dot(xs, cc, preferred_element_type=F32))
        ti.append(-jnp.dot(xs, sc, preferred_element_type=F32))
    return jnp.concatenate(tr, axis=1), jnp.concatenate(ti, axis=1)


def _fnet_stage1_kernel(x_ref, cc_ref, sc_ref, c_ref, s_ref, twc_ref, tws_ref, yr_ref, yi_ref, *, nb):
    tr, ti = _channel_dft(x_ref[...], cc_ref[...], sc_ref[...])
    trb, tib = tr.astype(BF16), ti.astype(BF16)
    c, s = c_ref[...], s_ref[...]
    yr = jnp.dot(c, trb, preferred_element_type=F32) + jnp.dot(s, tib, preferred_element_type=F32)
    yi = jnp.dot(c, tib, preferred_element_type=F32) - jnp.dot(s, trb, preferred_element_type=F32)
    for i in range(nb):
        twc = jnp.concatenate([twc_ref[i]] * FNET_GROUPS, axis=1)
        tws = jnp.concatenate([tws_ref[i]] * FNET_GROUPS, axis=1)
        a = yr[:, i * FNET_WIDTH:(i + 1) * FNET_WIDTH]
        b = yi[:, i * FNET_WIDTH:(i + 1) * FNET_WIDTH]
        yr_ref[i] = a * twc + b * tws
        yi_ref[i] = b * twc - a * tws


def _fnet_stage2_kernel(yr_ref, yi_ref, c_ref, s_ref, o_ref, *, scale):
    xr = (jnp.dot(c_ref[...], yr_ref[...].astype(BF16), preferred_element_type=F32)
          + jnp.dot(s_ref[...], yi_ref[...].astype(BF16), preferred_element_type=F32))
    o_ref[...] = (xr * scale).astype(o_ref.dtype)


def _fnet_direct_kernel(x_ref, cc_ref, sc_ref, c_ref, s_ref, o_ref, *, scale):
    tr, ti = _channel_dft(x_ref[...], cc_ref[...], sc_ref[...])
    xr = (jnp.dot(c_ref[...], tr.astype(BF16), preferred_element_type=F32)
          + jnp.dot(s_ref[...], ti.astype(BF16), preferred_element_type=F32))
    o_ref[...] = (xr * scale).astype(o_ref.dtype)


def _full(a):
    return pl.BlockSpec(a.shape, lambda *_: (0,) * a.ndim)


def fourier_mix_long(zf):
    B, L, W = zf.shape
    n2 = LANES
    n1 = L // n2
    nb = _pick(n1, (8,))
    cc, sc = _dft_mats(FNET_GROUP_DIM)
    c2, s2 = _dft_mats(n2)
    c1, s1 = _dft_mats(n1)
    ang = 2.0 * np.pi * (np.arange(n1)[:, None] * np.arange(n2)[None, :]) / L
    twc = jnp.asarray(np.broadcast_to(np.cos(ang)[:, :, None], (n1, n2, LANES)), F32)
    tws = jnp.asarray(np.broadcast_to(np.sin(ang)[:, :, None], (n1, n2, LANES)), F32)
    x = zf.reshape(B, n2, n1 * W)
    y_shape = jax.ShapeDtypeStruct((B, n1, n2, W), F32)
    y_spec = pl.BlockSpec((None, nb, n2, W), lambda b, j: (b, j, 0, 0))
    tw_spec = pl.BlockSpec((nb, n2, LANES), lambda b, j: (j, 0, 0))
    yr, yi = pl.pallas_call(
        functools.partial(_fnet_stage1_kernel, nb=nb), grid=(B, n1 // nb),
        in_specs=[pl.BlockSpec((None, n2, nb * W), lambda b, j: (b, 0, j)),
                  _full(cc), _full(sc), _full(c2), _full(s2), tw_spec, tw_spec],
        out_specs=[y_spec, y_spec], out_shape=[y_shape, y_shape],
        compiler_params=_cparams(("parallel", "parallel")), name="fnet_stage1",
    )(x, cc, sc, c2, s2, twc, tws)
    cols = n2 * W
    tn = 4096
    y2_spec = pl.BlockSpec((None, n1, tn), lambda b, j: (b, 0, j))
    out = pl.pallas_call(
        functools.partial(_fnet_stage2_kernel, scale=float((L * FNET_GROUP_DIM) ** -0.5)),
        grid=(B, cols // tn),
        in_specs=[y2_spec, y2_spec, _full(c1), _full(s1)],
        out_specs=y2_spec, out_shape=jax.ShapeDtypeStruct((B, n1, cols), BF16),
        compiler_params=_cparams(("parallel", "parallel")), name="fnet_stage2",
    )(yr.reshape(B, n1, cols), yi.reshape(B, n1, cols), c1, s1)
    return out.reshape(B, L, W)


def fourier_mix_short(zf):
    B, L, W = zf.shape
    cc, sc = _dft_mats(FNET_GROUP_DIM)
    c, s = _dft_mats(L)
    spec = pl.BlockSpec((None, L, W), lambda b: (b, 0, 0))
    return pl.pallas_call(
        functools.partial(_fnet_direct_kernel, scale=float((L * FNET_GROUP_DIM) ** -0.5)), grid=(B,),
        in_specs=[spec, _full(cc), _full(sc), _full(c), _full(s)],
        out_specs=spec, out_shape=jax.ShapeDtypeStruct((B, L, W), BF16),
        compiler_params=_cparams(("parallel",)), name="fnet_direct",
    )(zf, cc, sc, c, s)


HALO = 16


def _conv_kernel(prev_ref, cur_ref, next_ref, w_ref, bdw_ref, g_ref, b_ref, o_ref, ext_ref, *, nctx):
    i = pl.program_id(1)
    nt = pl.num_programs(1)
    tm = cur_ref.shape[0]

    def glu(z):
        z = z.astype(F32)
        return z[:, :CONV_WIDTH] * jax.nn.sigmoid(z[:, CONV_WIDTH:])

    has_prev = jnp.logical_and(i != 0, i != nctx)
    has_next = jnp.logical_and(i != nctx - 1, i != nt - 1)
    ext_ref[0:HALO, :] = jnp.where(has_prev, glu(prev_ref[tm - HALO:tm, :]), 0.0)
    ext_ref[HALO:HALO + tm, :] = glu(cur_ref[...])
    ext_ref[HALO + tm:2 * HALO + tm, :] = jnp.where(has_next, glu(next_ref[0:HALO, :]), 0.0)
    pad = CONV_KERNEL // 2
    acc = jnp.zeros((tm, CONV_WIDTH), F32)
    for k in range(CONV_KERNEL):
        acc = acc + w_ref[k:k + 1, :] * ext_ref[pl.ds(HALO - pad + k, tm), :]
    y = acc + bdw_ref[...]
    gw = CONV_WIDTH // CONV_GROUPS
    outs = []
    for gi in range(CONV_GROUPS):
        yg = y[:, gi * gw:(gi + 1) * gw]
        mu = jnp.mean(yg, axis=-1, keepdims=True)
        var = jnp.mean(jnp.square(yg - mu), axis=-1, keepdims=True)
        outs.append((yg - mu) * lax.rsqrt(var + 1e-5))
    yn = jnp.concatenate(outs, axis=1) * g_ref[...] + b_ref[...]
    o_ref[...] = (yn * jax.nn.sigmoid(yn)).astype(o_ref.dtype)


def conformer_conv(zc, w_dw, b_dw, g, b, nctx):
    B, T, W2 = zc.shape
    tm = ROW_TILE
    nt = T // tm
    w_pad = jnp.concatenate([w_dw, jnp.zeros((32 - CONV_KERNEL, CONV_WIDTH), F32)], axis=0)
    vec = pl.BlockSpec((1, CONV_WIDTH), lambda bb, i: (0, 0))
    return pl.pallas_call(
        functools.partial(_conv_kernel, nctx=nctx), grid=(B, nt),
        in_specs=[pl.BlockSpec((None, tm, W2), lambda bb, i: (bb, jnp.maximum(i - 1, 0), 0)),
                  pl.BlockSpec((None, tm, W2), lambda bb, i: (bb, i, 0)),
                  pl.BlockSpec((None, tm, W2), lambda bb, i: (bb, jnp.minimum(i + 1, nt - 1), 0)),
                  pl.BlockSpec((32, CONV_WIDTH), lambda bb, i: (0, 0)), vec, vec, vec],
        out_specs=pl.BlockSpec((None, tm, CONV_WIDTH), lambda bb, i: (bb, i, 0)),
        out_shape=jax.ShapeDtypeStruct((B, T, CONV_WIDTH), BF16),
        scratch_shapes=[pltpu.VMEM((tm + 2 * HALO, CONV_WIDTH), F32)],
        compiler_params=_cparams(("parallel", "parallel")), name="conformer_conv",
    )(zc, zc, zc, w_pad, b_dw.reshape(1, -1), g.reshape(1, -1), b.reshape(1, -1))


def _merge_kernel(h_ref, b0_ref, b1_ref, b2_ref, b3_ref, wg_ref, wb_ref, o_ref):
    h = h_ref[...]
    acc = None
    for n, br in enumerate((b0_ref, b1_ref, b2_ref, b3_ref)):
        gate = jax.nn.sigmoid(jnp.dot(h, wg_ref[n], preferred_element_type=F32))
        term = gate * jnp.dot(br[...], wb_ref[n], preferred_element_type=F32)
        acc = term if acc is None else acc + term
    o_ref[...] = acc.astype(o_ref.dtype)


def gated_merge(h, branches, w_gate, w_branch):
    M, D = h.shape
    N = w_gate.shape[2]
    tm = _pick(M, (FLAT_TILE, 256))
    tn = _pick(N, (512, 256))
    br_spec = pl.BlockSpec((tm, BRANCH_W), lambda j, i: (i, 0))
    return pl.pallas_call(
        _merge_kernel, grid=(N // tn, M // tm),
        in_specs=[pl.BlockSpec((tm, D), lambda j, i: (i, 0)), br_spec, br_spec, br_spec, br_spec,
                  pl.BlockSpec((N_BRANCH, D, tn), lambda j, i: (0, 0, j)),
                  pl.BlockSpec((N_BRANCH, BRANCH_W, tn), lambda j, i: (0, 0, j))],
        out_specs=pl.BlockSpec((tm, tn), lambda j, i: (i, j)),
        out_shape=jax.ShapeDtypeStruct((M, N), BF16),
        compiler_params=_cparams(("parallel", "parallel")), name="gated_merge",
    )(h, *branches, w_gate, w_branch)


def _peer_scores_kernel(q_ref, keys_ref, o_ref):
    for hp in range(2 * PEER_HEADS):
        o_ref[hp] = lax.dot_general(keys_ref[hp % 2], q_ref[:, hp * LANES:(hp + 1) * LANES],
                                    (((1,), (1,)), ((), ())), preferred_element_type=F32)


def peer_scores(q, keys):
    M = q.shape[0]
    tm = _pick(M, (FLAT_TILE, 256))
    nhp = 2 * PEER_HEADS
    return pl.pallas_call(
        _peer_scores_kernel, grid=(M // tm,),
        in_specs=[pl.BlockSpec((tm, q.shape[1]), lambda i: (i, 0)), _full(keys)],
        out_specs=pl.BlockSpec((nhp, PEER_NKEYS, tm), lambda i: (0, 0, i)),
        out_shape=jax.ShapeDtypeStruct((nhp, PEER_NKEYS, M), F32),
        compiler_params=_cparams(("parallel",)), name="peer_scores",
    )(q, keys)


_NRANK = PEER_TOPK + 1
_STAIRCASE = [(a, b) for a in range(_NRANK) for b in range(_NRANK) if (a + 1) * (b + 1) <= _NRANK]
_NCAND = -(-len(_STAIRCASE) // 8) * 8
_NRANK_PAD = -(-_NRANK // 8) * 8


def _peer_threshold_kernel(s_ref, s1f_ref, s2f_ref, thr_ref, t1_ref, t2_ref, c_ref):
    def extract(vals, dst_ref):
        v = vals
        m = prev = None
        for r in range(_NRANK):
            prev = m
            m = jnp.max(v, axis=0, keepdims=True)
            if dst_ref is not None:
                dst_ref[r:r + 1, :] = m
            v = jnp.where(v == m, NEG_INF, v)
        return prev, m

    for h in range(PEER_HEADS):
        u1 = s_ref[2 * h] * LOG2E
        u2 = s_ref[2 * h + 1] * LOG2E
        extract(u1, t1_ref)
        extract(u2, t2_ref)
        m1 = t1_ref[0:1, :]
        m2 = t2_ref[0:1, :]
        c_ref[...] = jnp.full(c_ref.shape, NEG_INF, F32)
        for r, (a, b) in enumerate(_STAIRCASE):
            c_ref[r:r + 1, :] = (t1_ref[a:a + 1, :] - m1) + (t2_ref[b:b + 1, :] - m2)
        cand = c_ref[...]
        c16, c17 = extract(cand, None)
        thr = 0.5 * (c16 + c17)
        z = jnp.sum(jnp.where(cand >= thr, jnp.exp2(cand), 0.0), axis=0, keepdims=True)
        lz = -jnp.log2(z)
        thr_ref[h:h + 1, :] = thr + lz
        s1f_ref[h] = (u1 - m1) + lz
        s2f_ref[h] = u2 - m2


def peer_thresholds(scores):
    nhp, nk, M = scores.shape
    tl = _pick(M, (256, 128))
    sf_spec = pl.BlockSpec((PEER_HEADS, nk, tl), lambda i: (0, 0, i))
    sf_shape = jax.ShapeDtypeStruct((PEER_HEADS, nk, M), F32)
    return pl.pallas_call(
        _peer_threshold_kernel, grid=(M // tl,),
        in_specs=[pl.BlockSpec((nhp, nk, tl), lambda i: (0, 0, i))],
        out_specs=[sf_spec, sf_spec, pl.BlockSpec((PEER_HEADS, tl), lambda i: (0, i))],
        out_shape=[sf_shape, sf_shape, jax.ShapeDtypeStruct((PEER_HEADS, M), F32)],
        scratch_shapes=[pltpu.VMEM((_NRANK_PAD, tl), F32), pltpu.VMEM((_NRANK_PAD, tl), F32),
                        pltpu.VMEM((_NCAND, tl), F32)],
        compiler_params=_cparams(("parallel",)), name="peer_thresholds",
    )(scores)


PEER_TOKEN_SPLIT = 256
PEER_EXPERT_SPLIT = 512


def _peer_experts_kernel(x_ref, u_ref, vt_ref, s1_ref, s2_ref, thr_ref, o_ref,
                         acc_ref, e1_ref, e2_ref, thr2_ref, *scratch, ek):
    j = pl.program_id(1)
    tm = x_ref.shape[0]
    th, es = PEER_TOKEN_SPLIT, PEER_EXPERT_SPLIT
    pieces = [(c, t) for c in range(ek // es) for t in range(tm // th)]
    st_refs, a_refs = scratch[:len(pieces)], scratch[len(pieces):]

    @pl.when(j == 0)
    def _():
        acc_ref[...] = jnp.zeros(acc_ref.shape, F32)
        for h in range(PEER_HEADS):
            s1 = s1_ref[h]
            e1_ref[h] = jnp.exp2(s1)
            e2_ref[h] = jnp.exp2(s2_ref[h])
            thr2_ref[h] = thr_ref[h:h + 1, :] - s1

    def pre_activations(p):
        c, t = pieces[p]
        st_refs[p][...] = lax.dot_general(u_ref[c * es:(c + 1) * es, :], x_ref[t * th:(t + 1) * th, :],
                                          (((1,), (1,)), ((), ())), preferred_element_type=F32)

    def weights(p):
        c, t = pieces[p]
        for r in range(es // PEER_NKEYS):
            i1 = j * (ek // PEER_NKEYS) + c * (es // PEER_NKEYS) + r
            rows = slice(r * PEER_NKEYS, (r + 1) * PEER_NKEYS)
            half = slice(t * th, (t + 1) * th)
            thr_rows = [thr2_ref[h, pl.ds(i1, 1), half] for h in range(PEER_HEADS)]
            e1_rows = [e1_ref[h, pl.ds(i1, 1), half] for h in range(PEER_HEADS)]
            for g in range(th // LANES):
                lanes = slice(t * th + g * LANES, t * th + (g + 1) * LANES)
                sub = slice(g * LANES, (g + 1) * LANES)
                w = None
                for h in range(PEER_HEADS):
                    sel = jnp.where(s2_ref[h, :, lanes] >= thr_rows[h][:, sub], e2_ref[h, :, lanes], 0.0)
                    sel = sel * e1_rows[h][:, sub]
                    w = sel if w is None else w + sel
                pre = st_refs[p][rows, g * LANES:(g + 1) * LANES]
                act = pre * (0.5 + 0.5 * lax.erf(pre * (2.0 ** -0.5)))
                a_refs[p][rows, g * LANES:(g + 1) * LANES] = (w * act).astype(BF16)

    def accumulate(p):
        c, t = pieces[p]
        lanes = slice(t * th, (t + 1) * th)
        acc_ref[:, lanes] += jnp.dot(vt_ref[:, c * es:(c + 1) * es], a_refs[p][...], preferred_element_type=F32)

    for p in range(len(pieces)):
        pre_activations(p)
    for p in range(len(pieces)):
        weights(p)
        accumulate(p)

    @pl.when(j == pl.num_programs(1) - 1)
    def _():
        o_ref[...] = acc_ref[...].T


def peer_experts(x, u, vt, s1f, s2f, thr):
    M, D = x.shape
    E = u.shape[0]
    tm = _pick(M, (FLAT_TILE, 256))
    ek = 1024
    th, es = PEER_TOKEN_SPLIT, PEER_EXPERT_SPLIT
    npieces = (tm // th) * (ek // es)
    sf_spec = pl.BlockSpec((PEER_HEADS, PEER_NKEYS, tm), lambda i, j: (0, 0, i))
    sf_scratch = pltpu.VMEM((PEER_HEADS, PEER_NKEYS, tm), F32)
    return pl.pallas_call(
        functools.partial(_peer_experts_kernel, ek=ek), grid=(M // tm, E // ek),
        in_specs=[pl.BlockSpec((tm, D), lambda i, j: (i, 0)),
                  pl.BlockSpec((ek, D), lambda i, j: (j, 0)),
                  pl.BlockSpec((D, ek), lambda i, j: (0, j)),
                  sf_spec, sf_spec, pl.BlockSpec((PEER_HEADS, tm), lambda i, j: (0, i))],
        out_specs=pl.BlockSpec((tm, D), lambda i, j: (i, 0)),
        out_shape=jax.ShapeDtypeStruct((M, D), F32),
        scratch_shapes=([pltpu.VMEM((D, tm), F32), sf_scratch, sf_scratch, sf_scratch]
                        + [pltpu.VMEM((es, th), F32)] * npieces + [pltpu.VMEM((es, th), BF16)] * npieces),
        compiler_params=_cparams(("parallel", "arbitrary")), name="peer_experts",
    )(x, u, vt, s1f, s2f, thr)


def _final_kernel(x_ref, d_ref, mg_ref, g_ref, o_ref, *, gate_row):
    x = x_ref[...] + mg_ref[gate_row:gate_row + 1, :] * d_ref[...]
    o_ref[...] = x * lax.rsqrt(jnp.mean(x * x, axis=-1, keepdims=True) + EPS) * g_ref[...]


def final_norm(x, delta, mod_gate, g, nctx, gate_row):
    B, T, D = x.shape
    tm = ROW_TILE
    L = T - nctx * tm
    row_spec = pl.BlockSpec((None, tm, D), lambda b, i: (b, i + nctx, 0))
    return pl.pallas_call(
        functools.partial(_final_kernel, gate_row=gate_row), grid=(B, L // tm),
        in_specs=[row_spec, row_spec, pl.BlockSpec((None, 8, D), lambda b, i: (b, 0, 0)),
                  pl.BlockSpec((1, D), lambda b, i: (0, 0))],
        out_specs=pl.BlockSpec((None, tm, D), lambda b, i: (b, i, 0)),
        out_shape=jax.ShapeDtypeStruct((B, L, D), F32),
        compiler_params=_cparams(("parallel", "parallel")), name="final_norm",
    )(x, delta, mod_gate, g.reshape(1, D))


def _rope_tables(L, ctx_len):
    rows = L // GRID_W
    row = jnp.repeat(jnp.arange(rows, dtype=F32), GRID_W)
    col = jnp.tile(jnp.arange(GRID_W, dtype=F32), rows)
    nf = MLA_ROPE // 4
    inv = ROPE_BASE ** (-jnp.arange(nf, dtype=F32) / nf)
    ang = jnp.concatenate([row[:, None] * inv, col[:, None] * inv], axis=-1)
    cos, sin = jnp.cos(ang), jnp.sin(ang)
    cos_t = jnp.concatenate([cos, cos, cos, cos], axis=-1)
    sin_t = jnp.concatenate([-sin, sin, -sin, sin], axis=-1)
    cos_t = jnp.concatenate([jnp.ones((ctx_len, LANES), F32), cos_t], axis=0)
    sin_t = jnp.concatenate([jnp.zeros((ctx_len, LANES), F32), sin_t], axis=0)
    return cos_t, sin_t


def kernel(x, c, ctx, c_ctx, w_ada, b_ada, g_norm1, g_norm2, w_in, g_q_mla, w_uq, g_kv_mla, w_ukv, lam_q1, lam_k1, lam_q2, lam_k2, g_subln, w_dw, b_dw, g_conv_norm, b_conv_norm, w_gate, w_branch, w_out, w_peer_q, peer_keys, peer_u, peer_v, g_final):
    B, L, D = x.shape
    ctx_len = ctx.shape[1]
    depth = w_ada.shape[0]
    T = ctx_len + L
    M = B * T
    assert MLA_ROPE == DIFF_QK and ctx_len % ROW_TILE == 0 and L % ROW_TILE == 0
    nctx = ctx_len // ROW_TILE

    cos_t, sin_t = _rope_tables(L, ctx_len)
    xs = jnp.concatenate([ctx, x], axis=1)

    mods = ada_modulation(jnp.concatenate([c, c_ctx[None, :]], axis=0), w_ada, b_ada)
    mods = mods.reshape(depth, 8, 6, D)[:, :B + 1]
    mods = jnp.concatenate([mods, jnp.zeros((depth, B + 1, 2, D), F32)], axis=2)

    o_c, o_f, o_d = 0, 2 * CONV_WIDTH, 2 * CONV_WIDTH + FNET_WIDTH
    pending = None
    for l in range(depth):
        lam_init = 0.8 - 0.6 * math.exp(-0.3 * l)
        s0, s1, s2 = MLA_Q_RANK, MLA_Q_RANK + MLA_KV_RANK, IN_MLA
        wi = w_in[l]
        w_mla = jnp.concatenate([wi[:, :s2], jnp.zeros((D, LANES - MLA_ROPE), F32)], axis=1).astype(BF16)
        w_diff = wi[:, s2:s2 + IN_DIFF].astype(BF16)
        w_fnet = wi[:, s2 + IN_DIFF:s2 + IN_DIFF + FNET_WIDTH].astype(BF16)
        w_conv = wi[:, s2 + IN_DIFF + FNET_WIDTH:].astype(BF16)
        hd = MLA_NOPE + MLA_ROPE
        w_uq_p = jnp.pad(w_uq[l].reshape(MLA_Q_RANK, MLA_HEADS, hd),
                         ((0, 0), (0, 0), (0, 256 - hd))).reshape(MLA_Q_RANK, MLA_HEADS * 256).astype(BF16)
        wkv = w_ukv[l].reshape(MLA_KV_RANK, MLA_HEADS, MLA_NOPE + MLA_V)
        w_ukv_p = jnp.concatenate([wkv[:, :, :MLA_NOPE].reshape(MLA_KV_RANK, -1),
                                   wkv[:, :, MLA_NOPE:].reshape(MLA_KV_RANK, -1)], axis=1).astype(BF16)
        lam_vecs = jnp.stack([lam_q1[l], lam_k1[l], lam_q2[l], lam_k2[l]], axis=0)

        if pending is None:
            h = norm_modulate(xs, g_norm1[l], mods[l], 0, 1, nctx)
        else:
            xs, h = norm_modulate(xs, g_norm1[l], mods[l], 0, 1, nctx,
                                  delta=pending[0], mod_gate=pending[1], gate_row=5)
        hf = h.reshape(M, D)
        z_mla = matmul(hf, w_mla, BF16, "in_proj_mla").reshape(B, T, -1)
        z_diff = matmul(hf, w_diff, BF16, "in_proj_diff").reshape(B, T, -1)
        z_fnet = matmul(hf, w_fnet, BF16, "in_proj_fnet").reshape(B, T, -1)
        z_conv = matmul(hf, w_conv, BF16, "in_proj_conv").reshape(B, T, -1)

        qm, km, vm, qd, kd, vd = attention_projections(z_mla, z_diff, cos_t, sin_t, g_q_mla[l], g_kv_mla[l],
                                                   w_uq_p, w_ukv_p)
        o_mla = mla_attention(qm, km, vm, ctx_len)
        o_diff = diff_attention(qd, kd, vd, lam_vecs, g_subln[l], lam_init, ctx_len)
        o_fnet = jnp.concatenate([fourier_mix_short(z_fnet[:, :ctx_len]), fourier_mix_long(z_fnet[:, ctx_len:])],
                                 axis=1)
        o_conv = conformer_conv(z_conv, w_dw[l], b_dw[l], g_conv_norm[l], b_conv_norm[l], nctx)

        y = gated_merge(hf, [o.reshape(M, BRANCH_W) for o in (o_mla, o_fnet, o_conv, o_diff)],
                        w_gate[l].astype(BF16), w_branch[l].astype(BF16))
        mix = matmul(y, w_out[l].astype(BF16), F32, "out_proj").reshape(B, T, D)
        xs, h2 = norm_modulate(xs, g_norm2[l], mods[l], 3, 4, nctx, delta=mix, mod_gate=mods[l], gate_row=2)

        h2f = h2.reshape(M, D)
        q = matmul(h2f, w_peer_q[l].astype(BF16), BF16, "peer_query")
        scores = peer_scores(q, peer_keys[l].astype(BF16))
        s1f, s2f, thr = peer_thresholds(scores)
        peer_out = peer_experts(h2f, peer_u[l].astype(BF16), peer_v[l].T.astype(BF16), s1f, s2f, thr)
        pending = (peer_out.reshape(B, T, D), mods[l])

    return final_norm(xs, pending[0], pending[1], g_final, nctx, 5)
```

```python
import functools
import math

import numpy as np
import jax
import jax.numpy as jnp
from jax import lax
from jax.experimental import pallas as pl
from jax.experimental.pallas import tpu as pltpu

F32 = jnp.float32
BF16 = jnp.bfloat16

GRID_W = 64
ROPE_BASE = 10000.0
EPS = 1e-6
MLA_HEADS = 4
MLA_Q_RANK = 384
MLA_KV_RANK = 256
MLA_NOPE = 128
MLA_ROPE = 64
MLA_V = 128
DIFF_HEADS = 4
DIFF_QK = 64
DIFF_V = 128
FNET_GROUPS = 4
FNET_GROUP_DIM = 128
FNET_WIDTH = 512
CONV_WIDTH = 512
CONV_KERNEL = 31
CONV_GROUPS = 4
N_BRANCH = 4
BRANCH_W = 512
IN_MLA = MLA_Q_RANK + MLA_KV_RANK + MLA_ROPE
IN_DIFF = DIFF_HEADS * (4 * DIFF_QK + DIFF_V)
PEER_HEADS = 8
PEER_NKEYS = 128
PEER_EXPERTS = PEER_NKEYS * PEER_NKEYS
PEER_DKEY = 256
PEER_TOPK = 16

LANES = 128
ROW_TILE = 256
FLAT_TILE = 512
VMEM_LIMIT = 56 * 1024 * 1024
NEG_INF = float("-inf")
LOG2E = math.log2(math.e)


def _cparams(sem):
    return pltpu.CompilerParams(dimension_semantics=sem, vmem_limit_bytes=VMEM_LIMIT)


def _pick(n, cands):
    for c in cands:
        if n % c == 0:
            return c
    raise ValueError(f"no tile for {n} in {cands}")


def _ada_kernel(cb_ref, w_ref, b_ref, o_ref, *, rows, tn):
    outs = []
    for r in range(rows):
        a = cb_ref[r]
        a = a * jax.nn.sigmoid(a)
        cols = [jnp.sum(w_ref[:, j * LANES:(j + 1) * LANES] * a, axis=0, keepdims=True)
                for j in range(tn // LANES)]
        outs.append(jnp.concatenate(cols, axis=1) + b_ref[...])
    outs.append(jnp.zeros((8 - rows, tn), F32))
    o_ref[...] = jnp.concatenate(outs, axis=0)


def ada_modulation(cond, w_ada, b_ada):
    rows, d = cond.shape
    depth, _, n = w_ada.shape
    tn = 512
    cb = jnp.broadcast_to(cond[:, :, None], (rows, d, LANES))
    return pl.pallas_call(
        functools.partial(_ada_kernel, rows=rows, tn=tn),
        grid=(depth, n // tn),
        in_specs=[pl.BlockSpec((rows, d, LANES), lambda l, j: (0, 0, 0)),
                  pl.BlockSpec((None, d, tn), lambda l, j: (l, 0, j)),
                  pl.BlockSpec((None, 1, tn), lambda l, j: (l, 0, j))],
        out_specs=pl.BlockSpec((None, 8, tn), lambda l, j: (l, 0, j)),
        out_shape=jax.ShapeDtypeStruct((depth, 8, n), F32),
        compiler_params=_cparams(("arbitrary", "arbitrary")),
        name="ada_modulation",
    )(cb, w_ada, b_ada.reshape(depth, 1, n))


def _norm_mod_kernel(*refs, has_delta, gate_row, shift_row, scale_row):
    if has_delta:
        x_ref, d_ref, mg_ref, ms_ref, g_ref, xo_ref, h_ref = refs
        x = x_ref[...] + mg_ref[gate_row:gate_row + 1, :] * d_ref[...].astype(F32)
        xo_ref[...] = x
    else:
        x_ref, ms_ref, g_ref, h_ref = refs
        x = x_ref[...]
    y = x * lax.rsqrt(jnp.mean(x * x, axis=-1, keepdims=True) + EPS) * g_ref[...]
    h = y * (1.0 + ms_ref[scale_row:scale_row + 1, :]) + ms_ref[shift_row:shift_row + 1, :]
    h_ref[...] = h.astype(BF16)


def norm_modulate(x, g, mod_ss, shift_row, scale_row, nctx, delta=None, mod_gate=None, gate_row=None):
    B, T, D = x.shape
    tm = ROW_TILE
    row_spec = pl.BlockSpec((None, tm, D), lambda b, i: (b, i, 0))
    mod_spec = pl.BlockSpec((None, 8, D), lambda b, i: (jnp.where(i < nctx, B, b), 0, 0))
    g_spec = pl.BlockSpec((1, D), lambda b, i: (0, 0))
    h_shape = jax.ShapeDtypeStruct((B, T, D), BF16)
    kern = functools.partial(_norm_mod_kernel, has_delta=delta is not None, gate_row=gate_row,
                             shift_row=shift_row, scale_row=scale_row)
    if delta is None:
        return pl.pallas_call(
            kern, grid=(B, T // tm), in_specs=[row_spec, mod_spec, g_spec], out_specs=row_spec,
            out_shape=h_shape, compiler_params=_cparams(("parallel", "parallel")), name="norm_modulate",
        )(x, mod_ss, g.reshape(1, D))
    return pl.pallas_call(
        kern, grid=(B, T // tm), in_specs=[row_spec, row_spec, mod_spec, mod_spec, g_spec],
        out_specs=[row_spec, row_spec],
        out_shape=[jax.ShapeDtypeStruct((B, T, D), F32), h_shape],
        compiler_params=_cparams(("parallel", "parallel")), name="residual_norm_modulate",
    )(x, delta, mod_gate, mod_ss, g.reshape(1, D))


def _mm_kernel(x_ref, w_ref, o_ref):
    o_ref[...] = jnp.dot(x_ref[...], w_ref[...], preferred_element_type=F32).astype(o_ref.dtype)


def matmul(x, w, out_dtype, name):
    M, K = x.shape
    N = w.shape[1]
    tm = _pick(M, (FLAT_TILE, 256, 128))
    tn = _pick(N, (1024, 768, 512, 256, 128))
    return pl.pallas_call(
        _mm_kernel, grid=(N // tn, M // tm),
        in_specs=[pl.BlockSpec((tm, K), lambda j, i: (i, 0)),
                  pl.BlockSpec((K, tn), lambda j, i: (0, j))],
        out_specs=pl.BlockSpec((tm, tn), lambda j, i: (i, j)),
        out_shape=jax.ShapeDtypeStruct((M, N), out_dtype),
        compiler_params=_cparams(("parallel", "parallel")), name=name,
    )(x, w)


def _rope(v, cos, sin, lane):
    r = jnp.where((lane % 64) < 32, pltpu.roll(v, 96, 1), pltpu.roll(v, 32, 1))
    return v * cos + r * sin


def _proj_kernel(zm_ref, zd_ref, cos_ref, sin_ref, gq_ref, gkv_ref, wuq_ref, wukv_ref,
                 qm_ref, km_ref, vm_ref, qd_ref, kd_ref, vd_ref, *, mla_scale, diff_scale):
    tm = zm_ref.shape[0]
    cos = cos_ref[...]
    sin = sin_ref[...]
    lane = lax.broadcasted_iota(jnp.int32, (tm, LANES), 1)
    ones_col = jnp.where(lane == 0, 1.0, 0.0).astype(BF16)

    def rms(v, g):
        return v * lax.rsqrt(jnp.mean(v * v, axis=-1, keepdims=True) + EPS) * g

    zm = zm_ref[...].astype(F32)
    qn = rms(zm[:, :MLA_Q_RANK], gq_ref[...]).astype(BF16)
    kvn = rms(zm[:, MLA_Q_RANK:MLA_Q_RANK + MLA_KV_RANK], gkv_ref[...]).astype(BF16)
    k_rope = _rope(zm[:, MLA_Q_RANK + MLA_KV_RANK:], cos, sin, lane).astype(BF16)
    q = jnp.dot(qn, wuq_ref[...], preferred_element_type=F32)
    kv = jnp.dot(kvn, wukv_ref[...], preferred_element_type=F32)
    for h in range(MLA_HEADS):
        qm_ref[h, :, 0:LANES] = (q[:, 256 * h:256 * h + LANES] * mla_scale).astype(BF16)
        qm_ref[h, :, LANES:2 * LANES] = (
            _rope(q[:, 256 * h + LANES:256 * h + 2 * LANES], cos, sin, lane) * mla_scale).astype(BF16)
        km_ref[h, :, 0:LANES] = kv[:, LANES * h:LANES * (h + 1)].astype(BF16)
        km_ref[h, :, LANES:2 * LANES] = k_rope
        vm_ref[h, :, 0:LANES] = kv[:, 512 + LANES * h:512 + LANES * (h + 1)].astype(BF16)
        vm_ref[h, :, LANES:2 * LANES] = ones_col
    zd = zd_ref[...].astype(F32)
    for h in range(DIFF_HEADS):
        qd_ref[h] = (_rope(zd[:, LANES * h:LANES * (h + 1)], cos, sin, lane) * diff_scale).astype(BF16)
        kd_ref[h] = _rope(zd[:, 512 + LANES * h:512 + LANES * (h + 1)], cos, sin, lane).astype(BF16)
        vd_ref[h, :, 0:LANES] = zd_ref[:, 1024 + LANES * h:1024 + LANES * (h + 1)]
        vd_ref[h, :, LANES:2 * LANES] = ones_col


def attention_projections(z_mla, z_diff, cos_t, sin_t, g_q, g_kv, w_uq_p, w_ukv_p):
    B, T, _ = z_mla.shape
    tm = ROW_TILE
    H = MLA_HEADS

    def head_spec(w):
        return pl.BlockSpec((None, H, tm, w), lambda b, i: (b, 0, i, 0))

    def head_shape(w):
        return jax.ShapeDtypeStruct((B, H, T, w), BF16)

    return pl.pallas_call(
        functools.partial(_proj_kernel, mla_scale=float((MLA_NOPE + MLA_ROPE) ** -0.5 * LOG2E),
                          diff_scale=float(DIFF_QK ** -0.5 * LOG2E)),
        grid=(B, T // tm),
        in_specs=[pl.BlockSpec((None, tm, z_mla.shape[2]), lambda b, i: (b, i, 0)),
                  pl.BlockSpec((None, tm, z_diff.shape[2]), lambda b, i: (b, i, 0)),
                  pl.BlockSpec((tm, LANES), lambda b, i: (i, 0)),
                  pl.BlockSpec((tm, LANES), lambda b, i: (i, 0)),
                  pl.BlockSpec((1, MLA_Q_RANK), lambda b, i: (0, 0)),
                  pl.BlockSpec((1, MLA_KV_RANK), lambda b, i: (0, 0)),
                  pl.BlockSpec(w_uq_p.shape, lambda b, i: (0, 0)),
                  pl.BlockSpec(w_ukv_p.shape, lambda b, i: (0, 0))],
        out_specs=[head_spec(256), head_spec(256), head_spec(256), head_spec(128), head_spec(128), head_spec(256)],
        out_shape=[head_shape(256), head_shape(256), head_shape(256), head_shape(128), head_shape(128),
                   head_shape(256)],
        compiler_params=_cparams(("parallel", "parallel")), name="attention_projections",
    )(z_mla, z_diff, cos_t, sin_t, g_q.reshape(1, -1), g_kv.reshape(1, -1), w_uq_p, w_ukv_p)


V_EXT = 2 * LANES


def _att_scores(q, k, mask):
    s = lax.dot_general(q, k, (((1,), (1,)), ((), ())), preferred_element_type=F32)
    return s if mask is None else jnp.where(mask, NEG_INF, s)


def _lane_tile(x, width):
    return x if width == LANES else jnp.concatenate([x] * (width // LANES), axis=1)


def _att_update(s, v_ext, m_ref, acc_ref, idx):
    m_prev = m_ref[idx]
    m_new = jnp.maximum(m_prev, jnp.max(s, axis=-1, keepdims=True))
    p = jnp.exp2(s - _lane_tile(m_new, s.shape[1])).astype(BF16)
    alpha = _lane_tile(jnp.exp2(m_prev - m_new), V_EXT)
    acc_ref[idx] = alpha * acc_ref[idx] + jnp.dot(p, v_ext, preferred_element_type=F32)
    m_ref[idx] = m_new


def _att_init(m_ref, acc_ref):
    m_ref[...] = jnp.full(m_ref.shape, NEG_INF, F32)
    acc_ref[...] = jnp.zeros(acc_ref.shape, F32)


def _att_step(streams, mask, m_ref, acc_ref):
    for idx, (q, k, v) in enumerate(streams):
        _att_update(_att_scores(q(), k(), mask), v(), m_ref, acc_ref, idx)


def _att_sweep(step, qi, ki, tq, tk, ctx_len):
    @pl.when(qi == 0)
    def _():
        row = lax.broadcasted_iota(jnp.int32, (tq, tk), 0)
        col = ki * tk + lax.broadcasted_iota(jnp.int32, (tq, tk), 1)
        step((row < ctx_len) & (col >= ctx_len))

    @pl.when(qi != 0)
    def _():
        step(None)


def _att_result(acc):
    return acc[:, :LANES] / acc[:, LANES:LANES + 1]


def _mla_att_kernel(q_ref, k_ref, v_ref, o_ref, m_ref, acc_ref, *, tq, tk, ctx_len):
    qi = pl.program_id(1)
    ki = pl.program_id(2)

    @pl.when(ki == 0)
    def _():
        _att_init(m_ref, acc_ref)

    streams = [(lambda h=h: q_ref[h], lambda h=h: k_ref[h], lambda h=h: v_ref[h]) for h in range(MLA_HEADS)]
    _att_sweep(lambda mask: _att_step(streams, mask, m_ref, acc_ref), qi, ki, tq, tk, ctx_len)

    @pl.when(ki == pl.num_programs(2) - 1)
    def _():
        for h in range(MLA_HEADS):
            o_ref[:, h * MLA_V:(h + 1) * MLA_V] = _att_result(acc_ref[h]).astype(o_ref.dtype)


def _att_specs(H, tq, tk, dk):
    return [pl.BlockSpec((None, H, tq, dk), lambda b, i, j: (b, 0, i, 0)),
            pl.BlockSpec((None, H, tk, dk), lambda b, i, j: (b, 0, j, 0)),
            pl.BlockSpec((None, H, tk, V_EXT), lambda b, i, j: (b, 0, j, 0))]


def _att_scratch(nstreams, tq):
    return [pltpu.VMEM((nstreams, tq, LANES), F32), pltpu.VMEM((nstreams, tq, V_EXT), F32)]


def mla_attention(qm, km, vm, ctx_len):
    B, H, T, dk = qm.shape
    tq = tk = _pick(T, (768, 256))
    assert ctx_len <= tq
    return pl.pallas_call(
        functools.partial(_mla_att_kernel, tq=tq, tk=tk, ctx_len=ctx_len),
        grid=(B, T // tq, T // tk),
        in_specs=_att_specs(H, tq, tk, dk),
        out_specs=pl.BlockSpec((None, tq, H * MLA_V), lambda b, i, j: (b, i, 0)),
        out_shape=jax.ShapeDtypeStruct((B, T, H * MLA_V), BF16),
        scratch_shapes=_att_scratch(H, tq),
        compiler_params=_cparams(("parallel", "parallel", "arbitrary")), name="mla_attention",
    )(qm, km, vm)


def _diff_att_kernel(q_ref, k_ref, v_ref, lam_ref, g_ref, o_ref, m_ref, acc_ref, *, tq, tk, ctx_len, lam_init):
    qi = pl.program_id(1)
    ki = pl.program_id(2)

    @pl.when(ki == 0)
    def _():
        _att_init(m_ref, acc_ref)

    def component(h, c):
        q = q_ref[h]
        first = lax.broadcasted_iota(jnp.int32, q.shape, 1) < DIFF_QK
        return jnp.where(first if c == 0 else jnp.logical_not(first), q, jnp.zeros_like(q))

    streams = [(lambda h=h, c=c: component(h, c), lambda h=h: k_ref[h], lambda h=h: v_ref[h])
               for h in range(DIFF_HEADS) for c in range(2)]
    _att_sweep(lambda mask: _att_step(streams, mask, m_ref, acc_ref), qi, ki, tq, tk, ctx_len)

    @pl.when(ki == pl.num_programs(2) - 1)
    def _():
        lv = lam_ref[...]
        lam = (jnp.exp(jnp.sum(lv[0:1] * lv[1:2], axis=-1, keepdims=True))
               - jnp.exp(jnp.sum(lv[2:3] * lv[3:4], axis=-1, keepdims=True)) + lam_init)
        for h in range(DIFF_HEADS):
            o = _att_result(acc_ref[2 * h]) - lam * _att_result(acc_ref[2 * h + 1])
            o = o * lax.rsqrt(jnp.mean(o * o, axis=-1, keepdims=True) + EPS) * g_ref[...]
            o_ref[:, h * DIFF_V:(h + 1) * DIFF_V] = (o * (1.0 - lam_init)).astype(o_ref.dtype)


def diff_attention(qd, kd, vd, lam_vecs, g_subln, lam_init, ctx_len):
    B, H, T, dk = qd.shape
    tq = tk = _pick(T, (768, 256))
    assert ctx_len <= tq
    return pl.pallas_call(
        functools.partial(_diff_att_kernel, tq=tq, tk=tk, ctx_len=ctx_len, lam_init=lam_init),
        grid=(B, T // tq, T // tk),
        in_specs=_att_specs(H, tq, tk, dk) + [pl.BlockSpec(lam_vecs.shape, lambda b, i, j: (0, 0)),
                                               pl.BlockSpec((1, DIFF_V), lambda b, i, j: (0, 0))],
        out_specs=pl.BlockSpec((None, tq, H * DIFF_V), lambda b, i, j: (b, i, 0)),
        out_shape=jax.ShapeDtypeStruct((B, T, H * DIFF_V), BF16),
        scratch_shapes=_att_scratch(2 * H, tq),
        compiler_params=_cparams(("parallel", "parallel", "arbitrary")), name="diff_attention",
    )(qd, kd, vd, lam_vecs, g_subln.reshape(1, -1))


def _dft_mats(n):
    k = np.arange(n)
    ang = 2.0 * np.pi * ((k[:, None] * k[None, :]) % n) / n
    return jnp.asarray(np.cos(ang), BF16), jnp.asarray(np.sin(ang), BF16)


def _channel_dft(x, cc, sc):
    tr, ti = [], []
    for s in range(x.shape[1] // LANES):
        xs = x[:, s * LANES:(s + 1) * LANES]
        tr.append(jnp.dot(xs, cc, preferred_element_type=F32))
        ti.append(-jnp.dot(xs, sc, preferred_element_type=F32))
    return jnp.concatenate(tr, axis=1), jnp.concatenate(ti, axis=1)


def _fnet_stage1_kernel(x_ref, cc_ref, sc_ref, c_ref, s_ref, twc_ref, tws_ref, yr_ref, yi_ref, *, nb):
    tr, ti = _channel_dft(x_ref[...], cc_ref[...], sc_ref[...])
    trb, tib = tr.astype(BF16), ti.astype(BF16)
    c, s = c_ref[...], s_ref[...]
    yr = jnp.dot(c, trb, preferred_element_type=F32) + jnp.dot(s, tib, preferred_element_type=F32)
    yi = jnp.dot(c, tib, preferred_element_type=F32) - jnp.dot(s, trb, preferred_element_type=F32)
    for i in range(nb):
        twc = jnp.concatenate([twc_ref[i]] * FNET_GROUPS, axis=1)
        tws = jnp.concatenate([tws_ref[i]] * FNET_GROUPS, axis=1)
        a = yr[:, i * FNET_WIDTH:(i + 1) * FNET_WIDTH]
        b = yi[:, i * FNET_WIDTH:(i + 1) * FNET_WIDTH]
        yr_ref[i] = a * twc + b * tws
        yi_ref[i] = b * twc - a * tws


def _fnet_stage2_kernel(yr_ref, yi_ref, c_ref, s_ref, o_ref, *, scale):
    xr = (jnp.dot(c_ref[...], yr_ref[...].astype(BF16), preferred_element_type=F32)
          + jnp.dot(s_ref[...], yi_ref[...].astype(BF16), preferred_element_type=F32))
    o_ref[...] = (xr * scale).astype(o_ref.dtype)


def _fnet_direct_kernel(x_ref, cc_ref, sc_ref, c_ref, s_ref, o_ref, *, scale):
    tr, ti = _channel_dft(x_ref[...], cc_ref[...], sc_ref[...])
    xr = (jnp.dot(c_ref[...], tr.astype(BF16), preferred_element_type=F32)
          + jnp.dot(s_ref[...], ti.astype(BF16), preferred_element_type=F32))
    o_ref[...] = (xr * scale).astype(o_ref.dtype)


def _full(a):
    return pl.BlockSpec(a.shape, lambda *_: (0,) * a.ndim)


def fourier_mix_long(zf):
    B, L, W = zf.shape
    n2 = LANES
    n1 = L // n2
    nb = _pick(n1, (8,))
    cc, sc = _dft_mats(FNET_GROUP_DIM)
    c2, s2 = _dft_mats(n2)
    c1, s1 = _dft_mats(n1)
    ang = 2.0 * np.pi * (np.arange(n1)[:, None] * np.arange(n2)[None, :]) / L
    twc = jnp.asarray(np.broadcast_to(np.cos(ang)[:, :, None], (n1, n2, LANES)), F32)
    tws = jnp.asarray(np.broadcast_to(np.sin(ang)[:, :, None], (n1, n2, LANES)), F32)
    x = zf.reshape(B, n2, n1 * W)
    y_shape = jax.ShapeDtypeStruct((B, n1, n2, W), F32)
    y_spec = pl.BlockSpec((None, nb, n2, W), lambda b, j: (b, j, 0, 0))
    tw_spec = pl.BlockSpec((nb, n2, LANES), lambda b, j: (j, 0, 0))
    yr, yi = pl.pallas_call(
        functools.partial(_fnet_stage1_kernel, nb=nb), grid=(B, n1 // nb),
        in_specs=[pl.BlockSpec((None, n2, nb * W), lambda b, j: (b, 0, j)),
                  _full(cc), _full(sc), _full(c2), _full(s2), tw_spec, tw_spec],
        out_specs=[y_spec, y_spec], out_shape=[y_shape, y_shape],
        compiler_params=_cparams(("parallel", "parallel")), name="fnet_stage1",
    )(x, cc, sc, c2, s2, twc, tws)
    cols = n2 * W
    tn = 4096
    y2_spec = pl.BlockSpec((None, n1, tn), lambda b, j: (b, 0, j))
    out = pl.pallas_call(
        functools.partial(_fnet_stage2_kernel, scale=float((L * FNET_GROUP_DIM) ** -0.5)),
        grid=(B, cols // tn),
        in_specs=[y2_spec, y2_spec, _full(c1), _full(s1)],
        out_specs=y2_spec, out_shape=jax.ShapeDtypeStruct((B, n1, cols), BF16),
        compiler_params=_cparams(("parallel", "parallel")), name="fnet_stage2",
    )(yr.reshape(B, n1, cols), yi.reshape(B, n1, cols), c1, s1)
    return out.reshape(B, L, W)


def fourier_mix_short(zf):
    B, L, W = zf.shape
    cc, sc = _dft_mats(FNET_GROUP_DIM)
    c, s = _dft_mats(L)
    spec = pl.BlockSpec((None, L, W), lambda b: (b, 0, 0))
    return pl.pallas_call(
        functools.partial(_fnet_direct_kernel, scale=float((L * FNET_GROUP_DIM) ** -0.5)), grid=(B,),
        in_specs=[spec, _full(cc), _full(sc), _full(c), _full(s)],
        out_specs=spec, out_shape=jax.ShapeDtypeStruct((B, L, W), BF16),
        compiler_params=_cparams(("parallel",)), name="fnet_direct",
    )(zf, cc, sc, c, s)


HALO = 16


def _conv_kernel(prev_ref, cur_ref, next_ref, w_ref, bdw_ref, g_ref, b_ref, o_ref, ext_ref, *, nctx):
    i = pl.program_id(1)
    nt = pl.num_programs(1)
    tm = cur_ref.shape[0]

    def glu(z):
        z = z.astype(F32)
        return z[:, :CONV_WIDTH] * jax.nn.sigmoid(z[:, CONV_WIDTH:])

    has_prev = jnp.logical_and(i != 0, i != nctx)
    has_next = jnp.logical_and(i != nctx - 1, i != nt - 1)
    ext_ref[0:HALO, :] = jnp.where(has_prev, glu(prev_ref[tm - HALO:tm, :]), 0.0)
    ext_ref[HALO:HALO + tm, :] = glu(cur_ref[...])
    ext_ref[HALO + tm:2 * HALO + tm, :] = jnp.where(has_next, glu(next_ref[0:HALO, :]), 0.0)
    pad = CONV_KERNEL // 2
    acc = jnp.zeros((tm, CONV_WIDTH), F32)
    for k in range(CONV_KERNEL):
        acc = acc + w_ref[k:k + 1, :] * ext_ref[pl.ds(HALO - pad + k, tm), :]
    y = acc + bdw_ref[...]
    gw = CONV_WIDTH // CONV_GROUPS
    outs = []
    for gi in range(CONV_GROUPS):
        yg = y[:, gi * gw:(gi + 1) * gw]
        mu = jnp.mean(yg, axis=-1, keepdims=True)
        var = jnp.mean(jnp.square(yg - mu), axis=-1, keepdims=True)
        outs.append((yg - mu) * lax.rsqrt(var + 1e-5))
    yn = jnp.concatenate(outs, axis=1) * g_ref[...] + b_ref[...]
    o_ref[...] = (yn * jax.nn.sigmoid(yn)).astype(o_ref.dtype)


def conformer_conv(zc, w_dw, b_dw, g, b, nctx):
    B, T, W2 = zc.shape
    tm = ROW_TILE
    nt = T // tm
    w_pad = jnp.concatenate([w_dw, jnp.zeros((32 - CONV_KERNEL, CONV_WIDTH), F32)], axis=0)
    vec = pl.BlockSpec((1, CONV_WIDTH), lambda bb, i: (0, 0))
    return pl.pallas_call(
        functools.partial(_conv_kernel, nctx=nctx), grid=(B, nt),
        in_specs=[pl.BlockSpec((None, tm, W2), lambda bb, i: (bb, jnp.maximum(i - 1, 0), 0)),
                  pl.BlockSpec((None, tm, W2), lambda bb, i: (bb, i, 0)),
                  pl.BlockSpec((None, tm, W2), lambda bb, i: (bb, jnp.minimum(i + 1, nt - 1), 0)),
                  pl.BlockSpec((32, CONV_WIDTH), lambda bb, i: (0, 0)), vec, vec, vec],
        out_specs=pl.BlockSpec((None, tm, CONV_WIDTH), lambda bb, i: (bb, i, 0)),
        out_shape=jax.ShapeDtypeStruct((B, T, CONV_WIDTH), BF16),
        scratch_shapes=[pltpu.VMEM((tm + 2 * HALO, CONV_WIDTH), F32)],
        compiler_params=_cparams(("parallel", "parallel")), name="conformer_conv",
    )(zc, zc, zc, w_pad, b_dw.reshape(1, -1), g.reshape(1, -1), b.reshape(1, -1))


def _merge_kernel(h_ref, b0_ref, b1_ref, b2_ref, b3_ref, wg_ref, wb_ref, o_ref):
    h = h_ref[...]
    acc = None
    for n, br in enumerate((b0_ref, b1_ref, b2_ref, b3_ref)):
        gate = jax.nn.sigmoid(jnp.dot(h, wg_ref[n], preferred_element_type=F32))
        term = gate * jnp.dot(br[...], wb_ref[n], preferred_element_type=F32)
        acc = term if acc is None else acc + term
    o_ref[...] = acc.astype(o_ref.dtype)


def gated_merge(h, branches, w_gate, w_branch):
    M, D = h.shape
    N = w_gate.shape[2]
    tm = _pick(M, (FLAT_TILE, 256))
    tn = _pick(N, (512, 256))
    br_spec = pl.BlockSpec((tm, BRANCH_W), lambda j, i: (i, 0))
    return pl.pallas_call(
        _merge_kernel, grid=(N // tn, M // tm),
        in_specs=[pl.BlockSpec((tm, D), lambda j, i: (i, 0)), br_spec, br_spec, br_spec, br_spec,
                  pl.BlockSpec((N_BRANCH, D, tn), lambda j, i: (0, 0, j)),
                  pl.BlockSpec((N_BRANCH, BRANCH_W, tn), lambda j, i: (0, 0, j))],
        out_specs=pl.BlockSpec((tm, tn), lambda j, i: (i, j)),
        out_shape=jax.ShapeDtypeStruct((M, N), BF16),
        compiler_params=_cparams(("parallel", "parallel")), name="gated_merge",
    )(h, *branches, w_gate, w_branch)


def _peer_scores_kernel(q_ref, keys_ref, o_ref):
    for hp in range(2 * PEER_HEADS):
        o_ref[hp] = lax.dot_general(keys_ref[hp % 2], q_ref[:, hp * LANES:(hp + 1) * LANES],
                                    (((1,), (1,)), ((), ())), preferred_element_type=F32)


def peer_scores(q, keys):
    M = q.shape[0]
    tm = _pick(M, (FLAT_TILE, 256))
    nhp = 2 * PEER_HEADS
    return pl.pallas_call(
        _peer_scores_kernel, grid=(M // tm,),
        in_specs=[pl.BlockSpec((tm, q.shape[1]), lambda i: (i, 0)), _full(keys)],
        out_specs=pl.BlockSpec((nhp, PEER_NKEYS, tm), lambda i: (0, 0, i)),
        out_shape=jax.ShapeDtypeStruct((nhp, PEER_NKEYS, M), F32),
        compiler_params=_cparams(("parallel",)), name="peer_scores",
    )(q, keys)


_NRANK = PEER_TOPK + 1
_STAIRCASE = [(a, b) for a in range(_NRANK) for b in range(_NRANK) if (a + 1) * (b + 1) <= _NRANK]
_NCAND = -(-len(_STAIRCASE) // 8) * 8
_NRANK_PAD = -(-_NRANK // 8) * 8


def _peer_threshold_kernel(s_ref, e1_ref, cut_ref, e2_ref, t1_ref, t2_ref, c_ref):
    def extract(vals, dst_ref):
        v = vals
        m = prev = None
        for r in range(_NRANK):
            prev = m
            m = jnp.max(v, axis=0, keepdims=True)
            if dst_ref is not None:
                dst_ref[r:r + 1, :] = m
            v = jnp.where(v == m, NEG_INF, v)
        return prev, m

    for h in range(PEER_HEADS):
        u1 = s_ref[2 * h] * LOG2E
        u2 = s_ref[2 * h + 1] * LOG2E
        extract(u1, t1_ref)
        extract(u2, t2_ref)
        m1 = t1_ref[0:1, :]
        m2 = t2_ref[0:1, :]
        c_ref[...] = jnp.full(c_ref.shape, NEG_INF, F32)
        for r, (a, b) in enumerate(_STAIRCASE):
            c_ref[r:r + 1, :] = (t1_ref[a:a + 1, :] - m1) + (t2_ref[b:b + 1, :] - m2)
        cand = c_ref[...]
        c16, c17 = extract(cand, None)
        thr = 0.5 * (c16 + c17)
        z = jnp.sum(jnp.where(cand >= thr, jnp.exp2(cand), 0.0), axis=0, keepdims=True)
        lz = -jnp.log2(z)
        s1f = (u1 - m1) + lz
        e1_ref[h] = jnp.exp2(s1f)
        cut_ref[h] = jnp.exp2((thr + lz) - s1f)
        e2_ref[h] = jnp.exp2(u2 - m2)


def peer_thresholds(scores):
    nhp, nk, M = scores.shape
    tl = _pick(M, (256, 128))
    sf_spec = pl.BlockSpec((PEER_HEADS, nk, tl), lambda i: (0, 0, i))
    sf_shape = jax.ShapeDtypeStruct((PEER_HEADS, nk, M), F32)
    return pl.pallas_call(
        _peer_threshold_kernel, grid=(M // tl,),
        in_specs=[pl.BlockSpec((nhp, nk, tl), lambda i: (0, 0, i))],
        out_specs=[sf_spec, sf_spec, sf_spec], out_shape=[sf_shape, sf_shape, sf_shape],
        scratch_shapes=[pltpu.VMEM((_NRANK_PAD, tl), F32), pltpu.VMEM((_NRANK_PAD, tl), F32),
                        pltpu.VMEM((_NCAND, tl), F32)],
        compiler_params=_cparams(("parallel",)), name="peer_thresholds",
    )(scores)


PEER_TOKEN_SPLIT = 256
PEER_EXPERT_SPLIT = 512
PEER_EXPERT_CHUNK = 1024
PEER_TOKEN_TILE = 768


def _peer_experts_kernel(x_ref, u_ref, vt_ref, e1_ref, cut_ref, e2_ref, o_ref, acc_ref, *scratch, ek):
    j = pl.program_id(1)
    tm = x_ref.shape[0]
    th, es = PEER_TOKEN_SPLIT, PEER_EXPERT_SPLIT
    pieces = [(c, t) for c in range(ek // es) for t in range(tm // th)]
    st_refs, a_refs = scratch[:len(pieces)], scratch[len(pieces):]

    @pl.when(j == 0)
    def _():
        acc_ref[...] = jnp.zeros(acc_ref.shape, F32)

    def pre_activations(p):
        c, t = pieces[p]
        st_refs[p][...] = lax.dot_general(u_ref[c * es:(c + 1) * es, :], x_ref[t * th:(t + 1) * th, :],
                                          (((1,), (1,)), ((), ())), preferred_element_type=F32)

    def weights(p):
        c, t = pieces[p]
        for r in range(es // PEER_NKEYS):
            i1 = j * (ek // PEER_NKEYS) + c * (es // PEER_NKEYS) + r
            rows = slice(r * PEER_NKEYS, (r + 1) * PEER_NKEYS)
            half = slice(t * th, (t + 1) * th)
            cut_rows = [cut_ref[h, pl.ds(i1, 1), half] for h in range(PEER_HEADS)]
            e1_rows = [e1_ref[h, pl.ds(i1, 1), half] for h in range(PEER_HEADS)]
            for g in range(th // LANES):
                lanes = slice(t * th + g * LANES, t * th + (g + 1) * LANES)
                sub = slice(g * LANES, (g + 1) * LANES)
                w = None
                for h in range(PEER_HEADS):
                    e2 = e2_ref[h, :, lanes]
                    sel = jnp.where(e2 >= cut_rows[h][:, sub], e2, 0.0)
                    sel = sel * e1_rows[h][:, sub]
                    w = sel if w is None else w + sel
                pre = st_refs[p][rows, g * LANES:(g + 1) * LANES]
                act = pre * (0.5 + 0.5 * lax.erf(pre * (2.0 ** -0.5)))
                a_refs[p][rows, g * LANES:(g + 1) * LANES] = (w * act).astype(BF16)

    def accumulate(p):
        c, t = pieces[p]
        lanes = slice(t * th, (t + 1) * th)
        acc_ref[:, lanes] += jnp.dot(vt_ref[:, c * es:(c + 1) * es], a_refs[p][...], preferred_element_type=F32)

    for p in range(len(pieces)):
        pre_activations(p)
    for p in range(len(pieces)):
        weights(p)
        accumulate(p)

    @pl.when(j == pl.num_programs(1) - 1)
    def _():
        o_ref[...] = acc_ref[...].T


def peer_experts(x, u, vt, e1, cut, e2):
    M, D = x.shape
    E = u.shape[0]
    tm = _pick(M, (PEER_TOKEN_TILE, FLAT_TILE, 256))
    ek = vt.shape[2]
    th, es = PEER_TOKEN_SPLIT, PEER_EXPERT_SPLIT
    npieces = (tm // th) * (ek // es)
    table_spec = pl.BlockSpec((PEER_HEADS, PEER_NKEYS, tm), lambda i, j: (0, 0, i), pipeline_mode=pl.Buffered(1))
    return pl.pallas_call(
        functools.partial(_peer_experts_kernel, ek=ek), grid=(M // tm, E // ek),
        in_specs=[pl.BlockSpec((tm, D), lambda i, j: (i, 0)),
                  pl.BlockSpec((ek, D), lambda i, j: (j, 0)),
                  pl.BlockSpec((None, D, ek), lambda i, j: (j, 0, 0)),
                  table_spec, table_spec, table_spec],
        out_specs=pl.BlockSpec((tm, D), lambda i, j: (i, 0)),
        out_shape=jax.ShapeDtypeStruct((M, D), F32),
        scratch_shapes=([pltpu.VMEM((D, tm), F32)]
                        + [pltpu.VMEM((es, th), F32)] * npieces + [pltpu.VMEM((es, th), BF16)] * npieces),
        compiler_params=_cparams(("parallel", "arbitrary")), name="peer_experts",
    )(x, u, vt, e1, cut, e2)


def _final_kernel(x_ref, d_ref, mg_ref, g_ref, o_ref, *, gate_row):
    x = x_ref[...] + mg_ref[gate_row:gate_row + 1, :] * d_ref[...]
    o_ref[...] = x * lax.rsqrt(jnp.mean(x * x, axis=-1, keepdims=True) + EPS) * g_ref[...]


def final_norm(x, delta, mod_gate, g, nctx, gate_row):
    B, T, D = x.shape
    tm = ROW_TILE
    L = T - nctx * tm
    row_spec = pl.BlockSpec((None, tm, D), lambda b, i: (b, i + nctx, 0))
    return pl.pallas_call(
        functools.partial(_final_kernel, gate_row=gate_row), grid=(B, L // tm),
        in_specs=[row_spec, row_spec, pl.BlockSpec((None, 8, D), lambda b, i: (b, 0, 0)),
                  pl.BlockSpec((1, D), lambda b, i: (0, 0))],
        out_specs=pl.BlockSpec((None, tm, D), lambda b, i: (b, i, 0)),
        out_shape=jax.ShapeDtypeStruct((B, L, D), F32),
        compiler_params=_cparams(("parallel", "parallel")), name="final_norm",
    )(x, delta, mod_gate, g.reshape(1, D))


def _rope_tables(L, ctx_len):
    rows = L // GRID_W
    row = jnp.repeat(jnp.arange(rows, dtype=F32), GRID_W)
    col = jnp.tile(jnp.arange(GRID_W, dtype=F32), rows)
    nf = MLA_ROPE // 4
    inv = ROPE_BASE ** (-jnp.arange(nf, dtype=F32) / nf)
    ang = jnp.concatenate([row[:, None] * inv, col[:, None] * inv], axis=-1)
    cos, sin = jnp.cos(ang), jnp.sin(ang)
    cos_t = jnp.concatenate([cos, cos, cos, cos], axis=-1)
    sin_t = jnp.concatenate([-sin, sin, -sin, sin], axis=-1)
    cos_t = jnp.concatenate([jnp.ones((ctx_len, LANES), F32), cos_t], axis=0)
    sin_t = jnp.concatenate([jnp.zeros((ctx_len, LANES), F32), sin_t], axis=0)
    return cos_t, sin_t


def kernel(x, c, ctx, c_ctx, w_ada, b_ada, g_norm1, g_norm2, w_in, g_q_mla, w_uq, g_kv_mla, w_ukv, lam_q1, lam_k1, lam_q2, lam_k2, g_subln, w_dw, b_dw, g_conv_norm, b_conv_norm, w_gate, w_branch, w_out, w_peer_q, peer_keys, peer_u, peer_v, g_final):
    B, L, D = x.shape
    ctx_len = ctx.shape[1]
    depth = w_ada.shape[0]
    T = ctx_len + L
    M = B * T
    assert MLA_ROPE == DIFF_QK and ctx_len % ROW_TILE == 0 and L % ROW_TILE == 0
    nctx = ctx_len // ROW_TILE

    cos_t, sin_t = _rope_tables(L, ctx_len)
    xs = jnp.concatenate([ctx, x], axis=1)

    mods = ada_modulation(jnp.concatenate([c, c_ctx[None, :]], axis=0), w_ada, b_ada)
    mods = mods.reshape(depth, 8, 6, D)[:, :B + 1]
    mods = jnp.concatenate([mods, jnp.zeros((depth, B + 1, 2, D), F32)], axis=2)

    o_c, o_f, o_d = 0, 2 * CONV_WIDTH, 2 * CONV_WIDTH + FNET_WIDTH
    pending = None
    for l in range(depth):
        lam_init = 0.8 - 0.6 * math.exp(-0.3 * l)
        s0, s1, s2 = MLA_Q_RANK, MLA_Q_RANK + MLA_KV_RANK, IN_MLA
        wi = w_in[l]
        w_mla = jnp.concatenate([wi[:, :s2], jnp.zeros((D, LANES - MLA_ROPE), F32)], axis=1).astype(BF16)
        w_diff = wi[:, s2:s2 + IN_DIFF].astype(BF16)
        w_fnet = wi[:, s2 + IN_DIFF:s2 + IN_DIFF + FNET_WIDTH].astype(BF16)
        w_conv = wi[:, s2 + IN_DIFF + FNET_WIDTH:].astype(BF16)
        hd = MLA_NOPE + MLA_ROPE
        w_uq_p = jnp.pad(w_uq[l].reshape(MLA_Q_RANK, MLA_HEADS, hd),
                         ((0, 0), (0, 0), (0, 256 - hd))).reshape(MLA_Q_RANK, MLA_HEADS * 256).astype(BF16)
        wkv = w_ukv[l].reshape(MLA_KV_RANK, MLA_HEADS, MLA_NOPE + MLA_V)
        w_ukv_p = jnp.concatenate([wkv[:, :, :MLA_NOPE].reshape(MLA_KV_RANK, -1),
                                   wkv[:, :, MLA_NOPE:].reshape(MLA_KV_RANK, -1)], axis=1).astype(BF16)
        lam_vecs = jnp.stack([lam_q1[l], lam_k1[l], lam_q2[l], lam_k2[l]], axis=0)

        if pending is None:
            h = norm_modulate(xs, g_norm1[l], mods[l], 0, 1, nctx)
        else:
            xs, h = norm_modulate(xs, g_norm1[l], mods[l], 0, 1, nctx,
                                  delta=pending[0], mod_gate=pending[1], gate_row=5)
        hf = h.reshape(M, D)
        z_mla = matmul(hf, w_mla, BF16, "in_proj_mla").reshape(B, T, -1)
        z_diff = matmul(hf, w_diff, BF16, "in_proj_diff").reshape(B, T, -1)
        z_fnet = matmul(hf, w_fnet, BF16, "in_proj_fnet").reshape(B, T, -1)
        z_conv = matmul(hf, w_conv, BF16, "in_proj_conv").reshape(B, T, -1)

        qm, km, vm, qd, kd, vd = attention_projections(z_mla, z_diff, cos_t, sin_t, g_q_mla[l], g_kv_mla[l],
                                                   w_uq_p, w_ukv_p)
        o_mla = mla_attention(qm, km, vm, ctx_len)
        o_diff = diff_attention(qd, kd, vd, lam_vecs, g_subln[l], lam_init, ctx_len)
        o_fnet = jnp.concatenate([fourier_mix_short(z_fnet[:, :ctx_len]), fourier_mix_long(z_fnet[:, ctx_len:])],
                                 axis=1)
        o_conv = conformer_conv(z_conv, w_dw[l], b_dw[l], g_conv_norm[l], b_conv_norm[l], nctx)

        y = gated_merge(hf, [o.reshape(M, BRANCH_W) for o in (o_mla, o_fnet, o_conv, o_diff)],
                        w_gate[l].astype(BF16), w_branch[l].astype(BF16))
        mix = matmul(y, w_out[l].astype(BF16), F32, "out_proj").reshape(B, T, D)
        xs, h2 = norm_modulate(xs, g_norm2[l], mods[l], 3, 4, nctx, delta=mix, mod_gate=mods[l], gate_row=2)

        h2f = h2.reshape(M, D)
        q = matmul(h2f, w_peer_q[l].astype(BF16), BF16, "peer_query")
        scores = peer_scores(q, peer_keys[l].astype(BF16))
        e1, cut, e2 = peer_thresholds(scores)
        peer_vt = jnp.swapaxes(peer_v[l].reshape(-1, PEER_EXPERT_CHUNK, D), 1, 2).astype(BF16)
        peer_out = peer_experts(h2f, peer_u[l].astype(BF16), peer_vt, e1, cut, e2)
        pending = (peer_out.reshape(B, T, D), mods[l])

    return final_norm(xs, pending[0], pending[1], g_final, nctx, 5)
```

```python
import functools
import math

import numpy as np
import jax
import jax.numpy as jnp
from jax import lax
from jax.experimental import pallas as pl
from jax.experimental.pallas import tpu as pltpu

F32 = jnp.float32
BF16 = jnp.bfloat16

GRID_W = 64
ROPE_BASE = 10000.0
EPS = 1e-6
MLA_HEADS = 4
MLA_Q_RANK = 384
MLA_KV_RANK = 256
MLA_NOPE = 128
MLA_ROPE = 64
MLA_V = 128
DIFF_HEADS = 4
DIFF_QK = 64
DIFF_V = 128
FNET_GROUPS = 4
FNET_GROUP_DIM = 128
FNET_WIDTH = 512
CONV_WIDTH = 512
CONV_KERNEL = 31
CONV_GROUPS = 4
N_BRANCH = 4
BRANCH_W = 512
IN_MLA = MLA_Q_RANK + MLA_KV_RANK + MLA_ROPE
IN_DIFF = DIFF_HEADS * (4 * DIFF_QK + DIFF_V)
PEER_HEADS = 8
PEER_NKEYS = 128
PEER_EXPERTS = PEER_NKEYS * PEER_NKEYS
PEER_DKEY = 256
PEER_TOPK = 16

LANES = 128
ROW_TILE = 256
FLAT_TILE = 512
VMEM_LIMIT = 56 * 1024 * 1024
NEG_INF = float("-inf")
LOG2E = math.log2(math.e)


def _cparams(sem):
    return pltpu.CompilerParams(dimension_semantics=sem, vmem_limit_bytes=VMEM_LIMIT)


def _pick(n, cands):
    for c in cands:
        if n % c == 0:
            return c
    raise ValueError(f"no tile for {n} in {cands}")


def _ada_kernel(cb_ref, w_ref, b_ref, o_ref, *, rows, tn):
    outs = []
    for r in range(rows):
        a = cb_ref[r]
        a = a * jax.nn.sigmoid(a)
        cols = [jnp.sum(w_ref[:, j * LANES:(j + 1) * LANES] * a, axis=0, keepdims=True)
                for j in range(tn // LANES)]
        outs.append(jnp.concatenate(cols, axis=1) + b_ref[...])
    outs.append(jnp.zeros((8 - rows, tn), F32))
    o_ref[...] = jnp.concatenate(outs, axis=0)


def ada_modulation(cond, w_ada, b_ada):
    rows, d = cond.shape
    depth, _, n = w_ada.shape
    tn = 512
    cb = jnp.broadcast_to(cond[:, :, None], (rows, d, LANES))
    return pl.pallas_call(
        functools.partial(_ada_kernel, rows=rows, tn=tn),
        grid=(depth, n // tn),
        in_specs=[pl.BlockSpec((rows, d, LANES), lambda l, j: (0, 0, 0)),
                  pl.BlockSpec((None, d, tn), lambda l, j: (l, 0, j)),
                  pl.BlockSpec((None, 1, tn), lambda l, j: (l, 0, j))],
        out_specs=pl.BlockSpec((None, 8, tn), lambda l, j: (l, 0, j)),
        out_shape=jax.ShapeDtypeStruct((depth, 8, n), F32),
        compiler_params=_cparams(("arbitrary", "arbitrary")),
        name="ada_modulation",
    )(cb, w_ada, b_ada.reshape(depth, 1, n))


def _norm_mod_kernel(*refs, has_delta, gate_row, shift_row, scale_row):
    if has_delta:
        x_ref, d_ref, mg_ref, ms_ref, g_ref, xo_ref, h_ref = refs
        x = x_ref[...] + mg_ref[gate_row:gate_row + 1, :] * d_ref[...].astype(F32)
        xo_ref[...] = x
    else:
        x_ref, ms_ref, g_ref, h_ref = refs
        x = x_ref[...]
    y = x * lax.rsqrt(jnp.mean(x * x, axis=-1, keepdims=True) + EPS) * g_ref[...]
    h = y * (1.0 + ms_ref[scale_row:scale_row + 1, :]) + ms_ref[shift_row:shift_row + 1, :]
    h_ref[...] = h.astype(BF16)


def norm_modulate(x, g, mod_ss, shift_row, scale_row, nctx, delta=None, mod_gate=None, gate_row=None):
    B, T, D = x.shape
    tm = ROW_TILE
    row_spec = pl.BlockSpec((None, tm, D), lambda b, i: (b, i, 0))
    mod_spec = pl.BlockSpec((None, 8, D), lambda b, i: (jnp.where(i < nctx, B, b), 0, 0))
    g_spec = pl.BlockSpec((1, D), lambda b, i: (0, 0))
    h_shape = jax.ShapeDtypeStruct((B, T, D), BF16)
    kern = functools.partial(_norm_mod_kernel, has_delta=delta is not None, gate_row=gate_row,
                             shift_row=shift_row, scale_row=scale_row)
    if delta is None:
        return pl.pallas_call(
            kern, grid=(B, T // tm), in_specs=[row_spec, mod_spec, g_spec], out_specs=row_spec,
            out_shape=h_shape, compiler_params=_cparams(("parallel", "parallel")), name="norm_modulate",
        )(x, mod_ss, g.reshape(1, D))
    return pl.pallas_call(
        kern, grid=(B, T // tm), in_specs=[row_spec, row_spec, mod_spec, mod_spec, g_spec],
        out_specs=[row_spec, row_spec],
        out_shape=[jax.ShapeDtypeStruct((B, T, D), F32), h_shape],
        compiler_params=_cparams(("parallel", "parallel")), name="residual_norm_modulate",
    )(x, delta, mod_gate, mod_ss, g.reshape(1, D))


def _mm_kernel(x_ref, w_ref, o_ref):
    o_ref[...] = jnp.dot(x_ref[...], w_ref[...], preferred_element_type=F32).astype(o_ref.dtype)


def matmul(x, w, out_dtype, name):
    M, K = x.shape
    N = w.shape[1]
    tm = _pick(M, (FLAT_TILE, 256, 128))
    tn = _pick(N, (1024, 768, 512, 256, 128))
    return pl.pallas_call(
        _mm_kernel, grid=(N // tn, M // tm),
        in_specs=[pl.BlockSpec((tm, K), lambda j, i: (i, 0)),
                  pl.BlockSpec((K, tn), lambda j, i: (0, j))],
        out_specs=pl.BlockSpec((tm, tn), lambda j, i: (i, j)),
        out_shape=jax.ShapeDtypeStruct((M, N), out_dtype),
        compiler_params=_cparams(("parallel", "parallel")), name=name,
    )(x, w)


def _rope(v, cos, sin, lane):
    r = jnp.where((lane % 64) < 32, pltpu.roll(v, 96, 1), pltpu.roll(v, 32, 1))
    return v * cos + r * sin


def _proj_kernel(zm_ref, zd_ref, cos_ref, sin_ref, gq_ref, gkv_ref, wuq_ref, wukv_ref,
                 qm_ref, km_ref, vm_ref, qd_ref, kd_ref, vd_ref, *, mla_scale, diff_scale):
    tm = zm_ref.shape[0]
    cos = cos_ref[...]
    sin = sin_ref[...]
    lane = lax.broadcasted_iota(jnp.int32, (tm, LANES), 1)
    ones_col = jnp.where(lane == 0, 1.0, 0.0).astype(BF16)

    def rms(v, g):
        return v * lax.rsqrt(jnp.mean(v * v, axis=-1, keepdims=True) + EPS) * g

    zm = zm_ref[...].astype(F32)
    qn = rms(zm[:, :MLA_Q_RANK], gq_ref[...]).astype(BF16)
    kvn = rms(zm[:, MLA_Q_RANK:MLA_Q_RANK + MLA_KV_RANK], gkv_ref[...]).astype(BF16)
    k_rope = _rope(zm[:, MLA_Q_RANK + MLA_KV_RANK:], cos, sin, lane).astype(BF16)
    q = jnp.dot(qn, wuq_ref[...], preferred_element_type=F32)
    kv = jnp.dot(kvn, wukv_ref[...], preferred_element_type=F32)
    for h in range(MLA_HEADS):
        qm_ref[h, :, 0:LANES] = (q[:, 256 * h:256 * h + LANES] * mla_scale).astype(BF16)
        qm_ref[h, :, LANES:2 * LANES] = (
            _rope(q[:, 256 * h + LANES:256 * h + 2 * LANES], cos, sin, lane) * mla_scale).astype(BF16)
        km_ref[h, :, 0:LANES] = kv[:, LANES * h:LANES * (h + 1)].astype(BF16)
        km_ref[h, :, LANES:2 * LANES] = k_rope
        vm_ref[h, :, 0:LANES] = kv[:, 512 + LANES * h:512 + LANES * (h + 1)].astype(BF16)
        vm_ref[h, :, LANES:2 * LANES] = ones_col
    zd = zd_ref[...].astype(F32)
    for h in range(DIFF_HEADS):
        qd_ref[h] = (_rope(zd[:, LANES * h:LANES * (h + 1)], cos, sin, lane) * diff_scale).astype(BF16)
        kd_ref[h] = _rope(zd[:, 512 + LANES * h:512 + LANES * (h + 1)], cos, sin, lane).astype(BF16)
        vd_ref[h, :, 0:LANES] = zd_ref[:, 1024 + LANES * h:1024 + LANES * (h + 1)]
        vd_ref[h, :, LANES:2 * LANES] = ones_col


def attention_projections(z_mla, z_diff, cos_t, sin_t, g_q, g_kv, w_uq_p, w_ukv_p):
    B, T, _ = z_mla.shape
    tm = ROW_TILE
    H = MLA_HEADS

    def head_spec(w):
        return pl.BlockSpec((None, H, tm, w), lambda b, i: (b, 0, i, 0))

    def head_shape(w):
        return jax.ShapeDtypeStruct((B, H, T, w), BF16)

    return pl.pallas_call(
        functools.partial(_proj_kernel, mla_scale=float((MLA_NOPE + MLA_ROPE) ** -0.5 * LOG2E),
                          diff_scale=float(DIFF_QK ** -0.5 * LOG2E)),
        grid=(B, T // tm),
        in_specs=[pl.BlockSpec((None, tm, z_mla.shape[2]), lambda b, i: (b, i, 0)),
                  pl.BlockSpec((None, tm, z_diff.shape[2]), lambda b, i: (b, i, 0)),
                  pl.BlockSpec((tm, LANES), lambda b, i: (i, 0)),
                  pl.BlockSpec((tm, LANES), lambda b, i: (i, 0)),
                  pl.BlockSpec((1, MLA_Q_RANK), lambda b, i: (0, 0)),
                  pl.BlockSpec((1, MLA_KV_RANK), lambda b, i: (0, 0)),
                  pl.BlockSpec(w_uq_p.shape, lambda b, i: (0, 0)),
                  pl.BlockSpec(w_ukv_p.shape, lambda b, i: (0, 0))],
        out_specs=[head_spec(256), head_spec(256), head_spec(256), head_spec(128), head_spec(128), head_spec(256)],
        out_shape=[head_shape(256), head_shape(256), head_shape(256), head_shape(128), head_shape(128),
                   head_shape(256)],
        compiler_params=_cparams(("parallel", "parallel")), name="attention_projections",
    )(z_mla, z_diff, cos_t, sin_t, g_q.reshape(1, -1), g_kv.reshape(1, -1), w_uq_p, w_ukv_p)


V_EXT = 2 * LANES


def _att_scores(q, k, mask):
    s = lax.dot_general(q, k, (((1,), (1,)), ((), ())), preferred_element_type=F32)
    return s if mask is None else jnp.where(mask, NEG_INF, s)


def _lane_tile(x, width):
    return x if width == LANES else jnp.concatenate([x] * (width // LANES), axis=1)


def _att_update(s, v_ext, m_ref, acc_ref, idx):
    m_prev = m_ref[idx]
    m_new = jnp.maximum(m_prev, jnp.max(s, axis=-1, keepdims=True))
    p = jnp.exp2(s - _lane_tile(m_new, s.shape[1])).astype(BF16)
    alpha = _lane_tile(jnp.exp2(m_prev - m_new), V_EXT)
    acc_ref[idx] = alpha * acc_ref[idx] + jnp.dot(p, v_ext, preferred_element_type=F32)
    m_ref[idx] = m_new


def _att_init(m_ref, acc_ref):
    m_ref[...] = jnp.full(m_ref.shape, NEG_INF, F32)
    acc_ref[...] = jnp.zeros(acc_ref.shape, F32)


def _att_step(streams, mask, m_ref, acc_ref):
    for idx, (q, k, v) in enumerate(streams):
        _att_update(_att_scores(q(), k(), mask), v(), m_ref, acc_ref, idx)


def _att_sweep(step, qi, ki, tq, tk, ctx_len):
    @pl.when(qi == 0)
    def _():
        row = lax.broadcasted_iota(jnp.int32, (tq, tk), 0)
        col = ki * tk + lax.broadcasted_iota(jnp.int32, (tq, tk), 1)
        step((row < ctx_len) & (col >= ctx_len))

    @pl.when(qi != 0)
    def _():
        step(None)


def _att_result(acc):
    return acc[:, :LANES] / acc[:, LANES:LANES + 1]


def _mla_att_kernel(q_ref, k_ref, v_ref, o_ref, m_ref, acc_ref, *, tq, tk, ctx_len):
    qi = pl.program_id(1)
    ki = pl.program_id(2)

    @pl.when(ki == 0)
    def _():
        _att_init(m_ref, acc_ref)

    streams = [(lambda h=h: q_ref[h], lambda h=h: k_ref[h], lambda h=h: v_ref[h]) for h in range(MLA_HEADS)]
    _att_sweep(lambda mask: _att_step(streams, mask, m_ref, acc_ref), qi, ki, tq, tk, ctx_len)

    @pl.when(ki == pl.num_programs(2) - 1)
    def _():
        for h in range(MLA_HEADS):
            o_ref[:, h * MLA_V:(h + 1) * MLA_V] = _att_result(acc_ref[h]).astype(o_ref.dtype)


def _att_specs(H, tq, tk, dk):
    return [pl.BlockSpec((None, H, tq, dk), lambda b, i, j: (b, 0, i, 0)),
            pl.BlockSpec((None, H, tk, dk), lambda b, i, j: (b, 0, j, 0)),
            pl.BlockSpec((None, H, tk, V_EXT), lambda b, i, j: (b, 0, j, 0))]


def _att_scratch(nstreams, tq):
    return [pltpu.VMEM((nstreams, tq, LANES), F32), pltpu.VMEM((nstreams, tq, V_EXT), F32)]


def mla_attention(qm, km, vm, ctx_len):
    B, H, T, dk = qm.shape
    tq = tk = _pick(T, (768, 256))
    assert ctx_len <= tq
    return pl.pallas_call(
        functools.partial(_mla_att_kernel, tq=tq, tk=tk, ctx_len=ctx_len),
        grid=(B, T // tq, T // tk),
        in_specs=_att_specs(H, tq, tk, dk),
        out_specs=pl.BlockSpec((None, tq, H * MLA_V), lambda b, i, j: (b, i, 0)),
        out_shape=jax.ShapeDtypeStruct((B, T, H * MLA_V), BF16),
        scratch_shapes=_att_scratch(H, tq),
        compiler_params=_cparams(("parallel", "parallel", "arbitrary")), name="mla_attention",
    )(qm, km, vm)


def _diff_att_kernel(q_ref, k_ref, v_ref, lam_ref, g_ref, o_ref, m_ref, acc_ref, *, tq, tk, ctx_len, lam_init):
    qi = pl.program_id(1)
    ki = pl.program_id(2)

    @pl.when(ki == 0)
    def _():
        _att_init(m_ref, acc_ref)

    def component(h, c):
        q = q_ref[h]
        first = lax.broadcasted_iota(jnp.int32, q.shape, 1) < DIFF_QK
        return jnp.where(first if c == 0 else jnp.logical_not(first), q, jnp.zeros_like(q))

    streams = [(lambda h=h, c=c: component(h, c), lambda h=h: k_ref[h], lambda h=h: v_ref[h])
               for h in range(DIFF_HEADS) for c in range(2)]
    _att_sweep(lambda mask: _att_step(streams, mask, m_ref, acc_ref), qi, ki, tq, tk, ctx_len)

    @pl.when(ki == pl.num_programs(2) - 1)
    def _():
        lv = lam_ref[...]
        lam = (jnp.exp(jnp.sum(lv[0:1] * lv[1:2], axis=-1, keepdims=True))
               - jnp.exp(jnp.sum(lv[2:3] * lv[3:4], axis=-1, keepdims=True)) + lam_init)
        for h in range(DIFF_HEADS):
            o = _att_result(acc_ref[2 * h]) - lam * _att_result(acc_ref[2 * h + 1])
            o = o * lax.rsqrt(jnp.mean(o * o, axis=-1, keepdims=True) + EPS) * g_ref[...]
            o_ref[:, h * DIFF_V:(h + 1) * DIFF_V] = (o * (1.0 - lam_init)).astype(o_ref.dtype)


def diff_attention(qd, kd, vd, lam_vecs, g_subln, lam_init, ctx_len):
    B, H, T, dk = qd.shape
    tq = tk = _pick(T, (768, 256))
    assert ctx_len <= tq
    return pl.pallas_call(
        functools.partial(_diff_att_kernel, tq=tq, tk=tk, ctx_len=ctx_len, lam_init=lam_init),
        grid=(B, T // tq, T // tk),
        in_specs=_att_specs(H, tq, tk, dk) + [pl.BlockSpec(lam_vecs.shape, lambda b, i, j: (0, 0)),
                                               pl.BlockSpec((1, DIFF_V), lambda b, i, j: (0, 0))],
        out_specs=pl.BlockSpec((None, tq, H * DIFF_V), lambda b, i, j: (b, i, 0)),
        out_shape=jax.ShapeDtypeStruct((B, T, H * DIFF_V), BF16),
        scratch_shapes=_att_scratch(2 * H, tq),
        compiler_params=_cparams(("parallel", "parallel", "arbitrary")), name="diff_attention",
    )(qd, kd, vd, lam_vecs, g_subln.reshape(1, -1))


def _dft_mats(n):
    k = np.arange(n)
    ang = 2.0 * np.pi * ((k[:, None] * k[None, :]) % n) / n
    return jnp.asarray(np.cos(ang), BF16), jnp.asarray(np.sin(ang), BF16)


def _channel_dft(x, cc, sc):
    tr, ti = [], []
    for s in range(x.shape[1] // LANES):
        xs = x[:, s * LANES:(s + 1) * LANES]
        tr.append(jnp.dot(xs, cc, preferred_element_type=F32))
        ti.append(-jnp.dot(xs, sc, preferred_element_type=F32))
    return jnp.concatenate(tr, axis=1), jnp.concatenate(ti, axis=1)


def _fnet_stage1_kernel(x_ref, cc_ref, sc_ref, c_ref, s_ref, twc_ref, tws_ref, yr_ref, yi_ref, *, nb):
    tr, ti = _channel_dft(x_ref[...], cc_ref[...], sc_ref[...])
    trb, tib = tr.astype(BF16), ti.astype(BF16)
    c, s = c_ref[...], s_ref[...]
    yr = jnp.dot(c, trb, preferred_element_type=F32) + jnp.dot(s, tib, preferred_element_type=F32)
    yi = jnp.dot(c, tib, preferred_element_type=F32) - jnp.dot(s, trb, preferred_element_type=F32)
    for i in range(nb):
        twc = jnp.concatenate([twc_ref[i]] * FNET_GROUPS, axis=1)
        tws = jnp.concatenate([tws_ref[i]] * FNET_GROUPS, axis=1)
        a = yr[:, i * FNET_WIDTH:(i + 1) * FNET_WIDTH]
        b = yi[:, i * FNET_WIDTH:(i + 1) * FNET_WIDTH]
        yr_ref[i] = a * twc + b * tws
        yi_ref[i] = b * twc - a * tws


def _fnet_stage2_kernel(yr_ref, yi_ref, c_ref, s_ref, o_ref, *, scale):
    xr = (jnp.dot(c_ref[...], yr_ref[...].astype(BF16), preferred_element_type=F32)
          + jnp.dot(s_ref[...], yi_ref[...].astype(BF16), preferred_element_type=F32))
    o_ref[...] = (xr * scale).astype(o_ref.dtype)


def _fnet_direct_kernel(x_ref, cc_ref, sc_ref, c_ref, s_ref, o_ref, *, scale):
    tr, ti = _channel_dft(x_ref[...], cc_ref[...], sc_ref[...])
    xr = (jnp.dot(c_ref[...], tr.astype(BF16), preferred_element_type=F32)
          + jnp.dot(s_ref[...], ti.astype(BF16), preferred_element_type=F32))
    o_ref[...] = (xr * scale).astype(o_ref.dtype)


def _full(a):
    return pl.BlockSpec(a.shape, lambda *_: (0,) * a.ndim)


def fourier_mix_long(zf):
    B, L, W = zf.shape
    n2 = LANES
    n1 = L // n2
    nb = _pick(n1, (8,))
    cc, sc = _dft_mats(FNET_GROUP_DIM)
    c2, s2 = _dft_mats(n2)
    c1, s1 = _dft_mats(n1)
    ang = 2.0 * np.pi * (np.arange(n1)[:, None] * np.arange(n2)[None, :]) / L
    twc = jnp.asarray(np.broadcast_to(np.cos(ang)[:, :, None], (n1, n2, LANES)), F32)
    tws = jnp.asarray(np.broadcast_to(np.sin(ang)[:, :, None], (n1, n2, LANES)), F32)
    x = zf.reshape(B, n2, n1 * W)
    y_shape = jax.ShapeDtypeStruct((B, n1, n2, W), F32)
    y_spec = pl.BlockSpec((None, nb, n2, W), lambda b, j: (b, j, 0, 0))
    tw_spec = pl.BlockSpec((nb, n2, LANES), lambda b, j: (j, 0, 0))
    yr, yi = pl.pallas_call(
        functools.partial(_fnet_stage1_kernel, nb=nb), grid=(B, n1 // nb),
        in_specs=[pl.BlockSpec((None, n2, nb * W), lambda b, j: (b, 0, j)),
                  _full(cc), _full(sc), _full(c2), _full(s2), tw_spec, tw_spec],
        out_specs=[y_spec, y_spec], out_shape=[y_shape, y_shape],
        compiler_params=_cparams(("parallel", "parallel")), name="fnet_stage1",
    )(x, cc, sc, c2, s2, twc, tws)
    cols = n2 * W
    tn = 4096
    y2_spec = pl.BlockSpec((None, n1, tn), lambda b, j: (b, 0, j))
    out = pl.pallas_call(
        functools.partial(_fnet_stage2_kernel, scale=float((L * FNET_GROUP_DIM) ** -0.5)),
        grid=(B, cols // tn),
        in_specs=[y2_spec, y2_spec, _full(c1), _full(s1)],
        out_specs=y2_spec, out_shape=jax.ShapeDtypeStruct((B, n1, cols), BF16),
        compiler_params=_cparams(("parallel", "parallel")), name="fnet_stage2",
    )(yr.reshape(B, n1, cols), yi.reshape(B, n1, cols), c1, s1)
    return out.reshape(B, L, W)


def fourier_mix_short(zf):
    B, L, W = zf.shape
    cc, sc = _dft_mats(FNET_GROUP_DIM)
    c, s = _dft_mats(L)
    spec = pl.BlockSpec((None, L, W), lambda b: (b, 0, 0))
    return pl.pallas_call(
        functools.partial(_fnet_direct_kernel, scale=float((L * FNET_GROUP_DIM) ** -0.5)), grid=(B,),
        in_specs=[spec, _full(cc), _full(sc), _full(c), _full(s)],
        out_specs=spec, out_shape=jax.ShapeDtypeStruct((B, L, W), BF16),
        compiler_params=_cparams(("parallel",)), name="fnet_direct",
    )(zf, cc, sc, c, s)


HALO = 16


def _conv_kernel(prev_ref, cur_ref, next_ref, w_ref, bdw_ref, g_ref, b_ref, o_ref, ext_ref, *, nctx):
    i = pl.program_id(1)
    nt = pl.num_programs(1)
    tm = cur_ref.shape[0]

    def glu(z):
        z = z.astype(F32)
        return z[:, :CONV_WIDTH] * jax.nn.sigmoid(z[:, CONV_WIDTH:])

    has_prev = jnp.logical_and(i != 0, i != nctx)
    has_next = jnp.logical_and(i != nctx - 1, i != nt - 1)
    ext_ref[0:HALO, :] = jnp.where(has_prev, glu(prev_ref[tm - HALO:tm, :]), 0.0)
    ext_ref[HALO:HALO + tm, :] = glu(cur_ref[...])
    ext_ref[HALO + tm:2 * HALO + tm, :] = jnp.where(has_next, glu(next_ref[0:HALO, :]), 0.0)
    pad = CONV_KERNEL // 2
    acc = jnp.zeros((tm, CONV_WIDTH), F32)
    for k in range(CONV_KERNEL):
        acc = acc + w_ref[k:k + 1, :] * ext_ref[pl.ds(HALO - pad + k, tm), :]
    y = acc + bdw_ref[...]
    gw = CONV_WIDTH // CONV_GROUPS
    outs = []
    for gi in range(CONV_GROUPS):
        yg = y[:, gi * gw:(gi + 1) * gw]
        mu = jnp.mean(yg, axis=-1, keepdims=True)
        var = jnp.mean(jnp.square(yg - mu), axis=-1, keepdims=True)
        outs.append((yg - mu) * lax.rsqrt(var + 1e-5))
    yn = jnp.concatenate(outs, axis=1) * g_ref[...] + b_ref[...]
    o_ref[...] = (yn * jax.nn.sigmoid(yn)).astype(o_ref.dtype)


def conformer_conv(zc, w_dw, b_dw, g, b, nctx):
    B, T, W2 = zc.shape
    tm = ROW_TILE
    nt = T // tm
    w_pad = jnp.concatenate([w_dw, jnp.zeros((32 - CONV_KERNEL, CONV_WIDTH), F32)], axis=0)
    vec = pl.BlockSpec((1, CONV_WIDTH), lambda bb, i: (0, 0))
    return pl.pallas_call(
        functools.partial(_conv_kernel, nctx=nctx), grid=(B, nt),
        in_specs=[pl.BlockSpec((None, tm, W2), lambda bb, i: (bb, jnp.maximum(i - 1, 0), 0)),
                  pl.BlockSpec((None, tm, W2), lambda bb, i: (bb, i, 0)),
                  pl.BlockSpec((None, tm, W2), lambda bb, i: (bb, jnp.minimum(i + 1, nt - 1), 0)),
                  pl.BlockSpec((32, CONV_WIDTH), lambda bb, i: (0, 0)), vec, vec, vec],
        out_specs=pl.BlockSpec((None, tm, CONV_WIDTH), lambda bb, i: (bb, i, 0)),
        out_shape=jax.ShapeDtypeStruct((B, T, CONV_WIDTH), BF16),
        scratch_shapes=[pltpu.VMEM((tm + 2 * HALO, CONV_WIDTH), F32)],
        compiler_params=_cparams(("parallel", "parallel")), name="conformer_conv",
    )(zc, zc, zc, w_pad, b_dw.reshape(1, -1), g.reshape(1, -1), b.reshape(1, -1))


def _merge_kernel(h_ref, b0_ref, b1_ref, b2_ref, b3_ref, wg_ref, wb_ref, o_ref):
    h = h_ref[...]
    acc = None
    for n, br in enumerate((b0_ref, b1_ref, b2_ref, b3_ref)):
        gate = jax.nn.sigmoid(jnp.dot(h, wg_ref[n], preferred_element_type=F32))
        term = gate * jnp.dot(br[...], wb_ref[n], preferred_element_type=F32)
        acc = term if acc is None else acc + term
    o_ref[...] = acc.astype(o_ref.dtype)


def gated_merge(h, branches, w_gate, w_branch):
    M, D = h.shape
    N = w_gate.shape[2]
    tm = _pick(M, (FLAT_TILE, 256))
    tn = _pick(N, (512, 256))
    br_spec = pl.BlockSpec((tm, BRANCH_W), lambda j, i: (i, 0))
    return pl.pallas_call(
        _merge_kernel, grid=(N // tn, M // tm),
        in_specs=[pl.BlockSpec((tm, D), lambda j, i: (i, 0)), br_spec, br_spec, br_spec, br_spec,
                  pl.BlockSpec((N_BRANCH, D, tn), lambda j, i: (0, 0, j)),
                  pl.BlockSpec((N_BRANCH, BRANCH_W, tn), lambda j, i: (0, 0, j))],
        out_specs=pl.BlockSpec((tm, tn), lambda j, i: (i, j)),
        out_shape=jax.ShapeDtypeStruct((M, N), BF16),
        compiler_params=_cparams(("parallel", "parallel")), name="gated_merge",
    )(h, *branches, w_gate, w_branch)


def _peer_scores_kernel(q_ref, keys_ref, o_ref):
    for hp in range(2 * PEER_HEADS):
        o_ref[hp] = lax.dot_general(keys_ref[hp % 2], q_ref[:, hp * LANES:(hp + 1) * LANES],
                                    (((1,), (1,)), ((), ())), preferred_element_type=F32)


def peer_scores(q, keys):
    M = q.shape[0]
    tm = _pick(M, (FLAT_TILE, 256))
    nhp = 2 * PEER_HEADS
    return pl.pallas_call(
        _peer_scores_kernel, grid=(M // tm,),
        in_specs=[pl.BlockSpec((tm, q.shape[1]), lambda i: (i, 0)), _full(keys)],
        out_specs=pl.BlockSpec((nhp, PEER_NKEYS, tm), lambda i: (0, 0, i)),
        out_shape=jax.ShapeDtypeStruct((nhp, PEER_NKEYS, M), F32),
        compiler_params=_cparams(("parallel",)), name="peer_scores",
    )(q, keys)


_NRANK = PEER_TOPK + 1
_STAIRCASE = [(a, b) for a in range(_NRANK) for b in range(_NRANK) if (a + 1) * (b + 1) <= _NRANK]
_NCAND = -(-len(_STAIRCASE) // 8) * 8
_NRANK_PAD = -(-_NRANK // 8) * 8


def _peer_threshold_kernel(s_ref, e1_ref, cut_ref, e2_ref, t1_ref, t2_ref, c_ref):
    def extract(vals, dst_ref):
        v = vals
        m = prev = None
        for r in range(_NRANK):
            prev = m
            m = jnp.max(v, axis=0, keepdims=True)
            if dst_ref is not None:
                dst_ref[r:r + 1, :] = m
            v = jnp.where(v == m, NEG_INF, v)
        return prev, m

    for h in range(PEER_HEADS):
        u1 = s_ref[2 * h] * LOG2E
        u2 = s_ref[2 * h + 1] * LOG2E
        extract(u1, t1_ref)
        extract(u2, t2_ref)
        m1 = t1_ref[0:1, :]
        m2 = t2_ref[0:1, :]
        c_ref[...] = jnp.full(c_ref.shape, NEG_INF, F32)
        for r, (a, b) in enumerate(_STAIRCASE):
            c_ref[r:r + 1, :] = (t1_ref[a:a + 1, :] - m1) + (t2_ref[b:b + 1, :] - m2)
        cand = c_ref[...]
        c16, c17 = extract(cand, None)
        thr = 0.5 * (c16 + c17)
        z = jnp.sum(jnp.where(cand >= thr, jnp.exp2(cand), 0.0), axis=0, keepdims=True)
        lz = -jnp.log2(z)
        s1f = (u1 - m1) + lz
        e1_ref[h] = jnp.exp2(s1f)
        cut_ref[h] = jnp.exp2((thr + lz) - s1f)
        e2_ref[h] = jnp.exp2(u2 - m2)


def peer_thresholds(scores):
    nhp, nk, M = scores.shape
    tl = _pick(M, (256, 128))
    sf_spec = pl.BlockSpec((PEER_HEADS, nk, tl), lambda i: (0, 0, i))
    sf_shape = jax.ShapeDtypeStruct((PEER_HEADS, nk, M), F32)
    return pl.pallas_call(
        _peer_threshold_kernel, grid=(M // tl,),
        in_specs=[pl.BlockSpec((nhp, nk, tl), lambda i: (0, 0, i))],
        out_specs=[sf_spec, sf_spec, sf_spec], out_shape=[sf_shape, sf_shape, sf_shape],
        scratch_shapes=[pltpu.VMEM((_NRANK_PAD, tl), F32), pltpu.VMEM((_NRANK_PAD, tl), F32),
                        pltpu.VMEM((_NCAND, tl), F32)],
        compiler_params=_cparams(("parallel",)), name="peer_thresholds",
    )(scores)


PEER_TOKEN_SPLIT = 256
PEER_EXPERT_SPLIT = 512
PEER_EXPERT_CHUNK = 1024
PEER_TOKEN_TILE = 768


def _peer_experts_kernel(xt_ref, u_ref, vt_ref, e1_ref, cut_ref, e2_ref, o_ref, acc_ref, *scratch, ek):
    j = pl.program_id(1)
    tm = xt_ref.shape[1]
    th, es = PEER_TOKEN_SPLIT, PEER_EXPERT_SPLIT
    pieces = [(c, t) for c in range(ek // es) for t in range(tm // th)]
    st_refs, a_refs = scratch[:len(pieces)], scratch[len(pieces):]

    @pl.when(j == 0)
    def _():
        acc_ref[...] = jnp.zeros(acc_ref.shape, F32)

    def pre_activations(p):
        c, t = pieces[p]
        st_refs[p][...] = jnp.dot(u_ref[c * es:(c + 1) * es, :], xt_ref[:, t * th:(t + 1) * th],
                                  preferred_element_type=F32)

    def weights(p):
        c, t = pieces[p]
        for r in range(es // PEER_NKEYS):
            i1 = j * (ek // PEER_NKEYS) + c * (es // PEER_NKEYS) + r
            rows = slice(r * PEER_NKEYS, (r + 1) * PEER_NKEYS)
            half = slice(t * th, (t + 1) * th)
            cut_rows = [cut_ref[h, pl.ds(i1, 1), half] for h in range(PEER_HEADS)]
            e1_rows = [e1_ref[h, pl.ds(i1, 1), half] for h in range(PEER_HEADS)]
            for g in range(th // LANES):
                lanes = slice(t * th + g * LANES, t * th + (g + 1) * LANES)
                sub = slice(g * LANES, (g + 1) * LANES)
                w = None
                for h in range(PEER_HEADS):
                    e2 = e2_ref[h, :, lanes]
                    sel = jnp.where(e2 >= cut_rows[h][:, sub], e2, 0.0)
                    sel = sel * e1_rows[h][:, sub]
                    w = sel if w is None else w + sel
                pre = st_refs[p][rows, g * LANES:(g + 1) * LANES]
                act = pre * (0.5 + 0.5 * lax.erf(pre * (2.0 ** -0.5)))
                a_refs[p][rows, g * LANES:(g + 1) * LANES] = (w * act).astype(BF16)

    def accumulate(p):
        c, t = pieces[p]
        lanes = slice(t * th, (t + 1) * th)
        acc_ref[:, lanes] += jnp.dot(vt_ref[:, c * es:(c + 1) * es], a_refs[p][...], preferred_element_type=F32)

    for p in range(len(pieces)):
        pre_activations(p)
    for p in range(len(pieces)):
        weights(p)
        accumulate(p)

    @pl.when(j == pl.num_programs(1) - 1)
    def _():
        o_ref[...] = acc_ref[...].T


def peer_experts(xt, u, vt, e1, cut, e2):
    D, M = xt.shape
    E = u.shape[0]
    tm = _pick(M, (PEER_TOKEN_TILE, FLAT_TILE, 256))
    ek = vt.shape[2]
    th, es = PEER_TOKEN_SPLIT, PEER_EXPERT_SPLIT
    npieces = (tm // th) * (ek // es)
    table_spec = pl.BlockSpec((PEER_HEADS, PEER_NKEYS, tm), lambda i, j: (0, 0, i), pipeline_mode=pl.Buffered(1))
    return pl.pallas_call(
        functools.partial(_peer_experts_kernel, ek=ek), grid=(M // tm, E // ek),
        in_specs=[pl.BlockSpec((D, tm), lambda i, j: (0, i)),
                  pl.BlockSpec((ek, D), lambda i, j: (j, 0)),
                  pl.BlockSpec((None, D, ek), lambda i, j: (j, 0, 0)),
                  table_spec, table_spec, table_spec],
        out_specs=pl.BlockSpec((tm, D), lambda i, j: (i, 0)),
        out_shape=jax.ShapeDtypeStruct((M, D), F32),
        scratch_shapes=([pltpu.VMEM((D, tm), F32)]
                        + [pltpu.VMEM((es, th), F32)] * npieces + [pltpu.VMEM((es, th), BF16)] * npieces),
        compiler_params=_cparams(("parallel", "arbitrary")), name="peer_experts",
    )(xt, u, vt, e1, cut, e2)


def _final_kernel(x_ref, d_ref, mg_ref, g_ref, o_ref, *, gate_row):
    x = x_ref[...] + mg_ref[gate_row:gate_row + 1, :] * d_ref[...]
    o_ref[...] = x * lax.rsqrt(jnp.mean(x * x, axis=-1, keepdims=True) + EPS) * g_ref[...]


def final_norm(x, delta, mod_gate, g, nctx, gate_row):
    B, T, D = x.shape
    tm = ROW_TILE
    L = T - nctx * tm
    row_spec = pl.BlockSpec((None, tm, D), lambda b, i: (b, i + nctx, 0))
    return pl.pallas_call(
        functools.partial(_final_kernel, gate_row=gate_row), grid=(B, L // tm),
        in_specs=[row_spec, row_spec, pl.BlockSpec((None, 8, D), lambda b, i: (b, 0, 0)),
                  pl.BlockSpec((1, D), lambda b, i: (0, 0))],
        out_specs=pl.BlockSpec((None, tm, D), lambda b, i: (b, i, 0)),
        out_shape=jax.ShapeDtypeStruct((B, L, D), F32),
        compiler_params=_cparams(("parallel", "parallel")), name="final_norm",
    )(x, delta, mod_gate, g.reshape(1, D))


def _rope_tables(L, ctx_len):
    rows = L // GRID_W
    row = jnp.repeat(jnp.arange(rows, dtype=F32), GRID_W)
    col = jnp.tile(jnp.arange(GRID_W, dtype=F32), rows)
    nf = MLA_ROPE // 4
    inv = ROPE_BASE ** (-jnp.arange(nf, dtype=F32) / nf)
    ang = jnp.concatenate([row[:, None] * inv, col[:, None] * inv], axis=-1)
    cos, sin = jnp.cos(ang), jnp.sin(ang)
    cos_t = jnp.concatenate([cos, cos, cos, cos], axis=-1)
    sin_t = jnp.concatenate([-sin, sin, -sin, sin], axis=-1)
    cos_t = jnp.concatenate([jnp.ones((ctx_len, LANES), F32), cos_t], axis=0)
    sin_t = jnp.concatenate([jnp.zeros((ctx_len, LANES), F32), sin_t], axis=0)
    return cos_t, sin_t


def kernel(x, c, ctx, c_ctx, w_ada, b_ada, g_norm1, g_norm2, w_in, g_q_mla, w_uq, g_kv_mla, w_ukv, lam_q1, lam_k1, lam_q2, lam_k2, g_subln, w_dw, b_dw, g_conv_norm, b_conv_norm, w_gate, w_branch, w_out, w_peer_q, peer_keys, peer_u, peer_v, g_final):
    B, L, D = x.shape
    ctx_len = ctx.shape[1]
    depth = w_ada.shape[0]
    T = ctx_len + L
    M = B * T
    assert MLA_ROPE == DIFF_QK and ctx_len % ROW_TILE == 0 and L % ROW_TILE == 0
    nctx = ctx_len // ROW_TILE

    cos_t, sin_t = _rope_tables(L, ctx_len)
    xs = jnp.concatenate([ctx, x], axis=1)

    mods = ada_modulation(jnp.concatenate([c, c_ctx[None, :]], axis=0), w_ada, b_ada)
    mods = mods.reshape(depth, 8, 6, D)[:, :B + 1]
    mods = jnp.concatenate([mods, jnp.zeros((depth, B + 1, 2, D), F32)], axis=2)

    o_c, o_f, o_d = 0, 2 * CONV_WIDTH, 2 * CONV_WIDTH + FNET_WIDTH
    pending = None
    for l in range(depth):
        lam_init = 0.8 - 0.6 * math.exp(-0.3 * l)
        s0, s1, s2 = MLA_Q_RANK, MLA_Q_RANK + MLA_KV_RANK, IN_MLA
        wi = w_in[l]
        w_mla = jnp.concatenate([wi[:, :s2], jnp.zeros((D, LANES - MLA_ROPE), F32)], axis=1).astype(BF16)
        w_diff = wi[:, s2:s2 + IN_DIFF].astype(BF16)
        w_fnet = wi[:, s2 + IN_DIFF:s2 + IN_DIFF + FNET_WIDTH].astype(BF16)
        w_conv = wi[:, s2 + IN_DIFF + FNET_WIDTH:].astype(BF16)
        hd = MLA_NOPE + MLA_ROPE
        w_uq_p = jnp.pad(w_uq[l].reshape(MLA_Q_RANK, MLA_HEADS, hd),
                         ((0, 0), (0, 0), (0, 256 - hd))).reshape(MLA_Q_RANK, MLA_HEADS * 256).astype(BF16)
        wkv = w_ukv[l].reshape(MLA_KV_RANK, MLA_HEADS, MLA_NOPE + MLA_V)
        w_ukv_p = jnp.concatenate([wkv[:, :, :MLA_NOPE].reshape(MLA_KV_RANK, -1),
                                   wkv[:, :, MLA_NOPE:].reshape(MLA_KV_RANK, -1)], axis=1).astype(BF16)
        lam_vecs = jnp.stack([lam_q1[l], lam_k1[l], lam_q2[l], lam_k2[l]], axis=0)

        if pending is None:
            h = norm_modulate(xs, g_norm1[l], mods[l], 0, 1, nctx)
        else:
            xs, h = norm_modulate(xs, g_norm1[l], mods[l], 0, 1, nctx,
                                  delta=pending[0], mod_gate=pending[1], gate_row=5)
        hf = h.reshape(M, D)
        z_mla = matmul(hf, w_mla, BF16, "in_proj_mla").reshape(B, T, -1)
        z_diff = matmul(hf, w_diff, BF16, "in_proj_diff").reshape(B, T, -1)
        z_fnet = matmul(hf, w_fnet, BF16, "in_proj_fnet").reshape(B, T, -1)
        z_conv = matmul(hf, w_conv, BF16, "in_proj_conv").reshape(B, T, -1)

        qm, km, vm, qd, kd, vd = attention_projections(z_mla, z_diff, cos_t, sin_t, g_q_mla[l], g_kv_mla[l],
                                                   w_uq_p, w_ukv_p)
        o_mla = mla_attention(qm, km, vm, ctx_len)
        o_diff = diff_attention(qd, kd, vd, lam_vecs, g_subln[l], lam_init, ctx_len)
        o_fnet = jnp.concatenate([fourier_mix_short(z_fnet[:, :ctx_len]), fourier_mix_long(z_fnet[:, ctx_len:])],
                                 axis=1)
        o_conv = conformer_conv(z_conv, w_dw[l], b_dw[l], g_conv_norm[l], b_conv_norm[l], nctx)

        y = gated_merge(hf, [o.reshape(M, BRANCH_W) for o in (o_mla, o_fnet, o_conv, o_diff)],
                        w_gate[l].astype(BF16), w_branch[l].astype(BF16))
        mix = matmul(y, w_out[l].astype(BF16), F32, "out_proj").reshape(B, T, D)
        xs, h2 = norm_modulate(xs, g_norm2[l], mods[l], 3, 4, nctx, delta=mix, mod_gate=mods[l], gate_row=2)

        h2f = h2.reshape(M, D)
        q = matmul(h2f, w_peer_q[l].astype(BF16), BF16, "peer_query")
        scores = peer_scores(q, peer_keys[l].astype(BF16))
        e1, cut, e2 = peer_thresholds(scores)
        peer_vt = jnp.swapaxes(peer_v[l].reshape(-1, PEER_EXPERT_CHUNK, D), 1, 2).astype(BF16)
        peer_out = peer_experts(h2f.T, peer_u[l].astype(BF16), peer_vt, e1, cut, e2)
        pending = (peer_out.reshape(B, T, D), mods[l])

    return final_norm(xs, pending[0], pending[1], g_final, nctx, 5)
```

```python
import functools
import math

import numpy as np
import jax
import jax.numpy as jnp
from jax import lax
from jax.experimental import pallas as pl
from jax.experimental.pallas import tpu as pltpu

F32 = jnp.float32
BF16 = jnp.bfloat16

GRID_W = 64
ROPE_BASE = 10000.0
EPS = 1e-6
MLA_HEADS = 4
MLA_Q_RANK = 384
MLA_KV_RANK = 256
MLA_NOPE = 128
MLA_ROPE = 64
MLA_V = 128
DIFF_HEADS = 4
DIFF_QK = 64
DIFF_V = 128
FNET_GROUPS = 4
FNET_GROUP_DIM = 128
FNET_WIDTH = 512
CONV_WIDTH = 512
CONV_KERNEL = 31
CONV_GROUPS = 4
N_BRANCH = 4
BRANCH_W = 512
IN_MLA = MLA_Q_RANK + MLA_KV_RANK + MLA_ROPE
IN_DIFF = DIFF_HEADS * (4 * DIFF_QK + DIFF_V)
PEER_HEADS = 8
PEER_NKEYS = 128
PEER_EXPERTS = PEER_NKEYS * PEER_NKEYS
PEER_DKEY = 256
PEER_TOPK = 16

LANES = 128
ROW_TILE = 256
FLAT_TILE = 512
VMEM_LIMIT = 56 * 1024 * 1024
NEG_INF = float("-inf")
LOG2E = math.log2(math.e)


def _cparams(sem):
    return pltpu.CompilerParams(dimension_semantics=sem, vmem_limit_bytes=VMEM_LIMIT)


def _pick(n, cands):
    for c in cands:
        if n % c == 0:
            return c
    raise ValueError(f"no tile for {n} in {cands}")


def _ada_kernel(cb_ref, w_ref, b_ref, o_ref, *, rows, tn):
    outs = []
    for r in range(rows):
        a = cb_ref[r]
        a = a * jax.nn.sigmoid(a)
        cols = [jnp.sum(w_ref[:, j * LANES:(j + 1) * LANES] * a, axis=0, keepdims=True)
                for j in range(tn // LANES)]
        outs.append(jnp.concatenate(cols, axis=1) + b_ref[...])
    outs.append(jnp.zeros((8 - rows, tn), F32))
    o_ref[...] = jnp.concatenate(outs, axis=0)


def ada_modulation(cond, w_ada, b_ada):
    rows, d = cond.shape
    depth, _, n = w_ada.shape
    tn = _pick(n, (1024, 512))
    cb = jnp.broadcast_to(cond[:, :, None], (rows, d, LANES))
    return pl.pallas_call(
        functools.partial(_ada_kernel, rows=rows, tn=tn),
        grid=(depth, n // tn),
        in_specs=[pl.BlockSpec((rows, d, LANES), lambda l, j: (0, 0, 0)),
                  pl.BlockSpec((None, d, tn), lambda l, j: (l, 0, j)),
                  pl.BlockSpec((None, 1, tn), lambda l, j: (l, 0, j))],
        out_specs=pl.BlockSpec((None, 8, tn), lambda l, j: (l, 0, j)),
        out_shape=jax.ShapeDtypeStruct((depth, 8, n), F32),
        compiler_params=_cparams(("arbitrary", "arbitrary")),
        name="ada_modulation",
    )(cb, w_ada, b_ada.reshape(depth, 1, n))


def _norm_mod_kernel(*refs, has_delta, gate_row, shift_row, scale_row):
    if has_delta:
        x_ref, d_ref, mg_ref, ms_ref, g_ref, xo_ref, h_ref = refs
        x = x_ref[...] + mg_ref[gate_row:gate_row + 1, :] * d_ref[...].astype(F32)
        xo_ref[...] = x
    else:
        x_ref, ms_ref, g_ref, h_ref = refs
        x = x_ref[...]
    y = x * lax.rsqrt(jnp.mean(x * x, axis=-1, keepdims=True) + EPS) * g_ref[...]
    h = y * (1.0 + ms_ref[scale_row:scale_row + 1, :]) + ms_ref[shift_row:shift_row + 1, :]
    h_ref[...] = h.astype(BF16)


def norm_modulate(x, g, mod_ss, shift_row, scale_row, nctx, delta=None, mod_gate=None, gate_row=None):
    B, T, D = x.shape
    tm = ROW_TILE
    row_spec = pl.BlockSpec((None, tm, D), lambda b, i: (b, i, 0))
    mod_spec = pl.BlockSpec((None, 8, D), lambda b, i: (jnp.where(i < nctx, B, b), 0, 0))
    g_spec = pl.BlockSpec((1, D), lambda b, i: (0, 0))
    h_shape = jax.ShapeDtypeStruct((B, T, D), BF16)
    kern = functools.partial(_norm_mod_kernel, has_delta=delta is not None, gate_row=gate_row,
                             shift_row=shift_row, scale_row=scale_row)
    if delta is None:
        return pl.pallas_call(
            kern, grid=(B, T // tm), in_specs=[row_spec, mod_spec, g_spec], out_specs=row_spec,
            out_shape=h_shape, compiler_params=_cparams(("parallel", "parallel")), name="norm_modulate",
        )(x, mod_ss, g.reshape(1, D))
    return pl.pallas_call(
        kern, grid=(B, T // tm), in_specs=[row_spec, row_spec, mod_spec, mod_spec, g_spec],
        out_specs=[row_spec, row_spec],
        out_shape=[jax.ShapeDtypeStruct((B, T, D), F32), h_shape],
        compiler_params=_cparams(("parallel", "parallel")), name="residual_norm_modulate",
    )(x, delta, mod_gate, mod_ss, g.reshape(1, D))


def _mm_kernel(x_ref, w_ref, o_ref):
    o_ref[...] = jnp.dot(x_ref[...], w_ref[...], preferred_element_type=F32).astype(o_ref.dtype)


def matmul(x, w, out_dtype, name):
    M, K = x.shape
    N = w.shape[1]
    tm = _pick(M, (FLAT_TILE, 256, 128))
    tn = _pick(N, (1024, 768, 512, 256, 128))
    return pl.pallas_call(
        _mm_kernel, grid=(N // tn, M // tm),
        in_specs=[pl.BlockSpec((tm, K), lambda j, i: (i, 0)),
                  pl.BlockSpec((K, tn), lambda j, i: (0, j))],
        out_specs=pl.BlockSpec((tm, tn), lambda j, i: (i, j)),
        out_shape=jax.ShapeDtypeStruct((M, N), out_dtype),
        compiler_params=_cparams(("parallel", "parallel")), name=name,
    )(x, w)


def _rope(v, cos, sin, lane):
    r = jnp.where((lane % 64) < 32, pltpu.roll(v, 96, 1), pltpu.roll(v, 32, 1))
    return v * cos + r * sin


def _proj_kernel(zm_ref, zd_ref, cos_ref, sin_ref, gq_ref, gkv_ref, wuq_ref, wukv_ref,
                 qm_ref, km_ref, vm_ref, qd_ref, kd_ref, vd_ref, *, mla_scale, diff_scale):
    tm = zm_ref.shape[0]
    cos = cos_ref[...]
    sin = sin_ref[...]
    lane = lax.broadcasted_iota(jnp.int32, (tm, LANES), 1)
    ones_col = jnp.where(lane == 0, 1.0, 0.0).astype(BF16)

    def rms(v, g):
        return v * lax.rsqrt(jnp.mean(v * v, axis=-1, keepdims=True) + EPS) * g

    zm = zm_ref[...].astype(F32)
    qn = rms(zm[:, :MLA_Q_RANK], gq_ref[...]).astype(BF16)
    kvn = rms(zm[:, MLA_Q_RANK:MLA_Q_RANK + MLA_KV_RANK], gkv_ref[...]).astype(BF16)
    k_rope = _rope(zm[:, MLA_Q_RANK + MLA_KV_RANK:], cos, sin, lane).astype(BF16)
    q = jnp.dot(qn, wuq_ref[...], preferred_element_type=F32)
    kv = jnp.dot(kvn, wukv_ref[...], preferred_element_type=F32)
    for h in range(MLA_HEADS):
        qm_ref[h, :, 0:LANES] = (q[:, 256 * h:256 * h + LANES] * mla_scale).astype(BF16)
        qm_ref[h, :, LANES:2 * LANES] = (
            _rope(q[:, 256 * h + LANES:256 * h + 2 * LANES], cos, sin, lane) * mla_scale).astype(BF16)
        km_ref[h, :, 0:LANES] = kv[:, LANES * h:LANES * (h + 1)].astype(BF16)
        km_ref[h, :, LANES:2 * LANES] = k_rope
        vm_ref[h, :, 0:LANES] = kv[:, 512 + LANES * h:512 + LANES * (h + 1)].astype(BF16)
        vm_ref[h, :, LANES:2 * LANES] = ones_col
    zd = zd_ref[...].astype(F32)
    for h in range(DIFF_HEADS):
        qd_ref[h] = (_rope(zd[:, LANES * h:LANES * (h + 1)], cos, sin, lane) * diff_scale).astype(BF16)
        kd_ref[h] = _rope(zd[:, 512 + LANES * h:512 + LANES * (h + 1)], cos, sin, lane).astype(BF16)
        vd_ref[h, :, 0:LANES] = zd_ref[:, 1024 + LANES * h:1024 + LANES * (h + 1)]
        vd_ref[h, :, LANES:2 * LANES] = ones_col


def attention_projections(z_mla, z_diff, cos_t, sin_t, g_q, g_kv, w_uq_p, w_ukv_p):
    B, T, _ = z_mla.shape
    tm = ROW_TILE
    H = MLA_HEADS

    def head_spec(w):
        return pl.BlockSpec((None, H, tm, w), lambda b, i: (b, 0, i, 0))

    def head_shape(w):
        return jax.ShapeDtypeStruct((B, H, T, w), BF16)

    return pl.pallas_call(
        functools.partial(_proj_kernel, mla_scale=float((MLA_NOPE + MLA_ROPE) ** -0.5 * LOG2E),
                          diff_scale=float(DIFF_QK ** -0.5 * LOG2E)),
        grid=(B, T // tm),
        in_specs=[pl.BlockSpec((None, tm, z_mla.shape[2]), lambda b, i: (b, i, 0)),
                  pl.BlockSpec((None, tm, z_diff.shape[2]), lambda b, i: (b, i, 0)),
                  pl.BlockSpec((tm, LANES), lambda b, i: (i, 0)),
                  pl.BlockSpec((tm, LANES), lambda b, i: (i, 0)),
                  pl.BlockSpec((1, MLA_Q_RANK), lambda b, i: (0, 0)),
                  pl.BlockSpec((1, MLA_KV_RANK), lambda b, i: (0, 0)),
                  pl.BlockSpec(w_uq_p.shape, lambda b, i: (0, 0)),
                  pl.BlockSpec(w_ukv_p.shape, lambda b, i: (0, 0))],
        out_specs=[head_spec(256), head_spec(256), head_spec(256), head_spec(128), head_spec(128), head_spec(256)],
        out_shape=[head_shape(256), head_shape(256), head_shape(256), head_shape(128), head_shape(128),
                   head_shape(256)],
        compiler_params=_cparams(("parallel", "parallel")), name="attention_projections",
    )(z_mla, z_diff, cos_t, sin_t, g_q.reshape(1, -1), g_kv.reshape(1, -1), w_uq_p, w_ukv_p)


V_EXT = 2 * LANES


def _att_scores(q, k, mask):
    s = lax.dot_general(q, k, (((1,), (1,)), ((), ())), preferred_element_type=F32)
    return s if mask is None else jnp.where(mask, NEG_INF, s)


def _lane_tile(x, width):
    return x if width == LANES else jnp.concatenate([x] * (width // LANES), axis=1)


def _att_update(s, v_ext, m_ref, acc_ref, idx):
    m_prev = m_ref[idx]
    m_new = jnp.maximum(m_prev, jnp.max(s, axis=-1, keepdims=True))
    p = jnp.exp2(s - _lane_tile(m_new, s.shape[1])).astype(BF16)
    alpha = _lane_tile(jnp.exp2(m_prev - m_new), V_EXT)
    acc_ref[idx] = alpha * acc_ref[idx] + jnp.dot(p, v_ext, preferred_element_type=F32)
    m_ref[idx] = m_new


def _att_init(m_ref, acc_ref):
    m_ref[...] = jnp.full(m_ref.shape, NEG_INF, F32)
    acc_ref[...] = jnp.zeros(acc_ref.shape, F32)


def _att_step(streams, mask, m_ref, acc_ref):
    for idx, (q, k, v) in enumerate(streams):
        _att_update(_att_scores(q(), k(), mask), v(), m_ref, acc_ref, idx)


def _att_sweep(step, qi, ki, tq, tk, ctx_len):
    @pl.when(qi == 0)
    def _():
        row = lax.broadcasted_iota(jnp.int32, (tq, tk), 0)
        col = ki * tk + lax.broadcasted_iota(jnp.int32, (tq, tk), 1)
        step((row < ctx_len) & (col >= ctx_len))

    @pl.when(qi != 0)
    def _():
        step(None)


def _att_result(acc):
    return acc[:, :LANES] / acc[:, LANES:LANES + 1]


def _mla_att_kernel(q_ref, k_ref, v_ref, o_ref, m_ref, acc_ref, *, tq, tk, ctx_len):
    qi = pl.program_id(1)
    ki = pl.program_id(2)

    @pl.when(ki == 0)
    def _():
        _att_init(m_ref, acc_ref)

    streams = [(lambda h=h: q_ref[h], lambda h=h: k_ref[h], lambda h=h: v_ref[h]) for h in range(MLA_HEADS)]
    _att_sweep(lambda mask: _att_step(streams, mask, m_ref, acc_ref), qi, ki, tq, tk, ctx_len)

    @pl.when(ki == pl.num_programs(2) - 1)
    def _():
        for h in range(MLA_HEADS):
            o_ref[:, h * MLA_V:(h + 1) * MLA_V] = _att_result(acc_ref[h]).astype(o_ref.dtype)


def _att_specs(H, tq, tk, dk):
    return [pl.BlockSpec((None, H, tq, dk), lambda b, i, j: (b, 0, i, 0)),
            pl.BlockSpec((None, H, tk, dk), lambda b, i, j: (b, 0, j, 0)),
            pl.BlockSpec((None, H, tk, V_EXT), lambda b, i, j: (b, 0, j, 0))]


def _att_scratch(nstreams, tq):
    return [pltpu.VMEM((nstreams, tq, LANES), F32), pltpu.VMEM((nstreams, tq, V_EXT), F32)]


def mla_attention(qm, km, vm, ctx_len):
    B, H, T, dk = qm.shape
    tq = tk = _pick(T, (768, 256))
    assert ctx_len <= tq
    return pl.pallas_call(
        functools.partial(_mla_att_kernel, tq=tq, tk=tk, ctx_len=ctx_len),
        grid=(B, T // tq, T // tk),
        in_specs=_att_specs(H, tq, tk, dk),
        out_specs=pl.BlockSpec((None, tq, H * MLA_V), lambda b, i, j: (b, i, 0)),
        out_shape=jax.ShapeDtypeStruct((B, T, H * MLA_V), BF16),
        scratch_shapes=_att_scratch(H, tq),
        compiler_params=_cparams(("parallel", "parallel", "arbitrary")), name="mla_attention",
    )(qm, km, vm)


def _diff_att_kernel(q_ref, k_ref, v_ref, lam_ref, g_ref, o_ref, m_ref, acc_ref, *, tq, tk, ctx_len, lam_init):
    qi = pl.program_id(1)
    ki = pl.program_id(2)

    @pl.when(ki == 0)
    def _():
        _att_init(m_ref, acc_ref)

    def component(h, c):
        q = q_ref[h]
        first = lax.broadcasted_iota(jnp.int32, q.shape, 1) < DIFF_QK
        return jnp.where(first if c == 0 else jnp.logical_not(first), q, jnp.zeros_like(q))

    streams = [(lambda h=h, c=c: component(h, c), lambda h=h: k_ref[h], lambda h=h: v_ref[h])
               for h in range(DIFF_HEADS) for c in range(2)]
    _att_sweep(lambda mask: _att_step(streams, mask, m_ref, acc_ref), qi, ki, tq, tk, ctx_len)

    @pl.when(ki == pl.num_programs(2) - 1)
    def _():
        lv = lam_ref[...]
        lam = (jnp.exp(jnp.sum(lv[0:1] * lv[1:2], axis=-1, keepdims=True))
               - jnp.exp(jnp.sum(lv[2:3] * lv[3:4], axis=-1, keepdims=True)) + lam_init)
        for h in range(DIFF_HEADS):
            o = _att_result(acc_ref[2 * h]) - lam * _att_result(acc_ref[2 * h + 1])
            o = o * lax.rsqrt(jnp.mean(o * o, axis=-1, keepdims=True) + EPS) * g_ref[...]
            o_ref[:, h * DIFF_V:(h + 1) * DIFF_V] = (o * (1.0 - lam_init)).astype(o_ref.dtype)


def diff_attention(qd, kd, vd, lam_vecs, g_subln, lam_init, ctx_len):
    B, H, T, dk = qd.shape
    tq = tk = _pick(T, (768, 256))
    assert ctx_len <= tq
    return pl.pallas_call(
        functools.partial(_diff_att_kernel, tq=tq, tk=tk, ctx_len=ctx_len, lam_init=lam_init),
        grid=(B, T // tq, T // tk),
        in_specs=_att_specs(H, tq, tk, dk) + [pl.BlockSpec(lam_vecs.shape, lambda b, i, j: (0, 0)),
                                               pl.BlockSpec((1, DIFF_V), lambda b, i, j: (0, 0))],
        out_specs=pl.BlockSpec((None, tq, H * DIFF_V), lambda b, i, j: (b, i, 0)),
        out_shape=jax.ShapeDtypeStruct((B, T, H * DIFF_V), BF16),
        scratch_shapes=_att_scratch(2 * H, tq),
        compiler_params=_cparams(("parallel", "parallel", "arbitrary")), name="diff_attention",
    )(qd, kd, vd, lam_vecs, g_subln.reshape(1, -1))


def _dft_mats(n):
    k = np.arange(n)
    ang = 2.0 * np.pi * ((k[:, None] * k[None, :]) % n) / n
    return jnp.asarray(np.cos(ang), BF16), jnp.asarray(np.sin(ang), BF16)


def _channel_dft(x, cc, sc):
    tr, ti = [], []
    for s in range(x.shape[1] // LANES):
        xs = x[:, s * LANES:(s + 1) * LANES]
        tr.append(jnp.dot(xs, cc, preferred_element_type=F32))
        ti.append(-jnp.dot(xs, sc, preferred_element_type=F32))
    return jnp.concatenate(tr, axis=1), jnp.concatenate(ti, axis=1)


def _fnet_stage1_kernel(x_ref, cc_ref, sc_ref, c_ref, s_ref, twc_ref, tws_ref, yr_ref, yi_ref, *, nb):
    tr, ti = _channel_dft(x_ref[...], cc_ref[...], sc_ref[...])
    trb, tib = tr.astype(BF16), ti.astype(BF16)
    c, s = c_ref[...], s_ref[...]
    yr = jnp.dot(c, trb, preferred_element_type=F32) + jnp.dot(s, tib, preferred_element_type=F32)
    yi = jnp.dot(c, tib, preferred_element_type=F32) - jnp.dot(s, trb, preferred_element_type=F32)
    for i in range(nb):
        twc = jnp.concatenate([twc_ref[i]] * FNET_GROUPS, axis=1)
        tws = jnp.concatenate([tws_ref[i]] * FNET_GROUPS, axis=1)
        a = yr[:, i * FNET_WIDTH:(i + 1) * FNET_WIDTH]
        b = yi[:, i * FNET_WIDTH:(i + 1) * FNET_WIDTH]
        yr_ref[i] = a * twc + b * tws
        yi_ref[i] = b * twc - a * tws


def _fnet_stage2_kernel(yr_ref, yi_ref, c_ref, s_ref, o_ref, *, scale):
    xr = (jnp.dot(c_ref[...], yr_ref[...].astype(BF16), preferred_element_type=F32)
          + jnp.dot(s_ref[...], yi_ref[...].astype(BF16), preferred_element_type=F32))
    o_ref[...] = (xr * scale).astype(o_ref.dtype)


def _fnet_direct_kernel(x_ref, cc_ref, sc_ref, c_ref, s_ref, o_ref, *, scale):
    tr, ti = _channel_dft(x_ref[...], cc_ref[...], sc_ref[...])
    xr = (jnp.dot(c_ref[...], tr.astype(BF16), preferred_element_type=F32)
          + jnp.dot(s_ref[...], ti.astype(BF16), preferred_element_type=F32))
    o_ref[...] = (xr * scale).astype(o_ref.dtype)


def _full(a):
    return pl.BlockSpec(a.shape, lambda *_: (0,) * a.ndim)


def fourier_mix_long(zf):
    B, L, W = zf.shape
    n2 = LANES
    n1 = L // n2
    nb = _pick(n1, (8,))
    cc, sc = _dft_mats(FNET_GROUP_DIM)
    c2, s2 = _dft_mats(n2)
    c1, s1 = _dft_mats(n1)
    ang = 2.0 * np.pi * (np.arange(n1)[:, None] * np.arange(n2)[None, :]) / L
    twc = jnp.asarray(np.broadcast_to(np.cos(ang)[:, :, None], (n1, n2, LANES)), F32)
    tws = jnp.asarray(np.broadcast_to(np.sin(ang)[:, :, None], (n1, n2, LANES)), F32)
    x = zf.reshape(B, n2, n1 * W)
    y_shape = jax.ShapeDtypeStruct((B, n1, n2, W), F32)
    y_spec = pl.BlockSpec((None, nb, n2, W), lambda b, j: (b, j, 0, 0))
    tw_spec = pl.BlockSpec((nb, n2, LANES), lambda b, j: (j, 0, 0))
    yr, yi = pl.pallas_call(
        functools.partial(_fnet_stage1_kernel, nb=nb), grid=(B, n1 // nb),
        in_specs=[pl.BlockSpec((None, n2, nb * W), lambda b, j: (b, 0, j)),
                  _full(cc), _full(sc), _full(c2), _full(s2), tw_spec, tw_spec],
        out_specs=[y_spec, y_spec], out_shape=[y_shape, y_shape],
        compiler_params=_cparams(("parallel", "parallel")), name="fnet_stage1",
    )(x, cc, sc, c2, s2, twc, tws)
    cols = n2 * W
    tn = 4096
    y2_spec = pl.BlockSpec((None, n1, tn), lambda b, j: (b, 0, j))
    out = pl.pallas_call(
        functools.partial(_fnet_stage2_kernel, scale=float((L * FNET_GROUP_DIM) ** -0.5)),
        grid=(B, cols // tn),
        in_specs=[y2_spec, y2_spec, _full(c1), _full(s1)],
        out_specs=y2_spec, out_shape=jax.ShapeDtypeStruct((B, n1, cols), BF16),
        compiler_params=_cparams(("parallel", "parallel")), name="fnet_stage2",
    )(yr.reshape(B, n1, cols), yi.reshape(B, n1, cols), c1, s1)
    return out.reshape(B, L, W)


def fourier_mix_short(zf):
    B, L, W = zf.shape
    cc, sc = _dft_mats(FNET_GROUP_DIM)
    c, s = _dft_mats(L)
    spec = pl.BlockSpec((None, L, W), lambda b: (b, 0, 0))
    return pl.pallas_call(
        functools.partial(_fnet_direct_kernel, scale=float((L * FNET_GROUP_DIM) ** -0.5)), grid=(B,),
        in_specs=[spec, _full(cc), _full(sc), _full(c), _full(s)],
        out_specs=spec, out_shape=jax.ShapeDtypeStruct((B, L, W), BF16),
        compiler_params=_cparams(("parallel",)), name="fnet_direct",
    )(zf, cc, sc, c, s)


HALO = 16


def _conv_kernel(prev_ref, cur_ref, next_ref, w_ref, bdw_ref, g_ref, b_ref, o_ref, ext_ref, *, nctx):
    i = pl.program_id(1)
    nt = pl.num_programs(1)
    tm = cur_ref.shape[0]

    def glu(z):
        z = z.astype(F32)
        return z[:, :CONV_WIDTH] * jax.nn.sigmoid(z[:, CONV_WIDTH:])

    has_prev = jnp.logical_and(i != 0, i != nctx)
    has_next = jnp.logical_and(i != nctx - 1, i != nt - 1)
    ext_ref[0:HALO, :] = jnp.where(has_prev, glu(prev_ref[tm - HALO:tm, :]), 0.0)
    ext_ref[HALO:HALO + tm, :] = glu(cur_ref[...])
    ext_ref[HALO + tm:2 * HALO + tm, :] = jnp.where(has_next, glu(next_ref[0:HALO, :]), 0.0)
    pad = CONV_KERNEL // 2
    acc = jnp.zeros((tm, CONV_WIDTH), F32)
    for k in range(CONV_KERNEL):
        acc = acc + w_ref[k:k + 1, :] * ext_ref[pl.ds(HALO - pad + k, tm), :]
    y = acc + bdw_ref[...]
    gw = CONV_WIDTH // CONV_GROUPS
    outs = []
    for gi in range(CONV_GROUPS):
        yg = y[:, gi * gw:(gi + 1) * gw]
        mu = jnp.mean(yg, axis=-1, keepdims=True)
        var = jnp.mean(jnp.square(yg - mu), axis=-1, keepdims=True)
        outs.append((yg - mu) * lax.rsqrt(var + 1e-5))
    yn = jnp.concatenate(outs, axis=1) * g_ref[...] + b_ref[...]
    o_ref[...] = (yn * jax.nn.sigmoid(yn)).astype(o_ref.dtype)


def conformer_conv(zc, w_dw, b_dw, g, b, nctx):
    B, T, W2 = zc.shape
    tm = ROW_TILE
    nt = T // tm
    w_pad = jnp.concatenate([w_dw, jnp.zeros((32 - CONV_KERNEL, CONV_WIDTH), F32)], axis=0)
    vec = pl.BlockSpec((1, CONV_WIDTH), lambda bb, i: (0, 0))
    return pl.pallas_call(
        functools.partial(_conv_kernel, nctx=nctx), grid=(B, nt),
        in_specs=[pl.BlockSpec((None, tm, W2), lambda bb, i: (bb, jnp.maximum(i - 1, 0), 0)),
                  pl.BlockSpec((None, tm, W2), lambda bb, i: (bb, i, 0)),
                  pl.BlockSpec((None, tm, W2), lambda bb, i: (bb, jnp.minimum(i + 1, nt - 1), 0)),
                  pl.BlockSpec((32, CONV_WIDTH), lambda bb, i: (0, 0)), vec, vec, vec],
        out_specs=pl.BlockSpec((None, tm, CONV_WIDTH), lambda bb, i: (bb, i, 0)),
        out_shape=jax.ShapeDtypeStruct((B, T, CONV_WIDTH), BF16),
        scratch_shapes=[pltpu.VMEM((tm + 2 * HALO, CONV_WIDTH), F32)],
        compiler_params=_cparams(("parallel", "parallel")), name="conformer_conv",
    )(zc, zc, zc, w_pad, b_dw.reshape(1, -1), g.reshape(1, -1), b.reshape(1, -1))


def _merge_kernel(h_ref, b0_ref, b1_ref, b2_ref, b3_ref, wg_ref, wb_ref, o_ref):
    h = h_ref[...]
    acc = None
    for n, br in enumerate((b0_ref, b1_ref, b2_ref, b3_ref)):
        gate = jax.nn.sigmoid(jnp.dot(h, wg_ref[n], preferred_element_type=F32))
        term = gate * jnp.dot(br[...], wb_ref[n], preferred_element_type=F32)
        acc = term if acc is None else acc + term
    o_ref[...] = acc.astype(o_ref.dtype)


def gated_merge(h, branches, w_gate, w_branch):
    M, D = h.shape
    N = w_gate.shape[2]
    tm = _pick(M, (FLAT_TILE, 256))
    tn = _pick(N, (512, 256))
    br_spec = pl.BlockSpec((tm, BRANCH_W), lambda j, i: (i, 0))
    return pl.pallas_call(
        _merge_kernel, grid=(N // tn, M // tm),
        in_specs=[pl.BlockSpec((tm, D), lambda j, i: (i, 0)), br_spec, br_spec, br_spec, br_spec,
                  pl.BlockSpec((N_BRANCH, D, tn), lambda j, i: (0, 0, j)),
                  pl.BlockSpec((N_BRANCH, BRANCH_W, tn), lambda j, i: (0, 0, j))],
        out_specs=pl.BlockSpec((tm, tn), lambda j, i: (i, j)),
        out_shape=jax.ShapeDtypeStruct((M, N), BF16),
        compiler_params=_cparams(("parallel", "parallel")), name="gated_merge",
    )(h, *branches, w_gate, w_branch)


_NRANK = PEER_TOPK + 1
_STAIRCASE = [(a, b) for a in range(_NRANK) for b in range(_NRANK) if (a + 1) * (b + 1) <= _NRANK]
_NCAND = -(-len(_STAIRCASE) // 8) * 8
_NRANK_PAD = -(-_NRANK // 8) * 8


def _peer_threshold_kernel(q_ref, keys_ref, e1_ref, cut_ref, e2_ref, t1_ref, t2_ref, c_ref):
    def extract(vals, dst_ref):
        v = vals
        m = prev = None
        for r in range(_NRANK):
            prev = m
            m = jnp.max(v, axis=0, keepdims=True)
            if dst_ref is not None:
                dst_ref[r:r + 1, :] = m
            v = jnp.where(v == m, NEG_INF, v)
        return prev, m

    def sub_scores(h, p):
        cols = slice((2 * h + p) * LANES, (2 * h + p + 1) * LANES)
        return lax.dot_general(keys_ref[p], q_ref[:, cols], (((1,), (1,)), ((), ())),
                               preferred_element_type=F32) * LOG2E

    for h in range(PEER_HEADS):
        u1 = sub_scores(h, 0)
        u2 = sub_scores(h, 1)
        extract(u1, t1_ref)
        extract(u2, t2_ref)
        m1 = t1_ref[0:1, :]
        m2 = t2_ref[0:1, :]
        c_ref[...] = jnp.full(c_ref.shape, NEG_INF, F32)
        for r, (a, b) in enumerate(_STAIRCASE):
            c_ref[r:r + 1, :] = (t1_ref[a:a + 1, :] - m1) + (t2_ref[b:b + 1, :] - m2)
        cand = c_ref[...]
        c16, c17 = extract(cand, None)
        thr = 0.5 * (c16 + c17)
        z = jnp.sum(jnp.where(cand >= thr, jnp.exp2(cand), 0.0), axis=0, keepdims=True)
        lz = -jnp.log2(z)
        s1f = (u1 - m1) + lz
        e1_ref[h] = jnp.exp2(s1f)
        cut_ref[h] = jnp.exp2((thr + lz) - s1f)
        e2_ref[h] = jnp.exp2(u2 - m2)


def peer_tables(q, keys):
    M = q.shape[0]
    tl = _pick(M, (256, 128))
    sf_spec = pl.BlockSpec((PEER_HEADS, PEER_NKEYS, tl), lambda i: (0, 0, i))
    sf_shape = jax.ShapeDtypeStruct((PEER_HEADS, PEER_NKEYS, M), F32)
    return pl.pallas_call(
        _peer_threshold_kernel, grid=(M // tl,),
        in_specs=[pl.BlockSpec((tl, q.shape[1]), lambda i: (i, 0)), _full(keys)],
        out_specs=[sf_spec, sf_spec, sf_spec], out_shape=[sf_shape, sf_shape, sf_shape],
        scratch_shapes=[pltpu.VMEM((_NRANK_PAD, tl), F32), pltpu.VMEM((_NRANK_PAD, tl), F32),
                        pltpu.VMEM((_NCAND, tl), F32)],
        compiler_params=_cparams(("parallel",)), name="peer_tables",
    )(q, keys)


PEER_TOKEN_SPLIT = 256
PEER_EXPERT_SPLIT = 512
PEER_EXPERT_CHUNK = 1024
PEER_TOKEN_TILE = 768


def _peer_experts_kernel(xt_ref, u_ref, vt_ref, e1_ref, cut_ref, e2_ref, o_ref, acc_ref, *scratch, ek):
    j = pl.program_id(1)
    tm = xt_ref.shape[1]
    th, es = PEER_TOKEN_SPLIT, PEER_EXPERT_SPLIT
    pieces = [(c, t) for c in range(ek // es) for t in range(tm // th)]
    st_refs, a_refs = scratch[:len(pieces)], scratch[len(pieces):]

    @pl.when(j == 0)
    def _():
        acc_ref[...] = jnp.zeros(acc_ref.shape, F32)

    def pre_activations(p):
        c, t = pieces[p]
        st_refs[p][...] = jnp.dot(u_ref[c * es:(c + 1) * es, :], xt_ref[:, t * th:(t + 1) * th],
                                  preferred_element_type=F32)

    def weights(p):
        c, t = pieces[p]
        for r in range(es // PEER_NKEYS):
            i1 = j * (ek // PEER_NKEYS) + c * (es // PEER_NKEYS) + r
            rows = slice(r * PEER_NKEYS, (r + 1) * PEER_NKEYS)
            half = slice(t * th, (t + 1) * th)
            cut_rows = [cut_ref[h, pl.ds(i1, 1), half] for h in range(PEER_HEADS)]
            e1_rows = [e1_ref[h, pl.ds(i1, 1), half] for h in range(PEER_HEADS)]
            for g in range(th // LANES):
                lanes = slice(t * th + g * LANES, t * th + (g + 1) * LANES)
                sub = slice(g * LANES, (g + 1) * LANES)
                w = None
                for h in range(PEER_HEADS):
                    e2 = e2_ref[h, :, lanes]
                    sel = jnp.where(e2 >= cut_rows[h][:, sub], e2, 0.0)
                    sel = sel * e1_rows[h][:, sub]
                    w = sel if w is None else w + sel
                pre = st_refs[p][rows, g * LANES:(g + 1) * LANES]
                act = pre * (0.5 + 0.5 * lax.erf(pre * (2.0 ** -0.5)))
                a_refs[p][rows, g * LANES:(g + 1) * LANES] = (w * act).astype(BF16)

    def accumulate(p):
        c, t = pieces[p]
        lanes = slice(t * th, (t + 1) * th)
        acc_ref[:, lanes] += jnp.dot(vt_ref[:, c * es:(c + 1) * es], a_refs[p][...], preferred_element_type=F32)

    for p in range(len(pieces)):
        pre_activations(p)
    for p in range(len(pieces)):
        weights(p)
        accumulate(p)

    @pl.when(j == pl.num_programs(1) - 1)
    def _():
        o_ref[...] = acc_ref[...].T


def peer_experts(xt, u, vt, e1, cut, e2):
    D, M = xt.shape
    E = u.shape[0]
    tm = _pick(M, (PEER_TOKEN_TILE, FLAT_TILE, 256))
    ek = vt.shape[2]
    th, es = PEER_TOKEN_SPLIT, PEER_EXPERT_SPLIT
    npieces = (tm // th) * (ek // es)
    table_spec = pl.BlockSpec((PEER_HEADS, PEER_NKEYS, tm), lambda i, j: (0, 0, i), pipeline_mode=pl.Buffered(1))
    return pl.pallas_call(
        functools.partial(_peer_experts_kernel, ek=ek), grid=(M // tm, E // ek),
        in_specs=[pl.BlockSpec((D, tm), lambda i, j: (0, i)),
                  pl.BlockSpec((ek, D), lambda i, j: (j, 0)),
                  pl.BlockSpec((None, D, ek), lambda i, j: (j, 0, 0)),
                  table_spec, table_spec, table_spec],
        out_specs=pl.BlockSpec((tm, D), lambda i, j: (i, 0)),
        out_shape=jax.ShapeDtypeStruct((M, D), F32),
        scratch_shapes=([pltpu.VMEM((D, tm), F32)]
                        + [pltpu.VMEM((es, th), F32)] * npieces + [pltpu.VMEM((es, th), BF16)] * npieces),
        compiler_params=_cparams(("parallel", "arbitrary")), name="peer_experts",
    )(xt, u, vt, e1, cut, e2)


def _final_kernel(x_ref, d_ref, mg_ref, g_ref, o_ref, *, gate_row):
    x = x_ref[...] + mg_ref[gate_row:gate_row + 1, :] * d_ref[...]
    o_ref[...] = x * lax.rsqrt(jnp.mean(x * x, axis=-1, keepdims=True) + EPS) * g_ref[...]


def final_norm(x, delta, mod_gate, g, nctx, gate_row):
    B, T, D = x.shape
    tm = ROW_TILE
    L = T - nctx * tm
    row_spec = pl.BlockSpec((None, tm, D), lambda b, i: (b, i + nctx, 0))
    return pl.pallas_call(
        functools.partial(_final_kernel, gate_row=gate_row), grid=(B, L // tm),
        in_specs=[row_spec, row_spec, pl.BlockSpec((None, 8, D), lambda b, i: (b, 0, 0)),
                  pl.BlockSpec((1, D), lambda b, i: (0, 0))],
        out_specs=pl.BlockSpec((None, tm, D), lambda b, i: (b, i, 0)),
        out_shape=jax.ShapeDtypeStruct((B, L, D), F32),
        compiler_params=_cparams(("parallel", "parallel")), name="final_norm",
    )(x, delta, mod_gate, g.reshape(1, D))


def _rope_tables(L, ctx_len):
    rows = L // GRID_W
    row = jnp.repeat(jnp.arange(rows, dtype=F32), GRID_W)
    col = jnp.tile(jnp.arange(GRID_W, dtype=F32), rows)
    nf = MLA_ROPE // 4
    inv = ROPE_BASE ** (-jnp.arange(nf, dtype=F32) / nf)
    ang = jnp.concatenate([row[:, None] * inv, col[:, None] * inv], axis=-1)
    cos, sin = jnp.cos(ang), jnp.sin(ang)
    cos_t = jnp.concatenate([cos, cos, cos, cos], axis=-1)
    sin_t = jnp.concatenate([-sin, sin, -sin, sin], axis=-1)
    cos_t = jnp.concatenate([jnp.ones((ctx_len, LANES), F32), cos_t], axis=0)
    sin_t = jnp.concatenate([jnp.zeros((ctx_len, LANES), F32), sin_t], axis=0)
    return cos_t, sin_t


def kernel(x, c, ctx, c_ctx, w_ada, b_ada, g_norm1, g_norm2, w_in, g_q_mla, w_uq, g_kv_mla, w_ukv, lam_q1, lam_k1, lam_q2, lam_k2, g_subln, w_dw, b_dw, g_conv_norm, b_conv_norm, w_gate, w_branch, w_out, w_peer_q, peer_keys, peer_u, peer_v, g_final):
    B, L, D = x.shape
    ctx_len = ctx.shape[1]
    depth = w_ada.shape[0]
    T = ctx_len + L
    M = B * T
    assert MLA_ROPE == DIFF_QK and ctx_len % ROW_TILE == 0 and L % ROW_TILE == 0
    nctx = ctx_len // ROW_TILE

    cos_t, sin_t = _rope_tables(L, ctx_len)
    xs = jnp.concatenate([ctx, x], axis=1)

    mods = ada_modulation(jnp.concatenate([c, c_ctx[None, :]], axis=0), w_ada, b_ada)
    mods = mods.reshape(depth, 8, 6, D)[:, :B + 1]
    mods = jnp.concatenate([mods, jnp.zeros((depth, B + 1, 2, D), F32)], axis=2)

    o_c, o_f, o_d = 0, 2 * CONV_WIDTH, 2 * CONV_WIDTH + FNET_WIDTH
    pending = None
    for l in range(depth):
        lam_init = 0.8 - 0.6 * math.exp(-0.3 * l)
        s0, s1, s2 = MLA_Q_RANK, MLA_Q_RANK + MLA_KV_RANK, IN_MLA
        wi = w_in[l]
        w_mla = jnp.concatenate([wi[:, :s2], jnp.zeros((D, LANES - MLA_ROPE), F32)], axis=1).astype(BF16)
        w_diff = wi[:, s2:s2 + IN_DIFF].astype(BF16)
        w_fnet = wi[:, s2 + IN_DIFF:s2 + IN_DIFF + FNET_WIDTH].astype(BF16)
        w_conv = wi[:, s2 + IN_DIFF + FNET_WIDTH:].astype(BF16)
        hd = MLA_NOPE + MLA_ROPE
        w_uq_p = jnp.pad(w_uq[l].reshape(MLA_Q_RANK, MLA_HEADS, hd),
                         ((0, 0), (0, 0), (0, 256 - hd))).reshape(MLA_Q_RANK, MLA_HEADS * 256).astype(BF16)
        wkv = w_ukv[l].reshape(MLA_KV_RANK, MLA_HEADS, MLA_NOPE + MLA_V)
        w_ukv_p = jnp.concatenate([wkv[:, :, :MLA_NOPE].reshape(MLA_KV_RANK, -1),
                                   wkv[:, :, MLA_NOPE:].reshape(MLA_KV_RANK, -1)], axis=1).astype(BF16)
        lam_vecs = jnp.stack([lam_q1[l], lam_k1[l], lam_q2[l], lam_k2[l]], axis=0)

        if pending is None:
            h = norm_modulate(xs, g_norm1[l], mods[l], 0, 1, nctx)
        else:
            xs, h = norm_modulate(xs, g_norm1[l], mods[l], 0, 1, nctx,
                                  delta=pending[0], mod_gate=pending[1], gate_row=5)
        hf = h.reshape(M, D)
        z_mla = matmul(hf, w_mla, BF16, "in_proj_mla").reshape(B, T, -1)
        z_diff = matmul(hf, w_diff, BF16, "in_proj_diff").reshape(B, T, -1)
        z_fnet = matmul(hf, w_fnet, BF16, "in_proj_fnet").reshape(B, T, -1)
        z_conv = matmul(hf, w_conv, BF16, "in_proj_conv").reshape(B, T, -1)

        qm, km, vm, qd, kd, vd = attention_projections(z_mla, z_diff, cos_t, sin_t, g_q_mla[l], g_kv_mla[l],
                                                   w_uq_p, w_ukv_p)
        o_mla = mla_attention(qm, km, vm, ctx_len)
        o_diff = diff_attention(qd, kd, vd, lam_vecs, g_subln[l], lam_init, ctx_len)
        o_fnet = jnp.concatenate([fourier_mix_short(z_fnet[:, :ctx_len]), fourier_mix_long(z_fnet[:, ctx_len:])],
                                 axis=1)
        o_conv = conformer_conv(z_conv, w_dw[l], b_dw[l], g_conv_norm[l], b_conv_norm[l], nctx)

        y = gated_merge(hf, [o.reshape(M, BRANCH_W) for o in (o_mla, o_fnet, o_conv, o_diff)],
                        w_gate[l].astype(BF16), w_branch[l].astype(BF16))
        mix = matmul(y, w_out[l].astype(BF16), F32, "out_proj").reshape(B, T, D)
        xs, h2 = norm_modulate(xs, g_norm2[l], mods[l], 3, 4, nctx, delta=mix, mod_gate=mods[l], gate_row=2)

        h2f = h2.reshape(M, D)
        q = matmul(h2f, w_peer_q[l].astype(BF16), BF16, "peer_query")
        e1, cut, e2 = peer_tables(q, peer_keys[l].astype(BF16))
        peer_vt = jnp.swapaxes(peer_v[l].reshape(-1, PEER_EXPERT_CHUNK, D), 1, 2).astype(BF16)
        peer_out = peer_experts(h2f.T, peer_u[l].astype(BF16), peer_vt, e1, cut, e2)
        pending = (peer_out.reshape(B, T, D), mods[l])

    return final_norm(xs, pending[0], pending[1], g_final, nctx, 5)
```

```python
import functools
import math

import numpy as np
import jax
import jax.numpy as jnp
from jax import lax
from jax.experimental import pallas as pl
from jax.experimental.pallas import tpu as pltpu

F32 = jnp.float32
BF16 = jnp.bfloat16

GRID_W = 64
ROPE_BASE = 10000.0
EPS = 1e-6
MLA_HEADS = 4
MLA_Q_RANK = 384
MLA_KV_RANK = 256
MLA_NOPE = 128
MLA_ROPE = 64
MLA_V = 128
DIFF_HEADS = 4
DIFF_QK = 64
DIFF_V = 128
FNET_GROUPS = 4
FNET_GROUP_DIM = 128
FNET_WIDTH = 512
CONV_WIDTH = 512
CONV_KERNEL = 31
CONV_GROUPS = 4
N_BRANCH = 4
BRANCH_W = 512
IN_MLA = MLA_Q_RANK + MLA_KV_RANK + MLA_ROPE
IN_DIFF = DIFF_HEADS * (4 * DIFF_QK + DIFF_V)
PEER_HEADS = 8
PEER_NKEYS = 128
PEER_EXPERTS = PEER_NKEYS * PEER_NKEYS
PEER_DKEY = 256
PEER_TOPK = 16

LANES = 128
ROW_TILE = 256
FLAT_TILE = 512
VMEM_LIMIT = 56 * 1024 * 1024
NEG_INF = float("-inf")
LOG2E = math.log2(math.e)


def _cparams(sem):
    return pltpu.CompilerParams(dimension_semantics=sem, vmem_limit_bytes=VMEM_LIMIT)


def _pick(n, cands):
    for c in cands:
        if n % c == 0:
            return c
    raise ValueError(f"no tile for {n} in {cands}")


def _ada_kernel(cb_ref, w_ref, b_ref, o_ref, *, rows, tn):
    outs = []
    for r in range(rows):
        a = cb_ref[r]
        a = a * jax.nn.sigmoid(a)
        cols = [jnp.sum(w_ref[:, j * LANES:(j + 1) * LANES] * a, axis=0, keepdims=True)
                for j in range(tn // LANES)]
        outs.append(jnp.concatenate(cols, axis=1) + b_ref[...])
    outs.append(jnp.zeros((8 - rows, tn), F32))
    o_ref[...] = jnp.concatenate(outs, axis=0)


def ada_modulation(cond, w_ada, b_ada):
    rows, d = cond.shape
    depth, _, n = w_ada.shape
    tn = _pick(n, (1024, 512))
    cb = jnp.broadcast_to(cond[:, :, None], (rows, d, LANES))
    return pl.pallas_call(
        functools.partial(_ada_kernel, rows=rows, tn=tn),
        grid=(depth, n // tn),
        in_specs=[pl.BlockSpec((rows, d, LANES), lambda l, j: (0, 0, 0)),
                  pl.BlockSpec((None, d, tn), lambda l, j: (l, 0, j)),
                  pl.BlockSpec((None, 1, tn), lambda l, j: (l, 0, j))],
        out_specs=pl.BlockSpec((None, 8, tn), lambda l, j: (l, 0, j)),
        out_shape=jax.ShapeDtypeStruct((depth, 8, n), F32),
        compiler_params=_cparams(("arbitrary", "arbitrary")),
        name="ada_modulation",
    )(cb, w_ada, b_ada.reshape(depth, 1, n))


def _norm_mod_kernel(*refs, has_delta, gate_row, shift_row, scale_row):
    if has_delta:
        x_ref, d_ref, mg_ref, ms_ref, g_ref, xo_ref, h_ref = refs
        x = x_ref[...] + mg_ref[gate_row:gate_row + 1, :] * d_ref[...].astype(F32)
        xo_ref[...] = x
    else:
        x_ref, ms_ref, g_ref, h_ref = refs
        x = x_ref[...]
    y = x * lax.rsqrt(jnp.mean(x * x, axis=-1, keepdims=True) + EPS) * g_ref[...]
    h = y * (1.0 + ms_ref[scale_row:scale_row + 1, :]) + ms_ref[shift_row:shift_row + 1, :]
    h_ref[...] = h.astype(BF16)


def norm_modulate(x, g, mod_ss, shift_row, scale_row, nctx, delta=None, mod_gate=None, gate_row=None):
    B, T, D = x.shape
    tm = ROW_TILE
    row_spec = pl.BlockSpec((None, tm, D), lambda b, i: (b, i, 0))
    mod_spec = pl.BlockSpec((None, 8, D), lambda b, i: (jnp.where(i < nctx, B, b), 0, 0))
    g_spec = pl.BlockSpec((1, D), lambda b, i: (0, 0))
    h_shape = jax.ShapeDtypeStruct((B, T, D), BF16)
    kern = functools.partial(_norm_mod_kernel, has_delta=delta is not None, gate_row=gate_row,
                             shift_row=shift_row, scale_row=scale_row)
    if delta is None:
        return pl.pallas_call(
            kern, grid=(B, T // tm), in_specs=[row_spec, mod_spec, g_spec], out_specs=row_spec,
            out_shape=h_shape, compiler_params=_cparams(("parallel", "parallel")), name="norm_modulate",
        )(x, mod_ss, g.reshape(1, D))
    return pl.pallas_call(
        kern, grid=(B, T // tm), in_specs=[row_spec, row_spec, mod_spec, mod_spec, g_spec],
        out_specs=[row_spec, row_spec],
        out_shape=[jax.ShapeDtypeStruct((B, T, D), F32), h_shape],
        compiler_params=_cparams(("parallel", "parallel")), name="residual_norm_modulate",
    )(x, delta, mod_gate, mod_ss, g.reshape(1, D))


def _mm_kernel(x_ref, w_ref, o_ref):
    o_ref[...] = jnp.dot(x_ref[...], w_ref[...], preferred_element_type=F32).astype(o_ref.dtype)


def matmul(x, w, out_dtype, name):
    M, K = x.shape
    N = w.shape[1]
    tm = _pick(M, (FLAT_TILE, 256, 128))
    tn = _pick(N, (1024, 768, 512, 256, 128))
    return pl.pallas_call(
        _mm_kernel, grid=(N // tn, M // tm),
        in_specs=[pl.BlockSpec((tm, K), lambda j, i: (i, 0)),
                  pl.BlockSpec((K, tn), lambda j, i: (0, j))],
        out_specs=pl.BlockSpec((tm, tn), lambda j, i: (i, j)),
        out_shape=jax.ShapeDtypeStruct((M, N), out_dtype),
        compiler_params=_cparams(("parallel", "parallel")), name=name,
    )(x, w)


def _rope(v, cos, sin, lane):
    r = jnp.where((lane % 64) < 32, pltpu.roll(v, 96, 1), pltpu.roll(v, 32, 1))
    return v * cos + r * sin


def _proj_kernel(zm_ref, zd_ref, cos_ref, sin_ref, gq_ref, gkv_ref, wuq_ref, wukv_ref,
                 qm_ref, km_ref, vm_ref, qd_ref, kd_ref, vd_ref, *, mla_scale, diff_scale):
    tm = zm_ref.shape[0]
    cos = cos_ref[...]
    sin = sin_ref[...]
    lane = lax.broadcasted_iota(jnp.int32, (tm, LANES), 1)
    ones_col = jnp.where(lane == 0, 1.0, 0.0).astype(BF16)

    def rms(v, g):
        return v * lax.rsqrt(jnp.mean(v * v, axis=-1, keepdims=True) + EPS) * g

    zm = zm_ref[...].astype(F32)
    qn = rms(zm[:, :MLA_Q_RANK], gq_ref[...]).astype(BF16)
    kvn = rms(zm[:, MLA_Q_RANK:MLA_Q_RANK + MLA_KV_RANK], gkv_ref[...]).astype(BF16)
    k_rope = _rope(zm[:, MLA_Q_RANK + MLA_KV_RANK:], cos, sin, lane).astype(BF16)
    q = jnp.dot(qn, wuq_ref[...], preferred_element_type=F32)
    kv = jnp.dot(kvn, wukv_ref[...], preferred_element_type=F32)
    for h in range(MLA_HEADS):
        qm_ref[h, :, 0:LANES] = (q[:, 256 * h:256 * h + LANES] * mla_scale).astype(BF16)
        qm_ref[h, :, LANES:2 * LANES] = (
            _rope(q[:, 256 * h + LANES:256 * h + 2 * LANES], cos, sin, lane) * mla_scale).astype(BF16)
        km_ref[h, :, 0:LANES] = kv[:, LANES * h:LANES * (h + 1)].astype(BF16)
        km_ref[h, :, LANES:2 * LANES] = k_rope
        vm_ref[h, :, 0:LANES] = kv[:, 512 + LANES * h:512 + LANES * (h + 1)].astype(BF16)
        vm_ref[h, :, LANES:2 * LANES] = ones_col
    zd = zd_ref[...].astype(F32)
    for h in range(DIFF_HEADS):
        qd_ref[h] = (_rope(zd[:, LANES * h:LANES * (h + 1)], cos, sin, lane) * diff_scale).astype(BF16)
        kd_ref[h] = _rope(zd[:, 512 + LANES * h:512 + LANES * (h + 1)], cos, sin, lane).astype(BF16)
        vd_ref[h, :, 0:LANES] = zd_ref[:, 1024 + LANES * h:1024 + LANES * (h + 1)]
        vd_ref[h, :, LANES:2 * LANES] = ones_col


def attention_projections(z_mla, z_diff, cos_t, sin_t, g_q, g_kv, w_uq_p, w_ukv_p):
    B, T, _ = z_mla.shape
    tm = ROW_TILE
    H = MLA_HEADS

    def head_spec(w):
        return pl.BlockSpec((None, H, tm, w), lambda b, i: (b, 0, i, 0))

    def head_shape(w):
        return jax.ShapeDtypeStruct((B, H, T, w), BF16)

    return pl.pallas_call(
        functools.partial(_proj_kernel, mla_scale=float((MLA_NOPE + MLA_ROPE) ** -0.5 * LOG2E),
                          diff_scale=float(DIFF_QK ** -0.5 * LOG2E)),
        grid=(B, T // tm),
        in_specs=[pl.BlockSpec((None, tm, z_mla.shape[2]), lambda b, i: (b, i, 0)),
                  pl.BlockSpec((None, tm, z_diff.shape[2]), lambda b, i: (b, i, 0)),
                  pl.BlockSpec((tm, LANES), lambda b, i: (i, 0)),
                  pl.BlockSpec((tm, LANES), lambda b, i: (i, 0)),
                  pl.BlockSpec((1, MLA_Q_RANK), lambda b, i: (0, 0)),
                  pl.BlockSpec((1, MLA_KV_RANK), lambda b, i: (0, 0)),
                  pl.BlockSpec(w_uq_p.shape, lambda b, i: (0, 0)),
                  pl.BlockSpec(w_ukv_p.shape, lambda b, i: (0, 0))],
        out_specs=[head_spec(256), head_spec(256), head_spec(256), head_spec(128), head_spec(128), head_spec(256)],
        out_shape=[head_shape(256), head_shape(256), head_shape(256), head_shape(128), head_shape(128),
                   head_shape(256)],
        compiler_params=_cparams(("parallel", "parallel")), name="attention_projections",
    )(z_mla, z_diff, cos_t, sin_t, g_q.reshape(1, -1), g_kv.reshape(1, -1), w_uq_p, w_ukv_p)


V_EXT = 2 * LANES


def _att_scores(q, k, mask):
    s = lax.dot_general(q, k, (((1,), (1,)), ((), ())), preferred_element_type=F32)
    return s if mask is None else jnp.where(mask, NEG_INF, s)


def _lane_tile(x, width):
    return x if width == LANES else jnp.concatenate([x] * (width // LANES), axis=1)


def _att_update(s, v_ext, m_ref, acc_ref, idx):
    m_prev = m_ref[idx]
    m_new = jnp.maximum(m_prev, jnp.max(s, axis=-1, keepdims=True))
    p = jnp.exp2(s - _lane_tile(m_new, s.shape[1])).astype(BF16)
    alpha = _lane_tile(jnp.exp2(m_prev - m_new), V_EXT)
    acc_ref[idx] = alpha * acc_ref[idx] + jnp.dot(p, v_ext, preferred_element_type=F32)
    m_ref[idx] = m_new


def _att_init(m_ref, acc_ref):
    m_ref[...] = jnp.full(m_ref.shape, NEG_INF, F32)
    acc_ref[...] = jnp.zeros(acc_ref.shape, F32)


def _att_step(streams, mask, m_ref, acc_ref):
    for idx, (q, k, v) in enumerate(streams):
        _att_update(_att_scores(q(), k(), mask), v(), m_ref, acc_ref, idx)


def _att_sweep(step, qi, ki, tq, tk, ctx_len):
    @pl.when(qi == 0)
    def _():
        row = lax.broadcasted_iota(jnp.int32, (tq, tk), 0)
        col = ki * tk + lax.broadcasted_iota(jnp.int32, (tq, tk), 1)
        step((row < ctx_len) & (col >= ctx_len))

    @pl.when(qi != 0)
    def _():
        step(None)


def _att_result(acc):
    return acc[:, :LANES] / acc[:, LANES:LANES + 1]


def _mla_att_kernel(q_ref, k_ref, v_ref, o_ref, m_ref, acc_ref, *, tq, tk, ctx_len):
    qi = pl.program_id(1)
    ki = pl.program_id(2)

    @pl.when(ki == 0)
    def _():
        _att_init(m_ref, acc_ref)

    streams = [(lambda h=h: q_ref[h], lambda h=h: k_ref[h], lambda h=h: v_ref[h]) for h in range(MLA_HEADS)]
    _att_sweep(lambda mask: _att_step(streams, mask, m_ref, acc_ref), qi, ki, tq, tk, ctx_len)

    @pl.when(ki == pl.num_programs(2) - 1)
    def _():
        for h in range(MLA_HEADS):
            o_ref[:, h * MLA_V:(h + 1) * MLA_V] = _att_result(acc_ref[h]).astype(o_ref.dtype)


def _att_specs(H, tq, tk, dk):
    return [pl.BlockSpec((None, H, tq, dk), lambda b, i, j: (b, 0, i, 0)),
            pl.BlockSpec((None, H, tk, dk), lambda b, i, j: (b, 0, j, 0)),
            pl.BlockSpec((None, H, tk, V_EXT), lambda b, i, j: (b, 0, j, 0))]


def _att_scratch(nstreams, tq):
    return [pltpu.VMEM((nstreams, tq, LANES), F32), pltpu.VMEM((nstreams, tq, V_EXT), F32)]


def mla_attention(qm, km, vm, ctx_len):
    B, H, T, dk = qm.shape
    tq = tk = _pick(T, (768, 256))
    assert ctx_len <= tq
    return pl.pallas_call(
        functools.partial(_mla_att_kernel, tq=tq, tk=tk, ctx_len=ctx_len),
        grid=(B, T // tq, T // tk),
        in_specs=_att_specs(H, tq, tk, dk),
        out_specs=pl.BlockSpec((None, tq, H * MLA_V), lambda b, i, j: (b, i, 0)),
        out_shape=jax.ShapeDtypeStruct((B, T, H * MLA_V), BF16),
        scratch_shapes=_att_scratch(H, tq),
        compiler_params=_cparams(("parallel", "parallel", "arbitrary")), name="mla_attention",
    )(qm, km, vm)


def _diff_att_kernel(q_ref, k_ref, v_ref, lam_ref, g_ref, o_ref, m_ref, acc_ref, *, tq, tk, ctx_len, lam_init):
    qi = pl.program_id(1)
    ki = pl.program_id(2)

    @pl.when(ki == 0)
    def _():
        _att_init(m_ref, acc_ref)

    def component(h, c):
        q = q_ref[h]
        first = lax.broadcasted_iota(jnp.int32, q.shape, 1) < DIFF_QK
        return jnp.where(first if c == 0 else jnp.logical_not(first), q, jnp.zeros_like(q))

    streams = [(lambda h=h, c=c: component(h, c), lambda h=h: k_ref[h], lambda h=h: v_ref[h])
               for h in range(DIFF_HEADS) for c in range(2)]
    _att_sweep(lambda mask: _att_step(streams, mask, m_ref, acc_ref), qi, ki, tq, tk, ctx_len)

    @pl.when(ki == pl.num_programs(2) - 1)
    def _():
        lv = lam_ref[...]
        lam = (jnp.exp(jnp.sum(lv[0:1] * lv[1:2], axis=-1, keepdims=True))
               - jnp.exp(jnp.sum(lv[2:3] * lv[3:4], axis=-1, keepdims=True)) + lam_init)
        for h in range(DIFF_HEADS):
            o = _att_result(acc_ref[2 * h]) - lam * _att_result(acc_ref[2 * h + 1])
            o = o * lax.rsqrt(jnp.mean(o * o, axis=-1, keepdims=True) + EPS) * g_ref[...]
            o_ref[:, h * DIFF_V:(h + 1) * DIFF_V] = (o * (1.0 - lam_init)).astype(o_ref.dtype)


def diff_attention(qd, kd, vd, lam_vecs, g_subln, lam_init, ctx_len):
    B, H, T, dk = qd.shape
    tq = tk = _pick(T, (768, 256))
    assert ctx_len <= tq
    return pl.pallas_call(
        functools.partial(_diff_att_kernel, tq=tq, tk=tk, ctx_len=ctx_len, lam_init=lam_init),
        grid=(B, T // tq, T // tk),
        in_specs=_att_specs(H, tq, tk, dk) + [pl.BlockSpec(lam_vecs.shape, lambda b, i, j: (0, 0)),
                                               pl.BlockSpec((1, DIFF_V), lambda b, i, j: (0, 0))],
        out_specs=pl.BlockSpec((None, tq, H * DIFF_V), lambda b, i, j: (b, i, 0)),
        out_shape=jax.ShapeDtypeStruct((B, T, H * DIFF_V), BF16),
        scratch_shapes=_att_scratch(2 * H, tq),
        compiler_params=_cparams(("parallel", "parallel", "arbitrary")), name="diff_attention",
    )(qd, kd, vd, lam_vecs, g_subln.reshape(1, -1))


def _dft_mats(n):
    k = np.arange(n)
    ang = 2.0 * np.pi * ((k[:, None] * k[None, :]) % n) / n
    return jnp.asarray(np.cos(ang), BF16), jnp.asarray(np.sin(ang), BF16)


def _channel_dft(x, cc, sc):
    tr, ti = [], []
    for s in range(x.shape[1] // LANES):
        xs = x[:, s * LANES:(s + 1) * LANES]
        tr.append(jnp.dot(xs, cc, preferred_element_type=F32))
        ti.append(-jnp.dot(xs, sc, preferred_element_type=F32))
    return jnp.concatenate(tr, axis=1), jnp.concatenate(ti, axis=1)


def _fnet_stage1_kernel(x_ref, cc_ref, sc_ref, c_ref, s_ref, twc_ref, tws_ref, yr_ref, yi_ref, *, nb):
    tr, ti = _channel_dft(x_ref[...], cc_ref[...], sc_ref[...])
    trb, tib = tr.astype(BF16), ti.astype(BF16)
    c, s = c_ref[...], s_ref[...]
    yr = jnp.dot(c, trb, preferred_element_type=F32) + jnp.dot(s, tib, preferred_element_type=F32)
    yi = jnp.dot(c, tib, preferred_element_type=F32) - jnp.dot(s, trb, preferred_element_type=F32)
    for i in range(nb):
        twc = jnp.concatenate([twc_ref[i]] * FNET_GROUPS, axis=1)
        tws = jnp.concatenate([tws_ref[i]] * FNET_GROUPS, axis=1)
        a = yr[:, i * FNET_WIDTH:(i + 1) * FNET_WIDTH]
        b = yi[:, i * FNET_WIDTH:(i + 1) * FNET_WIDTH]
        yr_ref[i] = a * twc + b * tws
        yi_ref[i] = b * twc - a * tws


def _fnet_stage2_kernel(yr_ref, yi_ref, c_ref, s_ref, o_ref, *, scale):
    xr = (jnp.dot(c_ref[...], yr_ref[...].astype(BF16), preferred_element_type=F32)
          + jnp.dot(s_ref[...], yi_ref[...].astype(BF16), preferred_element_type=F32))
    o_ref[...] = (xr * scale).astype(o_ref.dtype)


def _fnet_direct_kernel(x_ref, cc_ref, sc_ref, c_ref, s_ref, o_ref, *, scale):
    tr, ti = _channel_dft(x_ref[...], cc_ref[...], sc_ref[...])
    xr = (jnp.dot(c_ref[...], tr.astype(BF16), preferred_element_type=F32)
          + jnp.dot(s_ref[...], ti.astype(BF16), preferred_element_type=F32))
    o_ref[...] = (xr * scale).astype(o_ref.dtype)


def _full(a):
    return pl.BlockSpec(a.shape, lambda *_: (0,) * a.ndim)


def fourier_mix_long(zf):
    B, L, W = zf.shape
    n2 = LANES
    n1 = L // n2
    nb = _pick(n1, (8,))
    cc, sc = _dft_mats(FNET_GROUP_DIM)
    c2, s2 = _dft_mats(n2)
    c1, s1 = _dft_mats(n1)
    ang = 2.0 * np.pi * (np.arange(n1)[:, None] * np.arange(n2)[None, :]) / L
    twc = jnp.asarray(np.broadcast_to(np.cos(ang)[:, :, None], (n1, n2, LANES)), F32)
    tws = jnp.asarray(np.broadcast_to(np.sin(ang)[:, :, None], (n1, n2, LANES)), F32)
    x = zf.reshape(B, n2, n1 * W)
    y_shape = jax.ShapeDtypeStruct((B, n1, n2, W), F32)
    y_spec = pl.BlockSpec((None, nb, n2, W), lambda b, j: (b, j, 0, 0))
    tw_spec = pl.BlockSpec((nb, n2, LANES), lambda b, j: (j, 0, 0))
    yr, yi = pl.pallas_call(
        functools.partial(_fnet_stage1_kernel, nb=nb), grid=(B, n1 // nb),
        in_specs=[pl.BlockSpec((None, n2, nb * W), lambda b, j: (b, 0, j)),
                  _full(cc), _full(sc), _full(c2), _full(s2), tw_spec, tw_spec],
        out_specs=[y_spec, y_spec], out_shape=[y_shape, y_shape],
        compiler_params=_cparams(("parallel", "parallel")), name="fnet_stage1",
    )(x, cc, sc, c2, s2, twc, tws)
    cols = n2 * W
    tn = 4096
    y2_spec = pl.BlockSpec((None, n1, tn), lambda b, j: (b, 0, j))
    out = pl.pallas_call(
        functools.partial(_fnet_stage2_kernel, scale=float((L * FNET_GROUP_DIM) ** -0.5)),
        grid=(B, cols // tn),
        in_specs=[y2_spec, y2_spec, _full(c1), _full(s1)],
        out_specs=y2_spec, out_shape=jax.ShapeDtypeStruct((B, n1, cols), BF16),
        compiler_params=_cparams(("parallel", "parallel")), name="fnet_stage2",
    )(yr.reshape(B, n1, cols), yi.reshape(B, n1, cols), c1, s1)
    return out.reshape(B, L, W)


def fourier_mix_short(zf):
    B, L, W = zf.shape
    cc, sc = _dft_mats(FNET_GROUP_DIM)
    c, s = _dft_mats(L)
    spec = pl.BlockSpec((None, L, W), lambda b: (b, 0, 0))
    return pl.pallas_call(
        functools.partial(_fnet_direct_kernel, scale=float((L * FNET_GROUP_DIM) ** -0.5)), grid=(B,),
        in_specs=[spec, _full(cc), _full(sc), _full(c), _full(s)],
        out_specs=spec, out_shape=jax.ShapeDtypeStruct((B, L, W), BF16),
        compiler_params=_cparams(("parallel",)), name="fnet_direct",
    )(zf, cc, sc, c, s)


HALO = 16


def _conv_kernel(prev_ref, cur_ref, next_ref, w_ref, bdw_ref, g_ref, b_ref, o_ref, ext_ref, *, nctx):
    i = pl.program_id(1)
    nt = pl.num_programs(1)
    tm = cur_ref.shape[0]

    def glu(z):
        z = z.astype(F32)
        return z[:, :CONV_WIDTH] * jax.nn.sigmoid(z[:, CONV_WIDTH:])

    has_prev = jnp.logical_and(i != 0, i != nctx)
    has_next = jnp.logical_and(i != nctx - 1, i != nt - 1)
    ext_ref[0:HALO, :] = jnp.where(has_prev, glu(prev_ref[tm - HALO:tm, :]), 0.0)
    ext_ref[HALO:HALO + tm, :] = glu(cur_ref[...])
    ext_ref[HALO + tm:2 * HALO + tm, :] = jnp.where(has_next, glu(next_ref[0:HALO, :]), 0.0)
    pad = CONV_KERNEL // 2
    acc = jnp.zeros((tm, CONV_WIDTH), F32)
    for k in range(CONV_KERNEL):
        acc = acc + w_ref[k:k + 1, :] * ext_ref[pl.ds(HALO - pad + k, tm), :]
    y = acc + bdw_ref[...]
    gw = CONV_WIDTH // CONV_GROUPS
    outs = []
    for gi in range(CONV_GROUPS):
        yg = y[:, gi * gw:(gi + 1) * gw]
        mu = jnp.mean(yg, axis=-1, keepdims=True)
        var = jnp.mean(jnp.square(yg - mu), axis=-1, keepdims=True)
        outs.append((yg - mu) * lax.rsqrt(var + 1e-5))
    yn = jnp.concatenate(outs, axis=1) * g_ref[...] + b_ref[...]
    o_ref[...] = (yn * jax.nn.sigmoid(yn)).astype(o_ref.dtype)


def conformer_conv(zc, w_dw, b_dw, g, b, nctx):
    B, T, W2 = zc.shape
    tm = ROW_TILE
    nt = T // tm
    w_pad = jnp.concatenate([w_dw, jnp.zeros((32 - CONV_KERNEL, CONV_WIDTH), F32)], axis=0)
    vec = pl.BlockSpec((1, CONV_WIDTH), lambda bb, i: (0, 0))
    return pl.pallas_call(
        functools.partial(_conv_kernel, nctx=nctx), grid=(B, nt),
        in_specs=[pl.BlockSpec((None, tm, W2), lambda bb, i: (bb, jnp.maximum(i - 1, 0), 0)),
                  pl.BlockSpec((None, tm, W2), lambda bb, i: (bb, i, 0)),
                  pl.BlockSpec((None, tm, W2), lambda bb, i: (bb, jnp.minimum(i + 1, nt - 1), 0)),
                  pl.BlockSpec((32, CONV_WIDTH), lambda bb, i: (0, 0)), vec, vec, vec],
        out_specs=pl.BlockSpec((None, tm, CONV_WIDTH), lambda bb, i: (bb, i, 0)),
        out_shape=jax.ShapeDtypeStruct((B, T, CONV_WIDTH), BF16),
        scratch_shapes=[pltpu.VMEM((tm + 2 * HALO, CONV_WIDTH), F32)],
        compiler_params=_cparams(("parallel", "parallel")), name="conformer_conv",
    )(zc, zc, zc, w_pad, b_dw.reshape(1, -1), g.reshape(1, -1), b.reshape(1, -1))


def _merge_kernel(h_ref, b0_ref, b1_ref, b2_ref, b3_ref, wg_ref, wb_ref, o_ref):
    h = h_ref[...]
    acc = None
    for n, br in enumerate((b0_ref, b1_ref, b2_ref, b3_ref)):
        gate = jax.nn.sigmoid(jnp.dot(h, wg_ref[n], preferred_element_type=F32))
        term = gate * jnp.dot(br[...], wb_ref[n], preferred_element_type=F32)
        acc = term if acc is None else acc + term
    o_ref[...] = acc.astype(o_ref.dtype)


def gated_merge(h, branches, w_gate, w_branch):
    M, D = h.shape
    N = w_gate.shape[2]
    tm = _pick(M, (FLAT_TILE, 256))
    tn = _pick(N, (512, 256))
    br_spec = pl.BlockSpec((tm, BRANCH_W), lambda j, i: (i, 0))
    return pl.pallas_call(
        _merge_kernel, grid=(N // tn, M // tm),
        in_specs=[pl.BlockSpec((tm, D), lambda j, i: (i, 0)), br_spec, br_spec, br_spec, br_spec,
                  pl.BlockSpec((N_BRANCH, D, tn), lambda j, i: (0, 0, j)),
                  pl.BlockSpec((N_BRANCH, BRANCH_W, tn), lambda j, i: (0, 0, j))],
        out_specs=pl.BlockSpec((tm, tn), lambda j, i: (i, j)),
        out_shape=jax.ShapeDtypeStruct((M, N), BF16),
        compiler_params=_cparams(("parallel", "parallel")), name="gated_merge",
    )(h, *branches, w_gate, w_branch)


_NRANK = PEER_TOPK + 1
_STAIRCASE = [(a, b) for a in range(_NRANK) for b in range(_NRANK) if (a + 1) * (b + 1) <= _NRANK]
_NCAND = -(-len(_STAIRCASE) // 8) * 8
_NRANK_PAD = -(-_NRANK // 8) * 8


def _peer_threshold_kernel(q_ref, keys_ref, e1_ref, cut_ref, e2_ref, t1_ref, t2_ref, c_ref):
    def extract(vals, dst_ref):
        v = vals
        m = prev = None
        for r in range(_NRANK):
            prev = m
            m = jnp.max(v, axis=0, keepdims=True)
            if dst_ref is not None:
                dst_ref[r:r + 1, :] = m
            v = jnp.where(v == m, NEG_INF, v)
        return prev, m

    def sub_scores(h, p):
        cols = slice((2 * h + p) * LANES, (2 * h + p + 1) * LANES)
        return lax.dot_general(keys_ref[p], q_ref[:, cols], (((1,), (1,)), ((), ())),
                               preferred_element_type=F32) * LOG2E

    for h in range(PEER_HEADS):
        u1 = sub_scores(h, 0)
        u2 = sub_scores(h, 1)
        extract(u1, t1_ref)
        extract(u2, t2_ref)
        m1 = t1_ref[0:1, :]
        m2 = t2_ref[0:1, :]
        c_ref[...] = jnp.full(c_ref.shape, NEG_INF, F32)
        for r, (a, b) in enumerate(_STAIRCASE):
            c_ref[r:r + 1, :] = (t1_ref[a:a + 1, :] - m1) + (t2_ref[b:b + 1, :] - m2)
        cand = c_ref[...]
        c16, c17 = extract(cand, None)
        thr = 0.5 * (c16 + c17)
        z = jnp.sum(jnp.where(cand >= thr, jnp.exp2(cand), 0.0), axis=0, keepdims=True)
        lz = -jnp.log2(z)
        s1f = (u1 - m1) + lz
        e1_ref[h] = jnp.exp2(s1f)
        cut_ref[h] = jnp.exp2((thr + lz) - s1f)
        e2_ref[h] = jnp.exp2(u2 - m2)


def peer_tables(q, keys):
    M = q.shape[0]
    tl = _pick(M, (256, 128))
    sf_spec = pl.BlockSpec((PEER_HEADS, PEER_NKEYS, tl), lambda i: (0, 0, i))
    sf_shape = jax.ShapeDtypeStruct((PEER_HEADS, PEER_NKEYS, M), F32)
    return pl.pallas_call(
        _peer_threshold_kernel, grid=(M // tl,),
        in_specs=[pl.BlockSpec((tl, q.shape[1]), lambda i: (i, 0)), _full(keys)],
        out_specs=[sf_spec, sf_spec, sf_spec], out_shape=[sf_shape, sf_shape, sf_shape],
        scratch_shapes=[pltpu.VMEM((_NRANK_PAD, tl), F32), pltpu.VMEM((_NRANK_PAD, tl), F32),
                        pltpu.VMEM((_NCAND, tl), F32)],
        compiler_params=_cparams(("parallel",)), name="peer_tables",
    )(q, keys)


PEER_TOKEN_SPLIT = 256
PEER_EXPERT_SPLIT = 512
PEER_EXPERT_CHUNK = 1024
PEER_TOKEN_TILE = 768


def _peer_experts_kernel(xt_ref, u_ref, vt_ref, e1_ref, cut_ref, e2_ref, o_ref, acc_ref, *scratch, ek):
    j = pl.program_id(1)
    tm = xt_ref.shape[1]
    th, es = PEER_TOKEN_SPLIT, PEER_EXPERT_SPLIT
    pieces = [(c, t) for c in range(ek // es) for t in range(tm // th)]
    st_refs, a_refs = scratch[:len(pieces)], scratch[len(pieces):]

    @pl.when(j == 0)
    def _():
        acc_ref[...] = jnp.zeros(acc_ref.shape, F32)

    def pre_activations(p):
        c, t = pieces[p]
        st_refs[p][...] = jnp.dot(u_ref[c * es:(c + 1) * es, :], xt_ref[:, t * th:(t + 1) * th],
                                  preferred_element_type=F32)

    def weights(p):
        c, t = pieces[p]
        for r in range(es // PEER_NKEYS):
            i1 = j * (ek // PEER_NKEYS) + c * (es // PEER_NKEYS) + r
            rows = slice(r * PEER_NKEYS, (r + 1) * PEER_NKEYS)
            half = slice(t * th, (t + 1) * th)
            cut_rows = [cut_ref[h, pl.ds(i1, 1), half] for h in range(PEER_HEADS)]
            e1_rows = [e1_ref[h, pl.ds(i1, 1), half] for h in range(PEER_HEADS)]
            for g in range(th // LANES):
                lanes = slice(t * th + g * LANES, t * th + (g + 1) * LANES)
                sub = slice(g * LANES, (g + 1) * LANES)
                w = None
                for h in range(PEER_HEADS):
                    e2 = e2_ref[h, :, lanes]
                    sel = jnp.where(e2 >= cut_rows[h][:, sub], e2, 0.0)
                    sel = sel * e1_rows[h][:, sub]
                    w = sel if w is None else w + sel
                pre = st_refs[p][rows, g * LANES:(g + 1) * LANES]
                act = pre * (1.0 + lax.erf(pre * (2.0 ** -0.5)))
                a_refs[p][rows, g * LANES:(g + 1) * LANES] = (w * act).astype(BF16)

    def accumulate(p):
        c, t = pieces[p]
        lanes = slice(t * th, (t + 1) * th)
        acc_ref[:, lanes] += jnp.dot(vt_ref[:, c * es:(c + 1) * es], a_refs[p][...], preferred_element_type=F32)

    for p in range(len(pieces)):
        pre_activations(p)
    for p in range(len(pieces)):
        weights(p)
        accumulate(p)

    @pl.when(j == pl.num_programs(1) - 1)
    def _():
        o_ref[...] = acc_ref[...].T


def peer_experts(xt, u, vt, e1, cut, e2):
    D, M = xt.shape
    E = u.shape[0]
    tm = _pick(M, (PEER_TOKEN_TILE, FLAT_TILE, 256))
    ek = vt.shape[2]
    th, es = PEER_TOKEN_SPLIT, PEER_EXPERT_SPLIT
    npieces = (tm // th) * (ek // es)
    table_spec = pl.BlockSpec((PEER_HEADS, PEER_NKEYS, tm), lambda i, j: (0, 0, i), pipeline_mode=pl.Buffered(1))
    return pl.pallas_call(
        functools.partial(_peer_experts_kernel, ek=ek), grid=(M // tm, E // ek),
        in_specs=[pl.BlockSpec((D, tm), lambda i, j: (0, i)),
                  pl.BlockSpec((ek, D), lambda i, j: (j, 0)),
                  pl.BlockSpec((None, D, ek), lambda i, j: (j, 0, 0)),
                  table_spec, table_spec, table_spec],
        out_specs=pl.BlockSpec((tm, D), lambda i, j: (i, 0)),
        out_shape=jax.ShapeDtypeStruct((M, D), F32),
        scratch_shapes=([pltpu.VMEM((D, tm), F32)]
                        + [pltpu.VMEM((es, th), F32)] * npieces + [pltpu.VMEM((es, th), BF16)] * npieces),
        compiler_params=_cparams(("parallel", "arbitrary")), name="peer_experts",
    )(xt, u, vt, e1, cut, e2)


def _final_kernel(x_ref, d_ref, mg_ref, g_ref, o_ref, *, gate_row):
    x = x_ref[...] + mg_ref[gate_row:gate_row + 1, :] * d_ref[...]
    o_ref[...] = x * lax.rsqrt(jnp.mean(x * x, axis=-1, keepdims=True) + EPS) * g_ref[...]


def final_norm(x, delta, mod_gate, g, nctx, gate_row):
    B, T, D = x.shape
    tm = ROW_TILE
    L = T - nctx * tm
    row_spec = pl.BlockSpec((None, tm, D), lambda b, i: (b, i + nctx, 0))
    return pl.pallas_call(
        functools.partial(_final_kernel, gate_row=gate_row), grid=(B, L // tm),
        in_specs=[row_spec, row_spec, pl.BlockSpec((None, 8, D), lambda b, i: (b, 0, 0)),
                  pl.BlockSpec((1, D), lambda b, i: (0, 0))],
        out_specs=pl.BlockSpec((None, tm, D), lambda b, i: (b, i, 0)),
        out_shape=jax.ShapeDtypeStruct((B, L, D), F32),
        compiler_params=_cparams(("parallel", "parallel")), name="final_norm",
    )(x, delta, mod_gate, g.reshape(1, D))


def _rope_tables(L, ctx_len):
    rows = L // GRID_W
    row = jnp.repeat(jnp.arange(rows, dtype=F32), GRID_W)
    col = jnp.tile(jnp.arange(GRID_W, dtype=F32), rows)
    nf = MLA_ROPE // 4
    inv = ROPE_BASE ** (-jnp.arange(nf, dtype=F32) / nf)
    ang = jnp.concatenate([row[:, None] * inv, col[:, None] * inv], axis=-1)
    cos, sin = jnp.cos(ang), jnp.sin(ang)
    cos_t = jnp.concatenate([cos, cos, cos, cos], axis=-1)
    sin_t = jnp.concatenate([-sin, sin, -sin, sin], axis=-1)
    cos_t = jnp.concatenate([jnp.ones((ctx_len, LANES), F32), cos_t], axis=0)
    sin_t = jnp.concatenate([jnp.zeros((ctx_len, LANES), F32), sin_t], axis=0)
    return cos_t, sin_t


def kernel(x, c, ctx, c_ctx, w_ada, b_ada, g_norm1, g_norm2, w_in, g_q_mla, w_uq, g_kv_mla, w_ukv, lam_q1, lam_k1, lam_q2, lam_k2, g_subln, w_dw, b_dw, g_conv_norm, b_conv_norm, w_gate, w_branch, w_out, w_peer_q, peer_keys, peer_u, peer_v, g_final):
    B, L, D = x.shape
    ctx_len = ctx.shape[1]
    depth = w_ada.shape[0]
    T = ctx_len + L
    M = B * T
    assert MLA_ROPE == DIFF_QK and ctx_len % ROW_TILE == 0 and L % ROW_TILE == 0
    nctx = ctx_len // ROW_TILE

    cos_t, sin_t = _rope_tables(L, ctx_len)
    xs = jnp.concatenate([ctx, x], axis=1)

    mods = ada_modulation(jnp.concatenate([c, c_ctx[None, :]], axis=0), w_ada, b_ada)
    mods = mods.reshape(depth, 8, 6, D)[:, :B + 1]
    mods = jnp.concatenate([mods, jnp.zeros((depth, B + 1, 2, D), F32)], axis=2)

    pending = None
    for l in range(depth):
        lam_init = 0.8 - 0.6 * math.exp(-0.3 * l)
        s2 = IN_MLA
        wi = w_in[l]
        w_mla = jnp.concatenate([wi[:, :s2], jnp.zeros((D, LANES - MLA_ROPE), F32)], axis=1).astype(BF16)
        w_diff = wi[:, s2:s2 + IN_DIFF].astype(BF16)
        w_fnet = wi[:, s2 + IN_DIFF:s2 + IN_DIFF + FNET_WIDTH].astype(BF16)
        w_conv = wi[:, s2 + IN_DIFF + FNET_WIDTH:].astype(BF16)
        hd = MLA_NOPE + MLA_ROPE
        w_uq_p = jnp.pad(w_uq[l].reshape(MLA_Q_RANK, MLA_HEADS, hd),
                         ((0, 0), (0, 0), (0, 256 - hd))).reshape(MLA_Q_RANK, MLA_HEADS * 256).astype(BF16)
        wkv = w_ukv[l].reshape(MLA_KV_RANK, MLA_HEADS, MLA_NOPE + MLA_V)
        w_ukv_p = jnp.concatenate([wkv[:, :, :MLA_NOPE].reshape(MLA_KV_RANK, -1),
                                   wkv[:, :, MLA_NOPE:].reshape(MLA_KV_RANK, -1)], axis=1).astype(BF16)
        lam_vecs = jnp.stack([lam_q1[l], lam_k1[l], lam_q2[l], lam_k2[l]], axis=0)

        if pending is None:
            h = norm_modulate(xs, g_norm1[l], mods[l], 0, 1, nctx)
        else:
            xs, h = norm_modulate(xs, g_norm1[l], mods[l], 0, 1, nctx,
                                  delta=pending[0], mod_gate=pending[1], gate_row=5)
        hf = h.reshape(M, D)
        z_mla = matmul(hf, w_mla, BF16, "in_proj_mla").reshape(B, T, -1)
        z_diff = matmul(hf, w_diff, BF16, "in_proj_diff").reshape(B, T, -1)
        z_fnet = matmul(hf, w_fnet, BF16, "in_proj_fnet").reshape(B, T, -1)
        z_conv = matmul(hf, w_conv, BF16, "in_proj_conv").reshape(B, T, -1)

        qm, km, vm, qd, kd, vd = attention_projections(z_mla, z_diff, cos_t, sin_t, g_q_mla[l], g_kv_mla[l],
                                                   w_uq_p, w_ukv_p)
        o_mla = mla_attention(qm, km, vm, ctx_len)
        o_diff = diff_attention(qd, kd, vd, lam_vecs, g_subln[l], lam_init, ctx_len)
        o_fnet = jnp.concatenate([fourier_mix_short(z_fnet[:, :ctx_len]), fourier_mix_long(z_fnet[:, ctx_len:])],
                                 axis=1)
        o_conv = conformer_conv(z_conv, w_dw[l], b_dw[l], g_conv_norm[l], b_conv_norm[l], nctx)

        y = gated_merge(hf, [o.reshape(M, BRANCH_W) for o in (o_mla, o_fnet, o_conv, o_diff)],
                        w_gate[l].astype(BF16), w_branch[l].astype(BF16))
        mix = matmul(y, w_out[l].astype(BF16), F32, "out_proj").reshape(B, T, D)
        xs, h2 = norm_modulate(xs, g_norm2[l], mods[l], 3, 4, nctx, delta=mix, mod_gate=mods[l], gate_row=2)

        h2f = h2.reshape(M, D)
        q = matmul(h2f, w_peer_q[l].astype(BF16), BF16, "peer_query")
        e1, cut, e2 = peer_tables(q, peer_keys[l].astype(BF16))
        peer_vt = jnp.swapaxes((0.5 * peer_v[l]).reshape(-1, PEER_EXPERT_CHUNK, D), 1, 2).astype(BF16)
        peer_out = peer_experts(h2f.T, peer_u[l].astype(BF16), peer_vt, e1, cut, e2)
        pending = (peer_out.reshape(B, T, D), mods[l])

    return final_norm(xs, pending[0], pending[1], g_final, nctx, 5)
```

```python
import functools
import math

import numpy as np
import jax
import jax.numpy as jnp
from jax import lax
from jax.experimental import pallas as pl
from jax.experimental.pallas import tpu as pltpu

F32 = jnp.float32
BF16 = jnp.bfloat16

GRID_W = 64
ROPE_BASE = 10000.0
EPS = 1e-6
MLA_HEADS = 4
MLA_Q_RANK = 384
MLA_KV_RANK = 256
MLA_NOPE = 128
MLA_ROPE = 64
MLA_V = 128
DIFF_HEADS = 4
DIFF_QK = 64
DIFF_V = 128
FNET_GROUPS = 4
FNET_GROUP_DIM = 128
FNET_WIDTH = 512
CONV_WIDTH = 512
CONV_KERNEL = 31
CONV_GROUPS = 4
N_BRANCH = 4
BRANCH_W = 512
IN_MLA = MLA_Q_RANK + MLA_KV_RANK + MLA_ROPE
IN_DIFF = DIFF_HEADS * (4 * DIFF_QK + DIFF_V)
PEER_HEADS = 8
PEER_NKEYS = 128
PEER_EXPERTS = PEER_NKEYS * PEER_NKEYS
PEER_DKEY = 256
PEER_TOPK = 16

LANES = 128
ROW_TILE = 256
FLAT_TILE = 512
VMEM_LIMIT = 56 * 1024 * 1024
NEG_INF = float("-inf")
LOG2E = math.log2(math.e)


def _cparams(sem):
    return pltpu.CompilerParams(dimension_semantics=sem, vmem_limit_bytes=VMEM_LIMIT)


def _pick(n, cands):
    for c in cands:
        if n % c == 0:
            return c
    raise ValueError(f"no tile for {n} in {cands}")


def _ada_kernel(cb_ref, w_ref, b_ref, o_ref, *, rows, tn):
    outs = []
    for r in range(rows):
        a = cb_ref[r]
        a = a * jax.nn.sigmoid(a)
        cols = [jnp.sum(w_ref[:, j * LANES:(j + 1) * LANES] * a, axis=0, keepdims=True)
                for j in range(tn // LANES)]
        outs.append(jnp.concatenate(cols, axis=1) + b_ref[...])
    outs.append(jnp.zeros((8 - rows, tn), F32))
    o_ref[...] = jnp.concatenate(outs, axis=0)


def ada_modulation(cond, w_ada, b_ada):
    rows, d = cond.shape
    depth, _, n = w_ada.shape
    tn = _pick(n, (1024, 512))
    cb = jnp.broadcast_to(cond[:, :, None], (rows, d, LANES))
    return pl.pallas_call(
        functools.partial(_ada_kernel, rows=rows, tn=tn),
        grid=(depth, n // tn),
        in_specs=[pl.BlockSpec((rows, d, LANES), lambda l, j: (0, 0, 0)),
                  pl.BlockSpec((None, d, tn), lambda l, j: (l, 0, j)),
                  pl.BlockSpec((None, 1, tn), lambda l, j: (l, 0, j))],
        out_specs=pl.BlockSpec((None, 8, tn), lambda l, j: (l, 0, j)),
        out_shape=jax.ShapeDtypeStruct((depth, 8, n), F32),
        compiler_params=_cparams(("arbitrary", "arbitrary")),
        name="ada_modulation",
    )(cb, w_ada, b_ada.reshape(depth, 1, n))


def _norm_mod_kernel(*refs, has_delta, gate_row, shift_row, scale_row):
    if has_delta:
        x_ref, d_ref, mg_ref, ms_ref, g_ref, xo_ref, h_ref = refs
        x = x_ref[...] + mg_ref[gate_row:gate_row + 1, :] * d_ref[...].astype(F32)
        xo_ref[...] = x
    else:
        x_ref, ms_ref, g_ref, h_ref = refs
        x = x_ref[...]
    y = x * lax.rsqrt(jnp.mean(x * x, axis=-1, keepdims=True) + EPS) * g_ref[...]
    h = y * (1.0 + ms_ref[scale_row:scale_row + 1, :]) + ms_ref[shift_row:shift_row + 1, :]
    h_ref[...] = h.astype(BF16)


def norm_modulate(x, g, mod_ss, shift_row, scale_row, nctx, delta=None, mod_gate=None, gate_row=None):
    B, T, D = x.shape
    tm = ROW_TILE
    row_spec = pl.BlockSpec((None, tm, D), lambda b, i: (b, i, 0))
    mod_spec = pl.BlockSpec((None, 8, D), lambda b, i: (jnp.where(i < nctx, B, b), 0, 0))
    g_spec = pl.BlockSpec((1, D), lambda b, i: (0, 0))
    h_shape = jax.ShapeDtypeStruct((B, T, D), BF16)
    kern = functools.partial(_norm_mod_kernel, has_delta=delta is not None, gate_row=gate_row,
                             shift_row=shift_row, scale_row=scale_row)
    if delta is None:
        return pl.pallas_call(
            kern, grid=(B, T // tm), in_specs=[row_spec, mod_spec, g_spec], out_specs=row_spec,
            out_shape=h_shape, compiler_params=_cparams(("parallel", "parallel")), name="norm_modulate",
        )(x, mod_ss, g.reshape(1, D))
    return pl.pallas_call(
        kern, grid=(B, T // tm), in_specs=[row_spec, row_spec, mod_spec, mod_spec, g_spec],
        out_specs=[row_spec, row_spec],
        out_shape=[jax.ShapeDtypeStruct((B, T, D), F32), h_shape],
        compiler_params=_cparams(("parallel", "parallel")), name="residual_norm_modulate",
    )(x, delta, mod_gate, mod_ss, g.reshape(1, D))


def _mm_kernel(x_ref, w_ref, o_ref):
    o_ref[...] = jnp.dot(x_ref[...], w_ref[...], preferred_element_type=F32).astype(o_ref.dtype)


def matmul(x, w, out_dtype, name):
    M, K = x.shape
    N = w.shape[1]
    tm = _pick(M, (FLAT_TILE, 256, 128))
    tn = _pick(N, (1024, 768, 512, 256, 128))
    return pl.pallas_call(
        _mm_kernel, grid=(N // tn, M // tm),
        in_specs=[pl.BlockSpec((tm, K), lambda j, i: (i, 0)),
                  pl.BlockSpec((K, tn), lambda j, i: (0, j))],
        out_specs=pl.BlockSpec((tm, tn), lambda j, i: (i, j)),
        out_shape=jax.ShapeDtypeStruct((M, N), out_dtype),
        compiler_params=_cparams(("parallel", "parallel")), name=name,
    )(x, w)


def _rope(v, cos, sin, lane):
    r = jnp.where((lane % 64) < 32, pltpu.roll(v, 96, 1), pltpu.roll(v, 32, 1))
    return v * cos + r * sin


def _proj_kernel(zm_ref, zd_ref, cos_ref, sin_ref, gq_ref, gkv_ref, wuq_ref, wukv_ref,
                 qm_ref, km_ref, vm_ref, qd_ref, kd_ref, vd_ref, *, mla_scale, diff_scale):
    tm = zm_ref.shape[0]
    cos = cos_ref[...]
    sin = sin_ref[...]
    lane = lax.broadcasted_iota(jnp.int32, (tm, LANES), 1)
    ones_col = jnp.where(lane == 0, 1.0, 0.0).astype(BF16)

    def rms(v, g):
        return v * lax.rsqrt(jnp.mean(v * v, axis=-1, keepdims=True) + EPS) * g

    zm = zm_ref[...].astype(F32)
    qn = rms(zm[:, :MLA_Q_RANK], gq_ref[...]).astype(BF16)
    kvn = rms(zm[:, MLA_Q_RANK:MLA_Q_RANK + MLA_KV_RANK], gkv_ref[...]).astype(BF16)
    k_rope = _rope(zm[:, MLA_Q_RANK + MLA_KV_RANK:], cos, sin, lane).astype(BF16)
    q = jnp.dot(qn, wuq_ref[...], preferred_element_type=F32)
    kv = jnp.dot(kvn, wukv_ref[...], preferred_element_type=F32)
    for h in range(MLA_HEADS):
        qm_ref[h, :, 0:LANES] = (q[:, 256 * h:256 * h + LANES] * mla_scale).astype(BF16)
        qm_ref[h, :, LANES:2 * LANES] = (
            _rope(q[:, 256 * h + LANES:256 * h + 2 * LANES], cos, sin, lane) * mla_scale).astype(BF16)
        km_ref[h, :, 0:LANES] = kv[:, LANES * h:LANES * (h + 1)].astype(BF16)
        km_ref[h, :, LANES:2 * LANES] = k_rope
        vm_ref[h, :, 0:LANES] = kv[:, 512 + LANES * h:512 + LANES * (h + 1)].astype(BF16)
        vm_ref[h, :, LANES:2 * LANES] = ones_col
    zd = zd_ref[...].astype(F32)
    for h in range(DIFF_HEADS):
        qd_ref[h] = (_rope(zd[:, LANES * h:LANES * (h + 1)], cos, sin, lane) * diff_scale).astype(BF16)
        kd_ref[h] = _rope(zd[:, 512 + LANES * h:512 + LANES * (h + 1)], cos, sin, lane).astype(BF16)
        vd_ref[h, :, 0:LANES] = zd_ref[:, 1024 + LANES * h:1024 + LANES * (h + 1)]
        vd_ref[h, :, LANES:2 * LANES] = ones_col


def attention_projections(z_mla, z_diff, cos_t, sin_t, g_q, g_kv, w_uq_p, w_ukv_p):
    B, T, _ = z_mla.shape
    tm = ROW_TILE
    H = MLA_HEADS

    def head_spec(w):
        return pl.BlockSpec((None, H, tm, w), lambda b, i: (b, 0, i, 0))

    def head_shape(w):
        return jax.ShapeDtypeStruct((B, H, T, w), BF16)

    return pl.pallas_call(
        functools.partial(_proj_kernel, mla_scale=float((MLA_NOPE + MLA_ROPE) ** -0.5 * LOG2E),
                          diff_scale=float(DIFF_QK ** -0.5 * LOG2E)),
        grid=(B, T // tm),
        in_specs=[pl.BlockSpec((None, tm, z_mla.shape[2]), lambda b, i: (b, i, 0)),
                  pl.BlockSpec((None, tm, z_diff.shape[2]), lambda b, i: (b, i, 0)),
                  pl.BlockSpec((tm, LANES), lambda b, i: (i, 0)),
                  pl.BlockSpec((tm, LANES), lambda b, i: (i, 0)),
                  pl.BlockSpec((1, MLA_Q_RANK), lambda b, i: (0, 0)),
                  pl.BlockSpec((1, MLA_KV_RANK), lambda b, i: (0, 0)),
                  pl.BlockSpec(w_uq_p.shape, lambda b, i: (0, 0)),
                  pl.BlockSpec(w_ukv_p.shape, lambda b, i: (0, 0))],
        out_specs=[head_spec(256), head_spec(256), head_spec(256), head_spec(128), head_spec(128), head_spec(256)],
        out_shape=[head_shape(256), head_shape(256), head_shape(256), head_shape(128), head_shape(128),
                   head_shape(256)],
        compiler_params=_cparams(("parallel", "parallel")), name="attention_projections",
    )(z_mla, z_diff, cos_t, sin_t, g_q.reshape(1, -1), g_kv.reshape(1, -1), w_uq_p, w_ukv_p)


V_EXT = 2 * LANES


def _att_scores(q, k, mask):
    s = lax.dot_general(q, k, (((1,), (1,)), ((), ())), preferred_element_type=F32)
    return s if mask is None else jnp.where(mask, NEG_INF, s)


def _lane_tile(x, width):
    return x if width == LANES else jnp.concatenate([x] * (width // LANES), axis=1)


def _att_update(s, v_ext, m_ref, acc_ref, idx):
    m_prev = m_ref[idx]
    m_new = jnp.maximum(m_prev, jnp.max(s, axis=-1, keepdims=True))
    p = jnp.exp2(s - _lane_tile(m_new, s.shape[1])).astype(BF16)
    alpha = _lane_tile(jnp.exp2(m_prev - m_new), V_EXT)
    acc_ref[idx] = alpha * acc_ref[idx] + jnp.dot(p, v_ext, preferred_element_type=F32)
    m_ref[idx] = m_new


def _att_init(m_ref, acc_ref):
    m_ref[...] = jnp.full(m_ref.shape, NEG_INF, F32)
    acc_ref[...] = jnp.zeros(acc_ref.shape, F32)


def _att_step(streams, mask, m_ref, acc_ref):
    for idx, (q, k, v) in enumerate(streams):
        _att_update(_att_scores(q(), k(), mask), v(), m_ref, acc_ref, idx)


def _att_sweep(step, qi, ki, tq, tk, ctx_len):
    @pl.when(qi == 0)
    def _():
        row = lax.broadcasted_iota(jnp.int32, (tq, tk), 0)
        col = ki * tk + lax.broadcasted_iota(jnp.int32, (tq, tk), 1)
        step((row < ctx_len) & (col >= ctx_len))

    @pl.when(qi != 0)
    def _():
        step(None)


def _att_result(acc):
    return acc[:, :LANES] / acc[:, LANES:LANES + 1]


def _mla_att_kernel(q_ref, k_ref, v_ref, o_ref, m_ref, acc_ref, *, tq, tk, ctx_len):
    qi = pl.program_id(1)
    ki = pl.program_id(2)

    @pl.when(ki == 0)
    def _():
        _att_init(m_ref, acc_ref)

    streams = [(lambda h=h: q_ref[h], lambda h=h: k_ref[h], lambda h=h: v_ref[h]) for h in range(MLA_HEADS)]
    _att_sweep(lambda mask: _att_step(streams, mask, m_ref, acc_ref), qi, ki, tq, tk, ctx_len)

    @pl.when(ki == pl.num_programs(2) - 1)
    def _():
        for h in range(MLA_HEADS):
            o_ref[:, h * MLA_V:(h + 1) * MLA_V] = _att_result(acc_ref[h]).astype(o_ref.dtype)


def _att_specs(H, tq, tk, dk):
    return [pl.BlockSpec((None, H, tq, dk), lambda b, i, j: (b, 0, i, 0)),
            pl.BlockSpec((None, H, tk, dk), lambda b, i, j: (b, 0, j, 0)),
            pl.BlockSpec((None, H, tk, V_EXT), lambda b, i, j: (b, 0, j, 0))]


def _att_scratch(nstreams, tq):
    return [pltpu.VMEM((nstreams, tq, LANES), F32), pltpu.VMEM((nstreams, tq, V_EXT), F32)]


def mla_attention(qm, km, vm, ctx_len):
    B, H, T, dk = qm.shape
    tq = tk = _pick(T, (768, 256))
    assert ctx_len <= tq
    return pl.pallas_call(
        functools.partial(_mla_att_kernel, tq=tq, tk=tk, ctx_len=ctx_len),
        grid=(B, T // tq, T // tk),
        in_specs=_att_specs(H, tq, tk, dk),
        out_specs=pl.BlockSpec((None, tq, H * MLA_V), lambda b, i, j: (b, i, 0)),
        out_shape=jax.ShapeDtypeStruct((B, T, H * MLA_V), BF16),
        scratch_shapes=_att_scratch(H, tq),
        compiler_params=_cparams(("parallel", "parallel", "arbitrary")), name="mla_attention",
    )(qm, km, vm)


def _diff_att_kernel(q_ref, k_ref, v_ref, lam_ref, g_ref, o_ref, m_ref, acc_ref, *, tq, tk, ctx_len, lam_init):
    qi = pl.program_id(1)
    ki = pl.program_id(2)

    @pl.when(ki == 0)
    def _():
        _att_init(m_ref, acc_ref)

    def component(h, c):
        q = q_ref[h]
        first = lax.broadcasted_iota(jnp.int32, q.shape, 1) < DIFF_QK
        return jnp.where(first if c == 0 else jnp.logical_not(first), q, jnp.zeros_like(q))

    streams = [(lambda h=h, c=c: component(h, c), lambda h=h: k_ref[h], lambda h=h: v_ref[h])
               for h in range(DIFF_HEADS) for c in range(2)]
    _att_sweep(lambda mask: _att_step(streams, mask, m_ref, acc_ref), qi, ki, tq, tk, ctx_len)

    @pl.when(ki == pl.num_programs(2) - 1)
    def _():
        lv = lam_ref[...]
        lam = (jnp.exp(jnp.sum(lv[0:1] * lv[1:2], axis=-1, keepdims=True))
               - jnp.exp(jnp.sum(lv[2:3] * lv[3:4], axis=-1, keepdims=True)) + lam_init)
        for h in range(DIFF_HEADS):
            o = _att_result(acc_ref[2 * h]) - lam * _att_result(acc_ref[2 * h + 1])
            o = o * lax.rsqrt(jnp.mean(o * o, axis=-1, keepdims=True) + EPS) * g_ref[...]
            o_ref[:, h * DIFF_V:(h + 1) * DIFF_V] = (o * (1.0 - lam_init)).astype(o_ref.dtype)


def diff_attention(qd, kd, vd, lam_vecs, g_subln, lam_init, ctx_len):
    B, H, T, dk = qd.shape
    tq = tk = _pick(T, (768, 256))
    assert ctx_len <= tq
    return pl.pallas_call(
        functools.partial(_diff_att_kernel, tq=tq, tk=tk, ctx_len=ctx_len, lam_init=lam_init),
        grid=(B, T // tq, T // tk),
        in_specs=_att_specs(H, tq, tk, dk) + [pl.BlockSpec(lam_vecs.shape, lambda b, i, j: (0, 0)),
                                               pl.BlockSpec((1, DIFF_V), lambda b, i, j: (0, 0))],
        out_specs=pl.BlockSpec((None, tq, H * DIFF_V), lambda b, i, j: (b, i, 0)),
        out_shape=jax.ShapeDtypeStruct((B, T, H * DIFF_V), BF16),
        scratch_shapes=_att_scratch(2 * H, tq),
        compiler_params=_cparams(("parallel", "parallel", "arbitrary")), name="diff_attention",
    )(qd, kd, vd, lam_vecs, g_subln.reshape(1, -1))


def _dft_mats(n):
    k = np.arange(n)
    ang = 2.0 * np.pi * ((k[:, None] * k[None, :]) % n) / n
    return jnp.asarray(np.cos(ang), BF16), jnp.asarray(np.sin(ang), BF16)


def _channel_dft(x, cc, sc):
    tr, ti = [], []
    for s in range(x.shape[1] // LANES):
        xs = x[:, s * LANES:(s + 1) * LANES]
        tr.append(jnp.dot(xs, cc, preferred_element_type=F32))
        ti.append(-jnp.dot(xs, sc, preferred_element_type=F32))
    return jnp.concatenate(tr, axis=1), jnp.concatenate(ti, axis=1)


def _fnet_stage1_kernel(x_ref, cc_ref, sc_ref, c_ref, s_ref, twc_ref, tws_ref, yr_ref, yi_ref, *, nb):
    tr, ti = _channel_dft(x_ref[...], cc_ref[...], sc_ref[...])
    trb, tib = tr.astype(BF16), ti.astype(BF16)
    c, s = c_ref[...], s_ref[...]
    yr = jnp.dot(c, trb, preferred_element_type=F32) + jnp.dot(s, tib, preferred_element_type=F32)
    yi = jnp.dot(c, tib, preferred_element_type=F32) - jnp.dot(s, trb, preferred_element_type=F32)
    for i in range(nb):
        twc = jnp.concatenate([twc_ref[i]] * FNET_GROUPS, axis=1)
        tws = jnp.concatenate([tws_ref[i]] * FNET_GROUPS, axis=1)
        a = yr[:, i * FNET_WIDTH:(i + 1) * FNET_WIDTH]
        b = yi[:, i * FNET_WIDTH:(i + 1) * FNET_WIDTH]
        yr_ref[i] = a * twc + b * tws
        yi_ref[i] = b * twc - a * tws


def _fnet_stage2_kernel(yr_ref, yi_ref, c_ref, s_ref, o_ref, *, scale):
    xr = (jnp.dot(c_ref[...], yr_ref[...].astype(BF16), preferred_element_type=F32)
          + jnp.dot(s_ref[...], yi_ref[...].astype(BF16), preferred_element_type=F32))
    o_ref[...] = (xr * scale).astype(o_ref.dtype)


def _fnet_direct_kernel(x_ref, cc_ref, sc_ref, c_ref, s_ref, o_ref, *, scale):
    tr, ti = _channel_dft(x_ref[...], cc_ref[...], sc_ref[...])
    xr = (jnp.dot(c_ref[...], tr.astype(BF16), preferred_element_type=F32)
          + jnp.dot(s_ref[...], ti.astype(BF16), preferred_element_type=F32))
    o_ref[...] = (xr * scale).astype(o_ref.dtype)


def _full(a):
    return pl.BlockSpec(a.shape, lambda *_: (0,) * a.ndim)


def fourier_mix_long(zf):
    B, L, W = zf.shape
    n2 = LANES
    n1 = L // n2
    nb = _pick(n1, (8,))
    cc, sc = _dft_mats(FNET_GROUP_DIM)
    c2, s2 = _dft_mats(n2)
    c1, s1 = _dft_mats(n1)
    ang = 2.0 * np.pi * (np.arange(n1)[:, None] * np.arange(n2)[None, :]) / L
    twc = jnp.asarray(np.broadcast_to(np.cos(ang)[:, :, None], (n1, n2, LANES)), F32)
    tws = jnp.asarray(np.broadcast_to(np.sin(ang)[:, :, None], (n1, n2, LANES)), F32)
    x = zf.reshape(B, n2, n1 * W)
    y_shape = jax.ShapeDtypeStruct((B, n1, n2, W), F32)
    y_spec = pl.BlockSpec((None, nb, n2, W), lambda b, j: (b, j, 0, 0))
    tw_spec = pl.BlockSpec((nb, n2, LANES), lambda b, j: (j, 0, 0))
    yr, yi = pl.pallas_call(
        functools.partial(_fnet_stage1_kernel, nb=nb), grid=(B, n1 // nb),
        in_specs=[pl.BlockSpec((None, n2, nb * W), lambda b, j: (b, 0, j)),
                  _full(cc), _full(sc), _full(c2), _full(s2), tw_spec, tw_spec],
        out_specs=[y_spec, y_spec], out_shape=[y_shape, y_shape],
        compiler_params=_cparams(("parallel", "parallel")), name="fnet_stage1",
    )(x, cc, sc, c2, s2, twc, tws)
    cols = n2 * W
    tn = 4096
    y2_spec = pl.BlockSpec((None, n1, tn), lambda b, j: (b, 0, j))
    out = pl.pallas_call(
        functools.partial(_fnet_stage2_kernel, scale=float((L * FNET_GROUP_DIM) ** -0.5)),
        grid=(B, cols // tn),
        in_specs=[y2_spec, y2_spec, _full(c1), _full(s1)],
        out_specs=y2_spec, out_shape=jax.ShapeDtypeStruct((B, n1, cols), BF16),
        compiler_params=_cparams(("parallel", "parallel")), name="fnet_stage2",
    )(yr.reshape(B, n1, cols), yi.reshape(B, n1, cols), c1, s1)
    return out.reshape(B, L, W)


def fourier_mix_short(zf):
    B, L, W = zf.shape
    cc, sc = _dft_mats(FNET_GROUP_DIM)
    c, s = _dft_mats(L)
    spec = pl.BlockSpec((None, L, W), lambda b: (b, 0, 0))
    return pl.pallas_call(
        functools.partial(_fnet_direct_kernel, scale=float((L * FNET_GROUP_DIM) ** -0.5)), grid=(B,),
        in_specs=[spec, _full(cc), _full(sc), _full(c), _full(s)],
        out_specs=spec, out_shape=jax.ShapeDtypeStruct((B, L, W), BF16),
        compiler_params=_cparams(("parallel",)), name="fnet_direct",
    )(zf, cc, sc, c, s)


HALO = 16


def _conv_kernel(prev_ref, cur_ref, next_ref, w_ref, bdw_ref, g_ref, b_ref, o_ref, ext_ref, *, nctx):
    i = pl.program_id(1)
    nt = pl.num_programs(1)
    tm = cur_ref.shape[0]

    def glu(z):
        z = z.astype(F32)
        return z[:, :CONV_WIDTH] * jax.nn.sigmoid(z[:, CONV_WIDTH:])

    has_prev = jnp.logical_and(i != 0, i != nctx)
    has_next = jnp.logical_and(i != nctx - 1, i != nt - 1)
    ext_ref[0:HALO, :] = jnp.where(has_prev, glu(prev_ref[tm - HALO:tm, :]), 0.0)
    ext_ref[HALO:HALO + tm, :] = glu(cur_ref[...])
    ext_ref[HALO + tm:2 * HALO + tm, :] = jnp.where(has_next, glu(next_ref[0:HALO, :]), 0.0)
    pad = CONV_KERNEL // 2
    acc = jnp.zeros((tm, CONV_WIDTH), F32)
    for k in range(CONV_KERNEL):
        acc = acc + w_ref[k:k + 1, :] * ext_ref[pl.ds(HALO - pad + k, tm), :]
    y = acc + bdw_ref[...]
    gw = CONV_WIDTH // CONV_GROUPS
    outs = []
    for gi in range(CONV_GROUPS):
        yg = y[:, gi * gw:(gi + 1) * gw]
        mu = jnp.mean(yg, axis=-1, keepdims=True)
        var = jnp.mean(jnp.square(yg - mu), axis=-1, keepdims=True)
        outs.append((yg - mu) * lax.rsqrt(var + 1e-5))
    yn = jnp.concatenate(outs, axis=1) * g_ref[...] + b_ref[...]
    o_ref[...] = (yn * jax.nn.sigmoid(yn)).astype(o_ref.dtype)


def conformer_conv(zc, w_dw, b_dw, g, b, nctx):
    B, T, W2 = zc.shape
    tm = ROW_TILE
    nt = T // tm
    w_pad = jnp.concatenate([w_dw, jnp.zeros((32 - CONV_KERNEL, CONV_WIDTH), F32)], axis=0)
    vec = pl.BlockSpec((1, CONV_WIDTH), lambda bb, i: (0, 0))
    return pl.pallas_call(
        functools.partial(_conv_kernel, nctx=nctx), grid=(B, nt),
        in_specs=[pl.BlockSpec((None, tm, W2), lambda bb, i: (bb, jnp.maximum(i - 1, 0), 0)),
                  pl.BlockSpec((None, tm, W2), lambda bb, i: (bb, i, 0)),
                  pl.BlockSpec((None, tm, W2), lambda bb, i: (bb, jnp.minimum(i + 1, nt - 1), 0)),
                  pl.BlockSpec((32, CONV_WIDTH), lambda bb, i: (0, 0)), vec, vec, vec],
        out_specs=pl.BlockSpec((None, tm, CONV_WIDTH), lambda bb, i: (bb, i, 0)),
        out_shape=jax.ShapeDtypeStruct((B, T, CONV_WIDTH), BF16),
        scratch_shapes=[pltpu.VMEM((tm + 2 * HALO, CONV_WIDTH), F32)],
        compiler_params=_cparams(("parallel", "parallel")), name="conformer_conv",
    )(zc, zc, zc, w_pad, b_dw.reshape(1, -1), g.reshape(1, -1), b.reshape(1, -1))


def _merge_kernel(h_ref, b0_ref, b1_ref, b2_ref, b3_ref, wg_ref, wb_ref, o_ref):
    h = h_ref[...]
    acc = None
    for n, br in enumerate((b0_ref, b1_ref, b2_ref, b3_ref)):
        gate = jax.nn.sigmoid(jnp.dot(h, wg_ref[n], preferred_element_type=F32))
        term = gate * jnp.dot(br[...], wb_ref[n], preferred_element_type=F32)
        acc = term if acc is None else acc + term
    o_ref[...] = acc.astype(o_ref.dtype)


def gated_merge(h, branches, w_gate, w_branch):
    M, D = h.shape
    N = w_gate.shape[2]
    tm = _pick(M, (FLAT_TILE, 256))
    tn = _pick(N, (512, 256))
    br_spec = pl.BlockSpec((tm, BRANCH_W), lambda j, i: (i, 0))
    return pl.pallas_call(
        _merge_kernel, grid=(N // tn, M // tm),
        in_specs=[pl.BlockSpec((tm, D), lambda j, i: (i, 0)), br_spec, br_spec, br_spec, br_spec,
                  pl.BlockSpec((N_BRANCH, D, tn), lambda j, i: (0, 0, j)),
                  pl.BlockSpec((N_BRANCH, BRANCH_W, tn), lambda j, i: (0, 0, j))],
        out_specs=pl.BlockSpec((tm, tn), lambda j, i: (i, j)),
        out_shape=jax.ShapeDtypeStruct((M, N), BF16),
        compiler_params=_cparams(("parallel", "parallel")), name="gated_merge",
    )(h, *branches, w_gate, w_branch)


_NRANK = PEER_TOPK + 1
_STAIRCASE = [(a, b) for a in range(_NRANK) for b in range(_NRANK) if (a + 1) * (b + 1) <= _NRANK]
_NCAND = -(-len(_STAIRCASE) // 8) * 8
_NRANK_PAD = -(-_NRANK // 8) * 8


def _peer_threshold_kernel(q_ref, keys_ref, e1_ref, cut_ref, e2_ref, t1_ref, t2_ref, c_ref):
    def extract(vals, dst_ref):
        v = vals
        m = prev = None
        for r in range(_NRANK):
            prev = m
            m = jnp.max(v, axis=0, keepdims=True)
            if dst_ref is not None:
                dst_ref[r:r + 1, :] = m
            v = jnp.where(v == m, NEG_INF, v)
        return prev, m

    def sub_scores(h, p):
        cols = slice((2 * h + p) * LANES, (2 * h + p + 1) * LANES)
        return lax.dot_general(keys_ref[p], q_ref[:, cols], (((1,), (1,)), ((), ())),
                               preferred_element_type=F32) * LOG2E

    for h in range(PEER_HEADS):
        u1 = sub_scores(h, 0)
        u2 = sub_scores(h, 1)
        extract(u1, t1_ref)
        extract(u2, t2_ref)
        m1 = t1_ref[0:1, :]
        m2 = t2_ref[0:1, :]
        c_ref[...] = jnp.full(c_ref.shape, NEG_INF, F32)
        for r, (a, b) in enumerate(_STAIRCASE):
            c_ref[r:r + 1, :] = (t1_ref[a:a + 1, :] - m1) + (t2_ref[b:b + 1, :] - m2)
        cand = c_ref[...]
        c16, c17 = extract(cand, None)
        thr = 0.5 * (c16 + c17)
        z = jnp.sum(jnp.where(cand >= thr, jnp.exp2(cand), 0.0), axis=0, keepdims=True)
        lz = -jnp.log2(z)
        s1f = (u1 - m1) + lz
        e1_ref[h] = jnp.exp2(s1f)
        cut_ref[h] = jnp.exp2((thr + lz) - s1f)
        e2_ref[h] = jnp.exp2(u2 - m2)


def peer_tables(q, keys):
    M = q.shape[0]
    tl = _pick(M, (256, 128))
    sf_spec = pl.BlockSpec((PEER_HEADS, PEER_NKEYS, tl), lambda i: (0, 0, i))
    sf_shape = jax.ShapeDtypeStruct((PEER_HEADS, PEER_NKEYS, M), F32)
    return pl.pallas_call(
        _peer_threshold_kernel, grid=(M // tl,),
        in_specs=[pl.BlockSpec((tl, q.shape[1]), lambda i: (i, 0)), _full(keys)],
        out_specs=[sf_spec, sf_spec, sf_spec], out_shape=[sf_shape, sf_shape, sf_shape],
        scratch_shapes=[pltpu.VMEM((_NRANK_PAD, tl), F32), pltpu.VMEM((_NRANK_PAD, tl), F32),
                        pltpu.VMEM((_NCAND, tl), F32)],
        compiler_params=_cparams(("parallel",)), name="peer_tables",
    )(q, keys)


PEER_TOKEN_SPLIT = 256
PEER_EXPERT_SPLIT = 512
PEER_EXPERT_CHUNK = 1024
PEER_TOKEN_TILE = 768


def _peer_experts_kernel(xt_ref, u_ref, vt_ref, e1_ref, cut_ref, e2_ref, o_ref, acc_ref, *scratch, ek):
    j = pl.program_id(1)
    tm = xt_ref.shape[1]
    th, es = PEER_TOKEN_SPLIT, PEER_EXPERT_SPLIT
    pieces = [(c, t) for c in range(ek // es) for t in range(tm // th)]
    st_refs, a_refs = scratch[:len(pieces)], scratch[len(pieces):]

    @pl.when(j == 0)
    def _():
        acc_ref[...] = jnp.zeros(acc_ref.shape, F32)

    def pre_activations(p):
        c, t = pieces[p]
        st_refs[p][...] = jnp.dot(u_ref[c * es:(c + 1) * es, :], xt_ref[:, t * th:(t + 1) * th],
                                  preferred_element_type=F32)

    def weights(p):
        c, t = pieces[p]
        half = slice(t * th, (t + 1) * th)
        pair = 2
        for r0 in range(0, es // PEER_NKEYS, pair):
            i1s = [j * (ek // PEER_NKEYS) + c * (es // PEER_NKEYS) + r0 + k for k in range(pair)]
            cut_rows = [[cut_ref[h, pl.ds(i1, 1), half] for h in range(PEER_HEADS)] for i1 in i1s]
            e1_rows = [[e1_ref[h, pl.ds(i1, 1), half] for h in range(PEER_HEADS)] for i1 in i1s]
            for g in range(th // LANES):
                lanes = slice(t * th + g * LANES, t * th + (g + 1) * LANES)
                sub = slice(g * LANES, (g + 1) * LANES)
                ws = [None] * pair
                for h in range(PEER_HEADS):
                    e2 = e2_ref[h, :, lanes]
                    for k in range(pair):
                        sel = jnp.where(e2 >= cut_rows[k][h][:, sub], e2, 0.0) * e1_rows[k][h][:, sub]
                        ws[k] = sel if ws[k] is None else ws[k] + sel
                for k in range(pair):
                    rows = slice((r0 + k) * PEER_NKEYS, (r0 + k + 1) * PEER_NKEYS)
                    pre = st_refs[p][rows, g * LANES:(g + 1) * LANES]
                    act = pre * (1.0 + lax.erf(pre * (2.0 ** -0.5)))
                    a_refs[p][rows, g * LANES:(g + 1) * LANES] = (ws[k] * act).astype(BF16)

    def accumulate(p):
        c, t = pieces[p]
        lanes = slice(t * th, (t + 1) * th)
        acc_ref[:, lanes] += jnp.dot(vt_ref[:, c * es:(c + 1) * es], a_refs[p][...], preferred_element_type=F32)

    for p in range(len(pieces)):
        pre_activations(p)
    for p in range(len(pieces)):
        weights(p)
        accumulate(p)

    @pl.when(j == pl.num_programs(1) - 1)
    def _():
        o_ref[...] = acc_ref[...].T


def peer_experts(xt, u, vt, e1, cut, e2):
    D, M = xt.shape
    E = u.shape[0]
    tm = _pick(M, (PEER_TOKEN_TILE, FLAT_TILE, 256))
    ek = vt.shape[2]
    th, es = PEER_TOKEN_SPLIT, PEER_EXPERT_SPLIT
    npieces = (tm // th) * (ek // es)
    table_spec = pl.BlockSpec((PEER_HEADS, PEER_NKEYS, tm), lambda i, j: (0, 0, i), pipeline_mode=pl.Buffered(1))
    return pl.pallas_call(
        functools.partial(_peer_experts_kernel, ek=ek), grid=(M // tm, E // ek),
        in_specs=[pl.BlockSpec((D, tm), lambda i, j: (0, i)),
                  pl.BlockSpec((ek, D), lambda i, j: (j, 0)),
                  pl.BlockSpec((None, D, ek), lambda i, j: (j, 0, 0)),
                  table_spec, table_spec, table_spec],
        out_specs=pl.BlockSpec((tm, D), lambda i, j: (i, 0)),
        out_shape=jax.ShapeDtypeStruct((M, D), F32),
        scratch_shapes=([pltpu.VMEM((D, tm), F32)]
                        + [pltpu.VMEM((es, th), F32)] * npieces + [pltpu.VMEM((es, th), BF16)] * npieces),
        compiler_params=_cparams(("parallel", "arbitrary")), name="peer_experts",
    )(xt, u, vt, e1, cut, e2)


def _final_kernel(x_ref, d_ref, mg_ref, g_ref, o_ref, *, gate_row):
    x = x_ref[...] + mg_ref[gate_row:gate_row + 1, :] * d_ref[...]
    o_ref[...] = x * lax.rsqrt(jnp.mean(x * x, axis=-1, keepdims=True) + EPS) * g_ref[...]


def final_norm(x, delta, mod_gate, g, nctx, gate_row):
    B, T, D = x.shape
    tm = ROW_TILE
    L = T - nctx * tm
    row_spec = pl.BlockSpec((None, tm, D), lambda b, i: (b, i + nctx, 0))
    return pl.pallas_call(
        functools.partial(_final_kernel, gate_row=gate_row), grid=(B, L // tm),
        in_specs=[row_spec, row_spec, pl.BlockSpec((None, 8, D), lambda b, i: (b, 0, 0)),
                  pl.BlockSpec((1, D), lambda b, i: (0, 0))],
        out_specs=pl.BlockSpec((None, tm, D), lambda b, i: (b, i, 0)),
        out_shape=jax.ShapeDtypeStruct((B, L, D), F32),
        compiler_params=_cparams(("parallel", "parallel")), name="final_norm",
    )(x, delta, mod_gate, g.reshape(1, D))


def _rope_tables(L, ctx_len):
    rows = L // GRID_W
    row = jnp.repeat(jnp.arange(rows, dtype=F32), GRID_W)
    col = jnp.tile(jnp.arange(GRID_W, dtype=F32), rows)
    nf = MLA_ROPE // 4
    inv = ROPE_BASE ** (-jnp.arange(nf, dtype=F32) / nf)
    ang = jnp.concatenate([row[:, None] * inv, col[:, None] * inv], axis=-1)
    cos, sin = jnp.cos(ang), jnp.sin(ang)
    cos_t = jnp.concatenate([cos, cos, cos, cos], axis=-1)
    sin_t = jnp.concatenate([-sin, sin, -sin, sin], axis=-1)
    cos_t = jnp.concatenate([jnp.ones((ctx_len, LANES), F32), cos_t], axis=0)
    sin_t = jnp.concatenate([jnp.zeros((ctx_len, LANES), F32), sin_t], axis=0)
    return cos_t, sin_t


def kernel(x, c, ctx, c_ctx, w_ada, b_ada, g_norm1, g_norm2, w_in, g_q_mla, w_uq, g_kv_mla, w_ukv, lam_q1, lam_k1, lam_q2, lam_k2, g_subln, w_dw, b_dw, g_conv_norm, b_conv_norm, w_gate, w_branch, w_out, w_peer_q, peer_keys, peer_u, peer_v, g_final):
    B, L, D = x.shape
    ctx_len = ctx.shape[1]
    depth = w_ada.shape[0]
    T = ctx_len + L
    M = B * T
    assert MLA_ROPE == DIFF_QK and ctx_len % ROW_TILE == 0 and L % ROW_TILE == 0
    nctx = ctx_len // ROW_TILE

    cos_t, sin_t = _rope_tables(L, ctx_len)
    xs = jnp.concatenate([ctx, x], axis=1)

    mods = ada_modulation(jnp.concatenate([c, c_ctx[None, :]], axis=0), w_ada, b_ada)
    mods = mods.reshape(depth, 8, 6, D)[:, :B + 1]
    mods = jnp.concatenate([mods, jnp.zeros((depth, B + 1, 2, D), F32)], axis=2)

    pending = None
    for l in range(depth):
        lam_init = 0.8 - 0.6 * math.exp(-0.3 * l)
        s2 = IN_MLA
        wi = w_in[l]
        w_mla = jnp.concatenate([wi[:, :s2], jnp.zeros((D, LANES - MLA_ROPE), F32)], axis=1).astype(BF16)
        w_diff = wi[:, s2:s2 + IN_DIFF].astype(BF16)
        w_fnet = wi[:, s2 + IN_DIFF:s2 + IN_DIFF + FNET_WIDTH].astype(BF16)
        w_conv = wi[:, s2 + IN_DIFF + FNET_WIDTH:].astype(BF16)
        hd = MLA_NOPE + MLA_ROPE
        w_uq_p = jnp.pad(w_uq[l].reshape(MLA_Q_RANK, MLA_HEADS, hd),
                         ((0, 0), (0, 0), (0, 256 - hd))).reshape(MLA_Q_RANK, MLA_HEADS * 256).astype(BF16)
        wkv = w_ukv[l].reshape(MLA_KV_RANK, MLA_HEADS, MLA_NOPE + MLA_V)
        w_ukv_p = jnp.concatenate([wkv[:, :, :MLA_NOPE].reshape(MLA_KV_RANK, -1),
                                   wkv[:, :, MLA_NOPE:].reshape(MLA_KV_RANK, -1)], axis=1).astype(BF16)
        lam_vecs = jnp.stack([lam_q1[l], lam_k1[l], lam_q2[l], lam_k2[l]], axis=0)

        if pending is None:
            h = norm_modulate(xs, g_norm1[l], mods[l], 0, 1, nctx)
        else:
            xs, h = norm_modulate(xs, g_norm1[l], mods[l], 0, 1, nctx,
                                  delta=pending[0], mod_gate=pending[1], gate_row=5)
        hf = h.reshape(M, D)
        z_mla = matmul(hf, w_mla, BF16, "in_proj_mla").reshape(B, T, -1)
        z_diff = matmul(hf, w_diff, BF16, "in_proj_diff").reshape(B, T, -1)
        z_fnet = matmul(hf, w_fnet, BF16, "in_proj_fnet").reshape(B, T, -1)
        z_conv = matmul(hf, w_conv, BF16, "in_proj_conv").reshape(B, T, -1)

        qm, km, vm, qd, kd, vd = attention_projections(z_mla, z_diff, cos_t, sin_t, g_q_mla[l], g_kv_mla[l],
                                                   w_uq_p, w_ukv_p)
        o_mla = mla_attention(qm, km, vm, ctx_len)
        o_diff = diff_attention(qd, kd, vd, lam_vecs, g_subln[l], lam_init, ctx_len)
        o_fnet = jnp.concatenate([fourier_mix_short(z_fnet[:, :ctx_len]), fourier_mix_long(z_fnet[:, ctx_len:])],
                                 axis=1)
        o_conv = conformer_conv(z_conv, w_dw[l], b_dw[l], g_conv_norm[l], b_conv_norm[l], nctx)

        y = gated_merge(hf, [o.reshape(M, BRANCH_W) for o in (o_mla, o_fnet, o_conv, o_diff)],
                        w_gate[l].astype(BF16), w_branch[l].astype(BF16))
        mix = matmul(y, w_out[l].astype(BF16), F32, "out_proj").reshape(B, T, D)
        xs, h2 = norm_modulate(xs, g_norm2[l], mods[l], 3, 4, nctx, delta=mix, mod_gate=mods[l], gate_row=2)

        h2f = h2.reshape(M, D)
        q = matmul(h2f, w_peer_q[l].astype(BF16), BF16, "peer_query")
        e1, cut, e2 = peer_tables(q, peer_keys[l].astype(BF16))
        peer_vt = jnp.swapaxes((0.5 * peer_v[l]).reshape(-1, PEER_EXPERT_CHUNK, D), 1, 2).astype(BF16)
        peer_out = peer_experts(h2f.T, peer_u[l].astype(BF16), peer_vt, e1, cut, e2)
        pending = (peer_out.reshape(B, T, D), mods[l])

    return final_norm(xs, pending[0], pending[1], g_final, nctx, 5)
```

```python
import functools
import math

import numpy as np
import jax
import jax.numpy as jnp
from jax import lax
from jax.experimental import pallas as pl
from jax.experimental.pallas import tpu as pltpu

F32 = jnp.float32
BF16 = jnp.bfloat16

GRID_W = 64
ROPE_BASE = 10000.0
EPS = 1e-6
MLA_HEADS = 4
MLA_Q_RANK = 384
MLA_KV_RANK = 256
MLA_NOPE = 128
MLA_ROPE = 64
MLA_V = 128
DIFF_HEADS = 4
DIFF_QK = 64
DIFF_V = 128
FNET_GROUPS = 4
FNET_GROUP_DIM = 128
FNET_WIDTH = 512
CONV_WIDTH = 512
CONV_KERNEL = 31
CONV_GROUPS = 4
N_BRANCH = 4
BRANCH_W = 512
IN_MLA = MLA_Q_RANK + MLA_KV_RANK + MLA_ROPE
IN_DIFF = DIFF_HEADS * (4 * DIFF_QK + DIFF_V)
PEER_HEADS = 8
PEER_NKEYS = 128
PEER_EXPERTS = PEER_NKEYS * PEER_NKEYS
PEER_DKEY = 256
PEER_TOPK = 16

LANES = 128
ROW_TILE = 256
FLAT_TILE = 512
VMEM_LIMIT = 56 * 1024 * 1024
NEG_INF = float("-inf")
LOG2E = math.log2(math.e)


def _cparams(sem):
    return pltpu.CompilerParams(dimension_semantics=sem, vmem_limit_bytes=VMEM_LIMIT)


def _pick(n, cands):
    for c in cands:
        if n % c == 0:
            return c
    raise ValueError(f"no tile for {n} in {cands}")


def _ada_kernel(cb_ref, w_ref, b_ref, o_ref, *, rows, tn):
    outs = []
    for r in range(rows):
        a = cb_ref[r]
        a = a * jax.nn.sigmoid(a)
        cols = [jnp.sum(w_ref[:, j * LANES:(j + 1) * LANES] * a, axis=0, keepdims=True)
                for j in range(tn // LANES)]
        outs.append(jnp.concatenate(cols, axis=1) + b_ref[...])
    outs.append(jnp.zeros((8 - rows, tn), F32))
    o_ref[...] = jnp.concatenate(outs, axis=0)


def ada_modulation(cond, w_ada, b_ada):
    rows, d = cond.shape
    depth, _, n = w_ada.shape
    tn = _pick(n, (1024, 512))
    cb = jnp.broadcast_to(cond[:, :, None], (rows, d, LANES))
    return pl.pallas_call(
        functools.partial(_ada_kernel, rows=rows, tn=tn),
        grid=(depth, n // tn),
        in_specs=[pl.BlockSpec((rows, d, LANES), lambda l, j: (0, 0, 0)),
                  pl.BlockSpec((None, d, tn), lambda l, j: (l, 0, j)),
                  pl.BlockSpec((None, 1, tn), lambda l, j: (l, 0, j))],
        out_specs=pl.BlockSpec((None, 8, tn), lambda l, j: (l, 0, j)),
        out_shape=jax.ShapeDtypeStruct((depth, 8, n), F32),
        compiler_params=_cparams(("arbitrary", "arbitrary")),
        name="ada_modulation",
    )(cb, w_ada, b_ada.reshape(depth, 1, n))


def _norm_mod_kernel(*refs, has_delta, gate_row, shift_row, scale_row):
    if has_delta:
        x_ref, d_ref, mg_ref, ms_ref, g_ref, xo_ref, h_ref = refs
        x = x_ref[...] + mg_ref[gate_row:gate_row + 1, :] * d_ref[...].astype(F32)
        xo_ref[...] = x
    else:
        x_ref, ms_ref, g_ref, h_ref = refs
        x = x_ref[...]
    y = x * lax.rsqrt(jnp.mean(x * x, axis=-1, keepdims=True) + EPS) * g_ref[...]
    h = y * (1.0 + ms_ref[scale_row:scale_row + 1, :]) + ms_ref[shift_row:shift_row + 1, :]
    h_ref[...] = h.astype(BF16)


def norm_modulate(x, g, mod_ss, shift_row, scale_row, nctx, delta=None, mod_gate=None, gate_row=None):
    B, T, D = x.shape
    tm = ROW_TILE
    row_spec = pl.BlockSpec((None, tm, D), lambda b, i: (b, i, 0))
    mod_spec = pl.BlockSpec((None, 8, D), lambda b, i: (jnp.where(i < nctx, B, b), 0, 0))
    g_spec = pl.BlockSpec((1, D), lambda b, i: (0, 0))
    h_shape = jax.ShapeDtypeStruct((B, T, D), BF16)
    kern = functools.partial(_norm_mod_kernel, has_delta=delta is not None, gate_row=gate_row,
                             shift_row=shift_row, scale_row=scale_row)
    if delta is None:
        return pl.pallas_call(
            kern, grid=(B, T // tm), in_specs=[row_spec, mod_spec, g_spec], out_specs=row_spec,
            out_shape=h_shape, compiler_params=_cparams(("parallel", "parallel")), name="norm_modulate",
        )(x, mod_ss, g.reshape(1, D))
    return pl.pallas_call(
        kern, grid=(B, T // tm), in_specs=[row_spec, row_spec, mod_spec, mod_spec, g_spec],
        out_specs=[row_spec, row_spec],
        out_shape=[jax.ShapeDtypeStruct((B, T, D), F32), h_shape],
        compiler_params=_cparams(("parallel", "parallel")), name="residual_norm_modulate",
    )(x, delta, mod_gate, mod_ss, g.reshape(1, D))


def _mm_kernel(x_ref, w_ref, o_ref):
    o_ref[...] = jnp.dot(x_ref[...], w_ref[...], preferred_element_type=F32).astype(o_ref.dtype)


def matmul(x, w, out_dtype, name):
    M, K = x.shape
    N = w.shape[1]
    tm = _pick(M, (FLAT_TILE, 256, 128))
    tn = _pick(N, (2048, 1536, 1024, 768, 512, 256, 128))
    return pl.pallas_call(
        _mm_kernel, grid=(N // tn, M // tm),
        in_specs=[pl.BlockSpec((tm, K), lambda j, i: (i, 0)),
                  pl.BlockSpec((K, tn), lambda j, i: (0, j))],
        out_specs=pl.BlockSpec((tm, tn), lambda j, i: (i, j)),
        out_shape=jax.ShapeDtypeStruct((M, N), out_dtype),
        compiler_params=_cparams(("parallel", "parallel")), name=name,
    )(x, w)


def _rope(v, cos, sin, lane):
    r = jnp.where((lane % 64) < 32, pltpu.roll(v, 96, 1), pltpu.roll(v, 32, 1))
    return v * cos + r * sin


def _proj_kernel(zm_ref, zd_ref, cos_ref, sin_ref, gq_ref, gkv_ref, wuq_ref, wukv_ref,
                 qm_ref, km_ref, vm_ref, qd_ref, kd_ref, vd_ref, *, mla_scale, diff_scale):
    tm = zm_ref.shape[0]
    cos = cos_ref[...]
    sin = sin_ref[...]
    lane = lax.broadcasted_iota(jnp.int32, (tm, LANES), 1)
    ones_col = jnp.where(lane == 0, 1.0, 0.0).astype(BF16)

    def rms(v, g):
        return v * lax.rsqrt(jnp.mean(v * v, axis=-1, keepdims=True) + EPS) * g

    zm = zm_ref[...].astype(F32)
    qn = rms(zm[:, :MLA_Q_RANK], gq_ref[...]).astype(BF16)
    kvn = rms(zm[:, MLA_Q_RANK:MLA_Q_RANK + MLA_KV_RANK], gkv_ref[...]).astype(BF16)
    k_rope = _rope(zm[:, MLA_Q_RANK + MLA_KV_RANK:], cos, sin, lane).astype(BF16)
    q = jnp.dot(qn, wuq_ref[...], preferred_element_type=F32)
    kv = jnp.dot(kvn, wukv_ref[...], preferred_element_type=F32)
    for h in range(MLA_HEADS):
        qm_ref[h, :, 0:LANES] = (q[:, 256 * h:256 * h + LANES] * mla_scale).astype(BF16)
        qm_ref[h, :, LANES:2 * LANES] = (
            _rope(q[:, 256 * h + LANES:256 * h + 2 * LANES], cos, sin, lane) * mla_scale).astype(BF16)
        km_ref[h, :, 0:LANES] = kv[:, LANES * h:LANES * (h + 1)].astype(BF16)
        km_ref[h, :, LANES:2 * LANES] = k_rope
        vm_ref[h, :, 0:LANES] = kv[:, 512 + LANES * h:512 + LANES * (h + 1)].astype(BF16)
        vm_ref[h, :, LANES:2 * LANES] = ones_col
    zd = zd_ref[...].astype(F32)
    for h in range(DIFF_HEADS):
        qd_ref[h] = (_rope(zd[:, LANES * h:LANES * (h + 1)], cos, sin, lane) * diff_scale).astype(BF16)
        kd_ref[h] = _rope(zd[:, 512 + LANES * h:512 + LANES * (h + 1)], cos, sin, lane).astype(BF16)
        vd_ref[h, :, 0:LANES] = zd_ref[:, 1024 + LANES * h:1024 + LANES * (h + 1)]
        vd_ref[h, :, LANES:2 * LANES] = ones_col


def attention_projections(z_mla, z_diff, cos_t, sin_t, g_q, g_kv, w_uq_p, w_ukv_p):
    B, T, _ = z_mla.shape
    tm = ROW_TILE
    H = MLA_HEADS

    def head_spec(w):
        return pl.BlockSpec((None, H, tm, w), lambda b, i: (b, 0, i, 0))

    def head_shape(w):
        return jax.ShapeDtypeStruct((B, H, T, w), BF16)

    return pl.pallas_call(
        functools.partial(_proj_kernel, mla_scale=float((MLA_NOPE + MLA_ROPE) ** -0.5 * LOG2E),
                          diff_scale=float(DIFF_QK ** -0.5 * LOG2E)),
        grid=(B, T // tm),
        in_specs=[pl.BlockSpec((None, tm, z_mla.shape[2]), lambda b, i: (b, i, 0)),
                  pl.BlockSpec((None, tm, z_diff.shape[2]), lambda b, i: (b, i, 0)),
                  pl.BlockSpec((tm, LANES), lambda b, i: (i, 0)),
                  pl.BlockSpec((tm, LANES), lambda b, i: (i, 0)),
                  pl.BlockSpec((1, MLA_Q_RANK), lambda b, i: (0, 0)),
                  pl.BlockSpec((1, MLA_KV_RANK), lambda b, i: (0, 0)),
                  pl.BlockSpec(w_uq_p.shape, lambda b, i: (0, 0)),
                  pl.BlockSpec(w_ukv_p.shape, lambda b, i: (0, 0))],
        out_specs=[head_spec(256), head_spec(256), head_spec(256), head_spec(128), head_spec(128), head_spec(256)],
        out_shape=[head_shape(256), head_shape(256), head_shape(256), head_shape(128), head_shape(128),
                   head_shape(256)],
        compiler_params=_cparams(("parallel", "parallel")), name="attention_projections",
    )(z_mla, z_diff, cos_t, sin_t, g_q.reshape(1, -1), g_kv.reshape(1, -1), w_uq_p, w_ukv_p)


V_EXT = 2 * LANES


def _att_scores(q, k, mask):
    s = lax.dot_general(q, k, (((1,), (1,)), ((), ())), preferred_element_type=F32)
    return s if mask is None else jnp.where(mask, NEG_INF, s)


def _lane_tile(x, width):
    return x if width == LANES else jnp.concatenate([x] * (width // LANES), axis=1)


def _att_update(s, v_ext, m_ref, acc_ref, idx):
    m_prev = m_ref[idx]
    m_new = jnp.maximum(m_prev, jnp.max(s, axis=-1, keepdims=True))
    p = jnp.exp2(s - _lane_tile(m_new, s.shape[1])).astype(BF16)
    alpha = _lane_tile(jnp.exp2(m_prev - m_new), V_EXT)
    acc_ref[idx] = alpha * acc_ref[idx] + jnp.dot(p, v_ext, preferred_element_type=F32)
    m_ref[idx] = m_new


def _att_init(m_ref, acc_ref):
    m_ref[...] = jnp.full(m_ref.shape, NEG_INF, F32)
    acc_ref[...] = jnp.zeros(acc_ref.shape, F32)


def _att_step(streams, mask, m_ref, acc_ref):
    for idx, (q, k, v) in enumerate(streams):
        _att_update(_att_scores(q(), k(), mask), v(), m_ref, acc_ref, idx)


def _att_sweep(step, qi, ki, tq, tk, ctx_len):
    @pl.when(qi == 0)
    def _():
        row = lax.broadcasted_iota(jnp.int32, (tq, tk), 0)
        col = ki * tk + lax.broadcasted_iota(jnp.int32, (tq, tk), 1)
        step((row < ctx_len) & (col >= ctx_len))

    @pl.when(qi != 0)
    def _():
        step(None)


def _att_result(acc):
    return acc[:, :LANES] / acc[:, LANES:LANES + 1]


def _mla_att_kernel(q_ref, k_ref, v_ref, o_ref, m_ref, acc_ref, *, tq, tk, ctx_len):
    qi = pl.program_id(1)
    ki = pl.program_id(2)

    @pl.when(ki == 0)
    def _():
        _att_init(m_ref, acc_ref)

    streams = [(lambda h=h: q_ref[h], lambda h=h: k_ref[h], lambda h=h: v_ref[h]) for h in range(MLA_HEADS)]
    _att_sweep(lambda mask: _att_step(streams, mask, m_ref, acc_ref), qi, ki, tq, tk, ctx_len)

    @pl.when(ki == pl.num_programs(2) - 1)
    def _():
        for h in range(MLA_HEADS):
            o_ref[:, h * MLA_V:(h + 1) * MLA_V] = _att_result(acc_ref[h]).astype(o_ref.dtype)


def _att_specs(H, tq, tk, dk):
    return [pl.BlockSpec((None, H, tq, dk), lambda b, i, j: (b, 0, i, 0)),
            pl.BlockSpec((None, H, tk, dk), lambda b, i, j: (b, 0, j, 0)),
            pl.BlockSpec((None, H, tk, V_EXT), lambda b, i, j: (b, 0, j, 0))]


def _att_scratch(nstreams, tq):
    return [pltpu.VMEM((nstreams, tq, LANES), F32), pltpu.VMEM((nstreams, tq, V_EXT), F32)]


def mla_attention(qm, km, vm, ctx_len):
    B, H, T, dk = qm.shape
    tq = tk = _pick(T, (768, 256))
    assert ctx_len <= tq
    return pl.pallas_call(
        functools.partial(_mla_att_kernel, tq=tq, tk=tk, ctx_len=ctx_len),
        grid=(B, T // tq, T // tk),
        in_specs=_att_specs(H, tq, tk, dk),
        out_specs=pl.BlockSpec((None, tq, H * MLA_V), lambda b, i, j: (b, i, 0)),
        out_shape=jax.ShapeDtypeStruct((B, T, H * MLA_V), BF16),
        scratch_shapes=_att_scratch(H, tq),
        compiler_params=_cparams(("parallel", "parallel", "arbitrary")), name="mla_attention",
    )(qm, km, vm)


def _diff_att_kernel(q_ref, k_ref, v_ref, lam_ref, g_ref, o_ref, m_ref, acc_ref, *, tq, tk, ctx_len, lam_init):
    qi = pl.program_id(1)
    ki = pl.program_id(2)

    @pl.when(ki == 0)
    def _():
        _att_init(m_ref, acc_ref)

    def component(h, c):
        q = q_ref[h]
        first = lax.broadcasted_iota(jnp.int32, q.shape, 1) < DIFF_QK
        return jnp.where(first if c == 0 else jnp.logical_not(first), q, jnp.zeros_like(q))

    streams = [(lambda h=h, c=c: component(h, c), lambda h=h: k_ref[h], lambda h=h: v_ref[h])
               for h in range(DIFF_HEADS) for c in range(2)]
    _att_sweep(lambda mask: _att_step(streams, mask, m_ref, acc_ref), qi, ki, tq, tk, ctx_len)

    @pl.when(ki == pl.num_programs(2) - 1)
    def _():
        lv = lam_ref[...]
        lam = (jnp.exp(jnp.sum(lv[0:1] * lv[1:2], axis=-1, keepdims=True))
               - jnp.exp(jnp.sum(lv[2:3] * lv[3:4], axis=-1, keepdims=True)) + lam_init)
        for h in range(DIFF_HEADS):
            o = _att_result(acc_ref[2 * h]) - lam * _att_result(acc_ref[2 * h + 1])
            o = o * lax.rsqrt(jnp.mean(o * o, axis=-1, keepdims=True) + EPS) * g_ref[...]
            o_ref[:, h * DIFF_V:(h + 1) * DIFF_V] = (o * (1.0 - lam_init)).astype(o_ref.dtype)


def diff_attention(qd, kd, vd, lam_vecs, g_subln, lam_init, ctx_len):
    B, H, T, dk = qd.shape
    tq = tk = _pick(T, (768, 256))
    assert ctx_len <= tq
    return pl.pallas_call(
        functools.partial(_diff_att_kernel, tq=tq, tk=tk, ctx_len=ctx_len, lam_init=lam_init),
        grid=(B, T // tq, T // tk),
        in_specs=_att_specs(H, tq, tk, dk) + [pl.BlockSpec(lam_vecs.shape, lambda b, i, j: (0, 0)),
                                               pl.BlockSpec((1, DIFF_V), lambda b, i, j: (0, 0))],
        out_specs=pl.BlockSpec((None, tq, H * DIFF_V), lambda b, i, j: (b, i, 0)),
        out_shape=jax.ShapeDtypeStruct((B, T, H * DIFF_V), BF16),
        scratch_shapes=_att_scratch(2 * H, tq),
        compiler_params=_cparams(("parallel", "parallel", "arbitrary")), name="diff_attention",
    )(qd, kd, vd, lam_vecs, g_subln.reshape(1, -1))


def _dft_mats(n):
    k = np.arange(n)
    ang = 2.0 * np.pi * ((k[:, None] * k[None, :]) % n) / n
    return jnp.asarray(np.cos(ang), BF16), jnp.asarray(np.sin(ang), BF16)


def _channel_dft(x, cc, sc):
    tr, ti = [], []
    for s in range(x.shape[1] // LANES):
        xs = x[:, s * LANES:(s + 1) * LANES]
        tr.append(jnp.dot(xs, cc, preferred_element_type=F32))
        ti.append(-jnp.dot(xs, sc, preferred_element_type=F32))
    return jnp.concatenate(tr, axis=1), jnp.concatenate(ti, axis=1)


def _fnet_stage1_kernel(x_ref, cc_ref, sc_ref, c_ref, s_ref, twc_ref, tws_ref, yr_ref, yi_ref, *, nb):
    tr, ti = _channel_dft(x_ref[...], cc_ref[...], sc_ref[...])
    trb, tib = tr.astype(BF16), ti.astype(BF16)
    c, s = c_ref[...], s_ref[...]
    yr = jnp.dot(c, trb, preferred_element_type=F32) + jnp.dot(s, tib, preferred_element_type=F32)
    yi = jnp.dot(c, tib, preferred_element_type=F32) - jnp.dot(s, trb, preferred_element_type=F32)
    for i in range(nb):
        twc = jnp.concatenate([twc_ref[i]] * FNET_GROUPS, axis=1)
        tws = jnp.concatenate([tws_ref[i]] * FNET_GROUPS, axis=1)
        a = yr[:, i * FNET_WIDTH:(i + 1) * FNET_WIDTH]
        b = yi[:, i * FNET_WIDTH:(i + 1) * FNET_WIDTH]
        yr_ref[i] = a * twc + b * tws
        yi_ref[i] = b * twc - a * tws


def _fnet_stage2_kernel(yr_ref, yi_ref, c_ref, s_ref, o_ref, *, scale):
    xr = (jnp.dot(c_ref[...], yr_ref[...].astype(BF16), preferred_element_type=F32)
          + jnp.dot(s_ref[...], yi_ref[...].astype(BF16), preferred_element_type=F32))
    o_ref[...] = (xr * scale).astype(o_ref.dtype)


def _fnet_direct_kernel(x_ref, cc_ref, sc_ref, c_ref, s_ref, o_ref, *, scale):
    tr, ti = _channel_dft(x_ref[...], cc_ref[...], sc_ref[...])
    xr = (jnp.dot(c_ref[...], tr.astype(BF16), preferred_element_type=F32)
          + jnp.dot(s_ref[...], ti.astype(BF16), preferred_element_type=F32))
    o_ref[...] = (xr * scale).astype(o_ref.dtype)


def _full(a):
    return pl.BlockSpec(a.shape, lambda *_: (0,) * a.ndim)


def fourier_mix_long(zf):
    B, L, W = zf.shape
    n2 = LANES
    n1 = L // n2
    nb = _pick(n1, (8,))
    cc, sc = _dft_mats(FNET_GROUP_DIM)
    c2, s2 = _dft_mats(n2)
    c1, s1 = _dft_mats(n1)
    ang = 2.0 * np.pi * (np.arange(n1)[:, None] * np.arange(n2)[None, :]) / L
    twc = jnp.asarray(np.broadcast_to(np.cos(ang)[:, :, None], (n1, n2, LANES)), F32)
    tws = jnp.asarray(np.broadcast_to(np.sin(ang)[:, :, None], (n1, n2, LANES)), F32)
    x = zf.reshape(B, n2, n1 * W)
    y_shape = jax.ShapeDtypeStruct((B, n1, n2, W), F32)
    y_spec = pl.BlockSpec((None, nb, n2, W), lambda b, j: (b, j, 0, 0))
    tw_spec = pl.BlockSpec((nb, n2, LANES), lambda b, j: (j, 0, 0))
    yr, yi = pl.pallas_call(
        functools.partial(_fnet_stage1_kernel, nb=nb), grid=(B, n1 // nb),
        in_specs=[pl.BlockSpec((None, n2, nb * W), lambda b, j: (b, 0, j)),
                  _full(cc), _full(sc), _full(c2), _full(s2), tw_spec, tw_spec],
        out_specs=[y_spec, y_spec], out_shape=[y_shape, y_shape],
        compiler_params=_cparams(("parallel", "parallel")), name="fnet_stage1",
    )(x, cc, sc, c2, s2, twc, tws)
    cols = n2 * W
    tn = 4096
    y2_spec = pl.BlockSpec((None, n1, tn), lambda b, j: (b, 0, j))
    out = pl.pallas_call(
        functools.partial(_fnet_stage2_kernel, scale=float((L * FNET_GROUP_DIM) ** -0.5)),
        grid=(B, cols // tn),
        in_specs=[y2_spec, y2_spec, _full(c1), _full(s1)],
        out_specs=y2_spec, out_shape=jax.ShapeDtypeStruct((B, n1, cols), BF16),
        compiler_params=_cparams(("parallel", "parallel")), name="fnet_stage2",
    )(yr.reshape(B, n1, cols), yi.reshape(B, n1, cols), c1, s1)
    return out.reshape(B, L, W)


def fourier_mix_short(zf):
    B, L, W = zf.shape
    cc, sc = _dft_mats(FNET_GROUP_DIM)
    c, s = _dft_mats(L)
    spec = pl.BlockSpec((None, L, W), lambda b: (b, 0, 0))
    return pl.pallas_call(
        functools.partial(_fnet_direct_kernel, scale=float((L * FNET_GROUP_DIM) ** -0.5)), grid=(B,),
        in_specs=[spec, _full(cc), _full(sc), _full(c), _full(s)],
        out_specs=spec, out_shape=jax.ShapeDtypeStruct((B, L, W), BF16),
        compiler_params=_cparams(("parallel",)), name="fnet_direct",
    )(zf, cc, sc, c, s)


HALO = 16


def _conv_kernel(prev_ref, cur_ref, next_ref, w_ref, bdw_ref, g_ref, b_ref, o_ref, ext_ref, *, nctx):
    i = pl.program_id(1)
    nt = pl.num_programs(1)
    tm = cur_ref.shape[0]

    def glu(z):
        z = z.astype(F32)
        return z[:, :CONV_WIDTH] * jax.nn.sigmoid(z[:, CONV_WIDTH:])

    has_prev = jnp.logical_and(i != 0, i != nctx)
    has_next = jnp.logical_and(i != nctx - 1, i != nt - 1)
    ext_ref[0:HALO, :] = jnp.where(has_prev, glu(prev_ref[tm - HALO:tm, :]), 0.0)
    ext_ref[HALO:HALO + tm, :] = glu(cur_ref[...])
    ext_ref[HALO + tm:2 * HALO + tm, :] = jnp.where(has_next, glu(next_ref[0:HALO, :]), 0.0)
    pad = CONV_KERNEL // 2
    acc = jnp.zeros((tm, CONV_WIDTH), F32)
    for k in range(CONV_KERNEL):
        acc = acc + w_ref[k:k + 1, :] * ext_ref[pl.ds(HALO - pad + k, tm), :]
    y = acc + bdw_ref[...]
    gw = CONV_WIDTH // CONV_GROUPS
    outs = []
    for gi in range(CONV_GROUPS):
        yg = y[:, gi * gw:(gi + 1) * gw]
        mu = jnp.mean(yg, axis=-1, keepdims=True)
        var = jnp.mean(jnp.square(yg - mu), axis=-1, keepdims=True)
        outs.append((yg - mu) * lax.rsqrt(var + 1e-5))
    yn = jnp.concatenate(outs, axis=1) * g_ref[...] + b_ref[...]
    o_ref[...] = (yn * jax.nn.sigmoid(yn)).astype(o_ref.dtype)


def conformer_conv(zc, w_dw, b_dw, g, b, nctx):
    B, T, W2 = zc.shape
    tm = ROW_TILE
    nt = T // tm
    w_pad = jnp.concatenate([w_dw, jnp.zeros((32 - CONV_KERNEL, CONV_WIDTH), F32)], axis=0)
    vec = pl.BlockSpec((1, CONV_WIDTH), lambda bb, i: (0, 0))
    return pl.pallas_call(
        functools.partial(_conv_kernel, nctx=nctx), grid=(B, nt),
        in_specs=[pl.BlockSpec((None, tm, W2), lambda bb, i: (bb, jnp.maximum(i - 1, 0), 0)),
                  pl.BlockSpec((None, tm, W2), lambda bb, i: (bb, i, 0)),
                  pl.BlockSpec((None, tm, W2), lambda bb, i: (bb, jnp.minimum(i + 1, nt - 1), 0)),
                  pl.BlockSpec((32, CONV_WIDTH), lambda bb, i: (0, 0)), vec, vec, vec],
        out_specs=pl.BlockSpec((None, tm, CONV_WIDTH), lambda bb, i: (bb, i, 0)),
        out_shape=jax.ShapeDtypeStruct((B, T, CONV_WIDTH), BF16),
        scratch_shapes=[pltpu.VMEM((tm + 2 * HALO, CONV_WIDTH), F32)],
        compiler_params=_cparams(("parallel", "parallel")), name="conformer_conv",
    )(zc, zc, zc, w_pad, b_dw.reshape(1, -1), g.reshape(1, -1), b.reshape(1, -1))


def _merge_kernel(h_ref, b0_ref, b1_ref, b2_ref, b3_ref, wg_ref, wb_ref, o_ref):
    h = h_ref[...]
    acc = None
    for n, br in enumerate((b0_ref, b1_ref, b2_ref, b3_ref)):
        gate = jax.nn.sigmoid(jnp.dot(h, wg_ref[n], preferred_element_type=F32))
        term = gate * jnp.dot(br[...], wb_ref[n], preferred_element_type=F32)
        acc = term if acc is None else acc + term
    o_ref[...] = acc.astype(o_ref.dtype)


def gated_merge(h, branches, w_gate, w_branch):
    M, D = h.shape
    N = w_gate.shape[2]
    tm = _pick(M, (FLAT_TILE, 256))
    tn = _pick(N, (512, 256))
    br_spec = pl.BlockSpec((tm, BRANCH_W), lambda j, i: (i, 0))
    return pl.pallas_call(
        _merge_kernel, grid=(N // tn, M // tm),
        in_specs=[pl.BlockSpec((tm, D), lambda j, i: (i, 0)), br_spec, br_spec, br_spec, br_spec,
                  pl.BlockSpec((N_BRANCH, D, tn), lambda j, i: (0, 0, j)),
                  pl.BlockSpec((N_BRANCH, BRANCH_W, tn), lambda j, i: (0, 0, j))],
        out_specs=pl.BlockSpec((tm, tn), lambda j, i: (i, j)),
        out_shape=jax.ShapeDtypeStruct((M, N), BF16),
        compiler_params=_cparams(("parallel", "parallel")), name="gated_merge",
    )(h, *branches, w_gate, w_branch)


_NRANK = PEER_TOPK + 1
_STAIRCASE = [(a, b) for a in range(_NRANK) for b in range(_NRANK) if (a + 1) * (b + 1) <= _NRANK]
_NCAND = -(-len(_STAIRCASE) // 8) * 8
_NRANK_PAD = -(-_NRANK // 8) * 8


def _peer_threshold_kernel(q_ref, keys_ref, e1_ref, cut_ref, e2_ref, t1_ref, t2_ref, c_ref):
    def extract(vals, dst_ref):
        v = vals
        m = prev = None
        for r in range(_NRANK):
            prev = m
            m = jnp.max(v, axis=0, keepdims=True)
            if dst_ref is not None:
                dst_ref[r:r + 1, :] = m
            v = jnp.where(v == m, NEG_INF, v)
        return prev, m

    def sub_scores(h, p):
        cols = slice((2 * h + p) * LANES, (2 * h + p + 1) * LANES)
        return lax.dot_general(keys_ref[p], q_ref[:, cols], (((1,), (1,)), ((), ())),
                               preferred_element_type=F32) * LOG2E

    for h in range(PEER_HEADS):
        u1 = sub_scores(h, 0)
        u2 = sub_scores(h, 1)
        extract(u1, t1_ref)
        extract(u2, t2_ref)
        m1 = t1_ref[0:1, :]
        m2 = t2_ref[0:1, :]
        c_ref[...] = jnp.full(c_ref.shape, NEG_INF, F32)
        for r, (a, b) in enumerate(_STAIRCASE):
            c_ref[r:r + 1, :] = (t1_ref[a:a + 1, :] - m1) + (t2_ref[b:b + 1, :] - m2)
        cand = c_ref[...]
        c16, c17 = extract(cand, None)
        thr = 0.5 * (c16 + c17)
        z = jnp.sum(jnp.where(cand >= thr, jnp.exp2(cand), 0.0), axis=0, keepdims=True)
        lz = -jnp.log2(z)
        s1f = (u1 - m1) + lz
        e1_ref[h] = jnp.exp2(s1f)
        cut_ref[h] = jnp.exp2((thr + lz) - s1f)
        e2_ref[h] = jnp.exp2(u2 - m2)


def peer_tables(q, keys):
    M = q.shape[0]
    tl = _pick(M, (256, 128))
    sf_spec = pl.BlockSpec((PEER_HEADS, PEER_NKEYS, tl), lambda i: (0, 0, i))
    sf_shape = jax.ShapeDtypeStruct((PEER_HEADS, PEER_NKEYS, M), F32)
    return pl.pallas_call(
        _peer_threshold_kernel, grid=(M // tl,),
        in_specs=[pl.BlockSpec((tl, q.shape[1]), lambda i: (i, 0)), _full(keys)],
        out_specs=[sf_spec, sf_spec, sf_spec], out_shape=[sf_shape, sf_shape, sf_shape],
        scratch_shapes=[pltpu.VMEM((_NRANK_PAD, tl), F32), pltpu.VMEM((_NRANK_PAD, tl), F32),
                        pltpu.VMEM((_NCAND, tl), F32)],
        compiler_params=_cparams(("parallel",)), name="peer_tables",
    )(q, keys)


PEER_TOKEN_SPLIT = 256
PEER_EXPERT_SPLIT = 512
PEER_EXPERT_CHUNK = 1024
PEER_TOKEN_TILE = 768


def _peer_experts_kernel(xt_ref, u_ref, vt_ref, e1_ref, cut_ref, e2_ref, o_ref, acc_ref, *scratch, ek):
    j = pl.program_id(1)
    tm = xt_ref.shape[1]
    th, es = PEER_TOKEN_SPLIT, PEER_EXPERT_SPLIT
    pieces = [(c, t) for c in range(ek // es) for t in range(tm // th)]
    st_refs, a_refs = scratch[:len(pieces)], scratch[len(pieces):]

    @pl.when(j == 0)
    def _():
        acc_ref[...] = jnp.zeros(acc_ref.shape, F32)

    def pre_activations(p):
        c, t = pieces[p]
        st_refs[p][...] = jnp.dot(u_ref[c * es:(c + 1) * es, :], xt_ref[:, t * th:(t + 1) * th],
                                  preferred_element_type=F32)

    def weights(p):
        c, t = pieces[p]
        for r in range(es // PEER_NKEYS):
            i1 = j * (ek // PEER_NKEYS) + c * (es // PEER_NKEYS) + r
            rows = slice(r * PEER_NKEYS, (r + 1) * PEER_NKEYS)
            half = slice(t * th, (t + 1) * th)
            cut_rows = [cut_ref[h, pl.ds(i1, 1), half] for h in range(PEER_HEADS)]
            e1_rows = [e1_ref[h, pl.ds(i1, 1), half] for h in range(PEER_HEADS)]
            for g in range(th // LANES):
                lanes = slice(t * th + g * LANES, t * th + (g + 1) * LANES)
                sub = slice(g * LANES, (g + 1) * LANES)
                w = None
                for h in range(PEER_HEADS):
                    e2 = e2_ref[h, :, lanes]
                    sel = jnp.where(e2 >= cut_rows[h][:, sub], e2, 0.0)
                    sel = sel * e1_rows[h][:, sub]
                    w = sel if w is None else w + sel
                pre = st_refs[p][rows, g * LANES:(g + 1) * LANES]
                act = pre * (1.0 + lax.erf(pre * (2.0 ** -0.5)))
                a_refs[p][rows, g * LANES:(g + 1) * LANES] = (w * act).astype(BF16)

    def accumulate(p):
        c, t = pieces[p]
        lanes = slice(t * th, (t + 1) * th)
        acc_ref[:, lanes] += jnp.dot(vt_ref[:, c * es:(c + 1) * es], a_refs[p][...], preferred_element_type=F32)

    for p in range(len(pieces)):
        pre_activations(p)
    for p in range(len(pieces)):
        weights(p)
        accumulate(p)

    @pl.when(j == pl.num_programs(1) - 1)
    def _():
        o_ref[...] = acc_ref[...].T


def peer_experts(xt, u, vt, e1, cut, e2):
    D, M = xt.shape
    E = u.shape[0]
    tm = _pick(M, (PEER_TOKEN_TILE, FLAT_TILE, 256))
    ek = vt.shape[2]
    th, es = PEER_TOKEN_SPLIT, PEER_EXPERT_SPLIT
    npieces = (tm // th) * (ek // es)
    table_spec = pl.BlockSpec((PEER_HEADS, PEER_NKEYS, tm), lambda i, j: (0, 0, i), pipeline_mode=pl.Buffered(1))
    return pl.pallas_call(
        functools.partial(_peer_experts_kernel, ek=ek), grid=(M // tm, E // ek),
        in_specs=[pl.BlockSpec((D, tm), lambda i, j: (0, i)),
                  pl.BlockSpec((ek, D), lambda i, j: (j, 0)),
                  pl.BlockSpec((None, D, ek), lambda i, j: (j, 0, 0)),
                  table_spec, table_spec, table_spec],
        out_specs=pl.BlockSpec((tm, D), lambda i, j: (i, 0)),
        out_shape=jax.ShapeDtypeStruct((M, D), F32),
        scratch_shapes=([pltpu.VMEM((D, tm), F32)]
                        + [pltpu.VMEM((es, th), F32)] * npieces + [pltpu.VMEM((es, th), BF16)] * npieces),
        compiler_params=_cparams(("parallel", "arbitrary")), name="peer_experts",
    )(xt, u, vt, e1, cut, e2)


def _final_kernel(x_ref, d_ref, mg_ref, g_ref, o_ref, *, gate_row):
    x = x_ref[...] + mg_ref[gate_row:gate_row + 1, :] * d_ref[...]
    o_ref[...] = x * lax.rsqrt(jnp.mean(x * x, axis=-1, keepdims=True) + EPS) * g_ref[...]


def final_norm(x, delta, mod_gate, g, nctx, gate_row):
    B, T, D = x.shape
    tm = ROW_TILE
    L = T - nctx * tm
    row_spec = pl.BlockSpec((None, tm, D), lambda b, i: (b, i + nctx, 0))
    return pl.pallas_call(
        functools.partial(_final_kernel, gate_row=gate_row), grid=(B, L // tm),
        in_specs=[row_spec, row_spec, pl.BlockSpec((None, 8, D), lambda b, i: (b, 0, 0)),
                  pl.BlockSpec((1, D), lambda b, i: (0, 0))],
        out_specs=pl.BlockSpec((None, tm, D), lambda b, i: (b, i, 0)),
        out_shape=jax.ShapeDtypeStruct((B, L, D), F32),
        compiler_params=_cparams(("parallel", "parallel")), name="final_norm",
    )(x, delta, mod_gate, g.reshape(1, D))


def _rope_tables(L, ctx_len):
    rows = L // GRID_W
    row = jnp.repeat(jnp.arange(rows, dtype=F32), GRID_W)
    col = jnp.tile(jnp.arange(GRID_W, dtype=F32), rows)
    nf = MLA_ROPE // 4
    inv = ROPE_BASE ** (-jnp.arange(nf, dtype=F32) / nf)
    ang = jnp.concatenate([row[:, None] * inv, col[:, None] * inv], axis=-1)
    cos, sin = jnp.cos(ang), jnp.sin(ang)
    cos_t = jnp.concatenate([cos, cos, cos, cos], axis=-1)
    sin_t = jnp.concatenate([-sin, sin, -sin, sin], axis=-1)
    cos_t = jnp.concatenate([jnp.ones((ctx_len, LANES), F32), cos_t], axis=0)
    sin_t = jnp.concatenate([jnp.zeros((ctx_len, LANES), F32), sin_t], axis=0)
    return cos_t, sin_t


def kernel(x, c, ctx, c_ctx, w_ada, b_ada, g_norm1, g_norm2, w_in, g_q_mla, w_uq, g_kv_mla, w_ukv, lam_q1, lam_k1, lam_q2, lam_k2, g_subln, w_dw, b_dw, g_conv_norm, b_conv_norm, w_gate, w_branch, w_out, w_peer_q, peer_keys, peer_u, peer_v, g_final):
    B, L, D = x.shape
    ctx_len = ctx.shape[1]
    depth = w_ada.shape[0]
    T = ctx_len + L
    M = B * T
    assert MLA_ROPE == DIFF_QK and ctx_len % ROW_TILE == 0 and L % ROW_TILE == 0
    nctx = ctx_len // ROW_TILE

    cos_t, sin_t = _rope_tables(L, ctx_len)
    xs = jnp.concatenate([ctx, x], axis=1)

    mods = ada_modulation(jnp.concatenate([c, c_ctx[None, :]], axis=0), w_ada, b_ada)
    mods = mods.reshape(depth, 8, 6, D)[:, :B + 1]
    mods = jnp.concatenate([mods, jnp.zeros((depth, B + 1, 2, D), F32)], axis=2)

    pending = None
    for l in range(depth):
        lam_init = 0.8 - 0.6 * math.exp(-0.3 * l)
        s2 = IN_MLA
        wi = w_in[l]
        w_mla = jnp.concatenate([wi[:, :s2], jnp.zeros((D, LANES - MLA_ROPE), F32)], axis=1).astype(BF16)
        w_diff = wi[:, s2:s2 + IN_DIFF].astype(BF16)
        w_fnet = wi[:, s2 + IN_DIFF:s2 + IN_DIFF + FNET_WIDTH].astype(BF16)
        w_conv = wi[:, s2 + IN_DIFF + FNET_WIDTH:].astype(BF16)
        hd = MLA_NOPE + MLA_ROPE
        w_uq_p = jnp.pad(w_uq[l].reshape(MLA_Q_RANK, MLA_HEADS, hd),
                         ((0, 0), (0, 0), (0, 256 - hd))).reshape(MLA_Q_RANK, MLA_HEADS * 256).astype(BF16)
        wkv = w_ukv[l].reshape(MLA_KV_RANK, MLA_HEADS, MLA_NOPE + MLA_V)
        w_ukv_p = jnp.concatenate([wkv[:, :, :MLA_NOPE].reshape(MLA_KV_RANK, -1),
                                   wkv[:, :, MLA_NOPE:].reshape(MLA_KV_RANK, -1)], axis=1).astype(BF16)
        lam_vecs = jnp.stack([lam_q1[l], lam_k1[l], lam_q2[l], lam_k2[l]], axis=0)

        if pending is None:
            h = norm_modulate(xs, g_norm1[l], mods[l], 0, 1, nctx)
        else:
            xs, h = norm_modulate(xs, g_norm1[l], mods[l], 0, 1, nctx,
                                  delta=pending[0], mod_gate=pending[1], gate_row=5)
        hf = h.reshape(M, D)
        z_mla = matmul(hf, w_mla, BF16, "in_proj_mla").reshape(B, T, -1)
        z_diff = matmul(hf, w_diff, BF16, "in_proj_diff").reshape(B, T, -1)
        z_fnet = matmul(hf, w_fnet, BF16, "in_proj_fnet").reshape(B, T, -1)
        z_conv = matmul(hf, w_conv, BF16, "in_proj_conv").reshape(B, T, -1)

        qm, km, vm, qd, kd, vd = attention_projections(z_mla, z_diff, cos_t, sin_t, g_q_mla[l], g_kv_mla[l],
                                                   w_uq_p, w_ukv_p)
        o_mla = mla_attention(qm, km, vm, ctx_len)
        o_diff = diff_attention(qd, kd, vd, lam_vecs, g_subln[l], lam_init, ctx_len)
        o_fnet = jnp.concatenate([fourier_mix_short(z_fnet[:, :ctx_len]), fourier_mix_long(z_fnet[:, ctx_len:])],
                                 axis=1)
        o_conv = conformer_conv(z_conv, w_dw[l], b_dw[l], g_conv_norm[l], b_conv_norm[l], nctx)

        y = gated_merge(hf, [o.reshape(M, BRANCH_W) for o in (o_mla, o_fnet, o_conv, o_diff)],
                        w_gate[l].astype(BF16), w_branch[l].astype(BF16))
        mix = matmul(y, w_out[l].astype(BF16), F32, "out_proj").reshape(B, T, D)
        xs, h2 = norm_modulate(xs, g_norm2[l], mods[l], 3, 4, nctx, delta=mix, mod_gate=mods[l], gate_row=2)

        h2f = h2.reshape(M, D)
        q = matmul(h2f, w_peer_q[l].astype(BF16), BF16, "peer_query")
        e1, cut, e2 = peer_tables(q, peer_keys[l].astype(BF16))
        peer_vt = jnp.swapaxes((0.5 * peer_v[l]).reshape(-1, PEER_EXPERT_CHUNK, D), 1, 2).astype(BF16)
        peer_out = peer_experts(h2f.T, peer_u[l].astype(BF16), peer_vt, e1, cut, e2)
        pending = (peer_out.reshape(B, T, D), mods[l])

    return final_norm(xs, pending[0], pending[1], g_final, nctx, 5)
```
